```python
import math
import jax, jax.numpy as jnp
from jax import lax
import numpy as np

D_MODEL = 2048
BATCH = 2
SEQ = 16384
DEPTH = 1
DEC_BATCH = 32
DEC_SEQ = 64
PAST_LEN = 2048

CHUNK = 64
Q_BLOCK = 128
D_HEAD = 128
H_SB = 8
H_SA = 8
W_SB = H_SB * D_HEAD
W_SA = H_SA * D_HEAD
H_IDX = 16
D_IDX = 64
TOPK_MAX = 256
N_BUCKETS = 32
REL_MAX_DIST = 1024
D_FF = 5632
CONV_W = 3
EPS = 1e-6
IN_SIZES = (W_SB, W_SB, W_SB, W_SA, W_SA, W_SA, H_IDX * D_IDX, D_IDX, H_IDX)
IN_COLS = sum(IN_SIZES)
IN_SPLITS = tuple(int(s) for s in np.cumsum(IN_SIZES[:-1]))

kernel_name = "streaming_stickbreak_dsa_hybrid_step"


def rmsnorm(x, g):
    xf = x.astype(jnp.float32)
    y = xf * lax.rsqrt(jnp.mean(xf * xf, axis=-1, keepdims=True) + EPS)
    return (y * g.astype(jnp.float32)).astype(x.dtype)


def chunk_limit(pos):
    return (pos // CHUNK + 1) * CHUNK


def rel_bucket(rel):
    nb = N_BUCKETS // 2
    max_exact = nb // 2
    ret = jnp.where(rel > 0, nb, 0)
    n = jnp.abs(rel)
    nf = jnp.maximum(n, 1).astype(jnp.float32)
    large = max_exact + (jnp.log(nf / max_exact) / math.log(REL_MAX_DIST / max_exact)
                         * (nb - max_exact)).astype(jnp.int32)
    large = jnp.minimum(large, nb - 1)
    return ret + jnp.where(n < max_exact, n, large)


def sb_block(q, k, v, q_pos, k_pos):
    z = jnp.einsum('bqhd,bkhd->bhqk', q, k).astype(jnp.float32) * (D_HEAD ** -0.5)
    mask = k_pos[None, :] < q_pos[:, None]
    log_keep = jnp.where(mask, jax.nn.log_sigmoid(-z), 0.0)
    after = lax.cumsum(log_keep, axis=3, reverse=True) - log_keep
    a = jnp.where(mask, jnp.exp(jax.nn.log_sigmoid(z) + after), 0.0)
    return jnp.einsum('bhqk,bkhd->bqhd', a.astype(v.dtype), v)


def dsa_block(q, qi, wi, q_pos, k, v, ki, k_pos, rel_table, topk):
    lim = chunk_limit(q_pos)
    admissible = k_pos[None, :] < lim[:, None]
    s_idx = jnp.einsum('bqhe,bke->bqhk', qi, ki).astype(jnp.float32) * (D_IDX ** -0.5)
    score = jnp.einsum('bqh,bqhk->bqk', wi.astype(jnp.float32) * (H_IDX ** -0.5),
                       jax.nn.relu(s_idx))
    score = jnp.where(admissible[None], score, -jnp.inf)
    _, sel = lax.top_k(score, topk)
    sel_pos = k_pos[sel]
    valid = sel_pos < lim[None, :, None]
    bidx = jnp.arange(k.shape[0])[:, None, None]
    kg = k[bidx, sel]
    vg = v[bidx, sel]
    logits = jnp.einsum('bqhd,bqkhd->bhqk', q, kg).astype(jnp.float32) * (D_HEAD ** -0.5)
    bias = rel_table.astype(jnp.float32)[rel_bucket(sel_pos - q_pos[None, :, None])]
    logits = logits + jnp.transpose(bias, (0, 3, 1, 2))
    logits = jnp.where(valid[:, None], logits, -jnp.inf)
    p = jax.nn.softmax(logits, axis=-1)
    return jnp.einsum('bhqk,bqkhd->bqhd', p.astype(vg.dtype), vg)


def sweep(fn, qs, q_pos, kvs):
    B, Lq = qs[0].shape[:2]
    qb = min(Q_BLOCK, Lq)
    nblk = Lq // qb

    def step(args):
        b, i = args
        q_blk = tuple(lax.dynamic_slice_in_dim(a[b], i * qb, qb, axis=0)[None] for a in qs)
        kv_b = tuple(a[b][None] for a in kvs)
        pos = lax.dynamic_slice_in_dim(q_pos, i * qb, qb)
        return fn(q_blk, pos, kv_b)[0]

    bs, ib = jnp.meshgrid(jnp.arange(B), jnp.arange(nblk), indexing='ij')
    out = lax.map(step, (bs.reshape(-1), ib.reshape(-1)))
    return out.reshape((B, Lq) + out.shape[2:])


def layer(x, c, past, w_ada, b_ada, g_mix, w_in, w_gate, w_br_sb, w_br_sa, w_out,
          rel_table, g_ffn, w_up, conv_w, conv_b, w_down):
    B, L, _ = x.shape
    past_len = 0 if past is None else past[0].shape[1]
    mod = jax.nn.silu(c) @ w_ada + b_ada
    sh1, sc1, gt1, sh2, sc2, gt2 = jnp.split(mod[:, None, :], 6, axis=-1)
    h = rmsnorm(x, g_mix) * (1 + sc1) + sh1
    q_sb, k_sb, v_sb, q_sa, k_sa, v_sa, q_ix, k_ix, w_ix = jnp.split(h @ w_in, IN_SPLITS, axis=-1)
    q_sb, k_sb, v_sb = (a.reshape(B, L, H_SB, D_HEAD) for a in (q_sb, k_sb, v_sb))
    q_sa, k_sa, v_sa = (a.reshape(B, L, H_SA, D_HEAD) for a in (q_sa, k_sa, v_sa))
    q_ix = q_ix.reshape(B, L, H_IDX, D_IDX)
    if past is None:
        k_sb_all, v_sb_all, k_sa_all, v_sa_all, k_ix_all = k_sb, v_sb, k_sa, v_sa, k_ix
        prev = jnp.zeros((B, CONV_W - 1, D_FF), x.dtype)
    else:
        k_sb_all = jnp.concatenate([past[0], k_sb], axis=1)
        v_sb_all = jnp.concatenate([past[1], v_sb], axis=1)
        k_sa_all = jnp.concatenate([past[2], k_sa], axis=1)
        v_sa_all = jnp.concatenate([past[3], v_sa], axis=1)
        k_ix_all = jnp.concatenate([past[4], k_ix], axis=1)
        prev = past[5]
    l_keys = past_len + L
    k_pos = jnp.arange(l_keys, dtype=jnp.int32)
    q_pos = jnp.arange(past_len, l_keys, dtype=jnp.int32)
    topk = min(TOPK_MAX, l_keys // 4)
    o_sb = sweep(lambda q, p, kv: sb_block(q[0], kv[0], kv[1], p, k_pos),
                 (q_sb,), q_pos, (k_sb_all, v_sb_all))
    o_sa = sweep(lambda q, p, kv: dsa_block(q[0], q[1], q[2], p, kv[0], kv[1], kv[2],
                                            k_pos, rel_table, topk),
                 (q_sa, q_ix, w_ix), q_pos, (k_sa_all, v_sa_all, k_ix_all))
    g = jax.nn.sigmoid((h @ w_gate).astype(jnp.float32)).astype(x.dtype)
    g_sb, g_sa = jnp.split(g, 2, axis=-1)
    merged = (g_sb * (o_sb.reshape(B, L, W_SB) @ w_br_sb)
              + g_sa * (o_sa.reshape(B, L, W_SA) @ w_br_sa))
    x = x + gt1 * (merged @ w_out)
    h2 = rmsnorm(x, g_ffn) * (1 + sc2) + sh2
    u_g, u_v = jnp.split(h2 @ w_up, 2, axis=-1)
    ext = jnp.concatenate([prev.astype(u_g.dtype), u_g], axis=1)
    conv = conv_b + conv_w[0] * ext[:, 0:L]
    for j in range(1, CONV_W):
        conv = conv + conv_w[j] * ext[:, j:j + L]
    x = x + gt2 * ((jax.nn.silu(conv) * u_v) @ w_down)
    return x, (k_sb, v_sb, k_sa, v_sa, k_ix, ext[:, L:])


def setup_inputs(seed: int = 0) -> dict:
    key = jax.random.key(seed)
    ks = iter(jax.random.split(key, 32))

    def nrm(shape, s=1.0):
        return s * jax.random.normal(next(ks), shape, jnp.float32)

    D = D_MODEL
    return {
        "x_prompt": nrm((BATCH, SEQ, D)),
        "x_sample": nrm((DEC_BATCH, DEC_SEQ, D)),
        "cache_sb_k": nrm((DEPTH, DEC_BATCH, PAST_LEN, H_SB, D_HEAD)),
        "cache_sb_v": nrm((DEPTH, DEC_BATCH, PAST_LEN, H_SB, D_HEAD)),
        "cache_sa_k": nrm((DEPTH, DEC_BATCH, PAST_LEN, H_SA, D_HEAD)),
        "cache_sa_v": nrm((DEPTH, DEC_BATCH, PAST_LEN, H_SA, D_HEAD)),
        "cache_idx_k": nrm((DEPTH, DEC_BATCH, PAST_LEN, D_IDX)),
        "state_ffn_conv": nrm((DEPTH, DEC_BATCH, CONV_W - 1, D_FF)),
        "c_prompt": nrm((BATCH, D)),
        "c_sample": nrm((DEC_BATCH, D)),
        "w_ada": nrm((DEPTH, D, 6 * D), 0.5 * D ** -0.5),
        "b_ada": nrm((DEPTH, 6 * D), 0.02),
        "g_mix": 1.0 + nrm((DEPTH, D), 0.02),
        "w_in": nrm((DEPTH, D, IN_COLS), D ** -0.5),
        "w_gate": nrm((DEPTH, D, 2 * D), D ** -0.5),
        "w_br_sb": nrm((DEPTH, W_SB, D), W_SB ** -0.5),
        "w_br_sa": nrm((DEPTH, W_SA, D), W_SA ** -0.5),
        "w_out": nrm((DEPTH, D, D), D ** -0.5),
        "rel_table": nrm((N_BUCKETS, H_SA), 0.5),
        "g_ffn": 1.0 + nrm((DEPTH, D), 0.02),
        "w_up": nrm((DEPTH, D, 2 * D_FF), D ** -0.5),
        "conv_w": nrm((DEPTH, CONV_W, D_FF), CONV_W ** -0.5),
        "conv_b": nrm((DEPTH, D_FF), 0.02),
        "w_down": nrm((DEPTH, D_FF, D), D_FF ** -0.5),
        "g_final": 1.0 + nrm((D,), 0.02),
    }


def reference(x_prompt, x_sample, cache_sb_k, cache_sb_v, cache_sa_k, cache_sa_v,
              cache_idx_k, state_ffn_conv, c_prompt, c_sample, w_ada, b_ada, g_mix,
              w_in, w_gate, w_br_sb, w_br_sa, w_out, rel_table, g_ffn, w_up, conv_w,
              conv_b, w_down, g_final):
    xp, xs = x_prompt, x_sample
    new_p, new_s = [], []
    for l in range(DEPTH):
        w = (w_ada[l], b_ada[l], g_mix[l], w_in[l], w_gate[l], w_br_sb[l], w_br_sa[l],
             w_out[l], rel_table, g_ffn[l], w_up[l], conv_w[l], conv_b[l], w_down[l])
        xp, sp = layer(xp, c_prompt, None, *w)
        past = (cache_sb_k[l], cache_sb_v[l], cache_sa_k[l], cache_sa_v[l],
                cache_idx_k[l], state_ffn_conv[l])
        xs, ss = layer(xs, c_sample, past, *w)
        new_p.append(sp)
        new_s.append(ss)
    y_prompt = rmsnorm(xp, g_final)
    y_sample = rmsnorm(xs, g_final)
    p_sb_k = jnp.stack([s[0] for s in new_p])
    p_sb_v = jnp.stack([s[1] for s in new_p])
    p_sa_k = jnp.stack([s[2] for s in new_p])
    p_sa_v = jnp.stack([s[3] for s in new_p])
    p_idx_k = jnp.stack([s[4] for s in new_p])
    p_conv = jnp.stack([s[5] for s in new_p])
    s_sb_k = jnp.stack([s[0] for s in new_s])
    s_sb_v = jnp.stack([s[1] for s in new_s])
    s_sa_k = jnp.stack([s[2] for s in new_s])
    s_sa_v = jnp.stack([s[3] for s in new_s])
    s_idx_k = jnp.stack([s[4] for s in new_s])
    s_conv = jnp.stack([s[5] for s in new_s])
    return (y_prompt, y_sample, p_sb_k, p_sb_v, p_sa_k, p_sa_v, p_idx_k, p_conv,
            s_sb_k, s_sb_v, s_sa_k, s_sa_v, s_idx_k, s_conv)
```

```python
import functools

import numpy as np
import jax
import jax.numpy as jnp
from jax import lax
from jax.experimental import pallas as pl
from jax.experimental.pallas import tpu as pltpu

F32 = jnp.float32
BF16 = jnp.bfloat16
I32 = jnp.int32

CHUNK = 64
D_HEAD = 128
H_SB = 8
H_SA = 8
W_SB = H_SB * D_HEAD
W_SA = H_SA * D_HEAD
H_IDX = 16
D_IDX = 64
TOPK_MAX = 256
N_BUCKETS = 32
REL_MAX_DIST = 1024
CONV_W = 3
EPS = 1e-6

COL_Q_SB = 0
COL_K_SB = W_SB
COL_V_SB = 2 * W_SB
COL_Q_SA = 3 * W_SB
COL_K_SA = 3 * W_SB + W_SA
COL_V_SA = 3 * W_SB + 2 * W_SA
COL_Q_IX = 3 * W_SB + 3 * W_SA
COL_K_IX = COL_Q_IX + H_IDX * D_IDX
COL_W_IX = COL_K_IX + D_IDX
IN_COLS = COL_W_IX + H_IDX

LANE = 128
VMEM_LIMIT = 56 * 1024 * 1024

ATT_SCALE = D_HEAD ** -0.5
IDX_SCALE = (D_IDX ** -0.5) * (H_IDX ** -0.5)
NEG = -1e30
INT_MIN = -2 ** 31
SB_DEAD = -104.0


def _cparams(sem):
    return pltpu.CompilerParams(dimension_semantics=sem, vmem_limit_bytes=VMEM_LIMIT)


def _adaln_kernel(c_ref, w_ref, b_ref, o_ref):
    c = c_ref[...]
    a = c * jax.nn.sigmoid(c)
    o_ref[...] = jnp.dot(a.astype(BF16), w_ref[...].astype(BF16),
                         preferred_element_type=F32) + b_ref[...]


def _adaln(c, w, b):
    R, D = c.shape
    N = w.shape[1]
    tn = 1024
    return pl.pallas_call(
        _adaln_kernel,
        grid=(N // tn,),
        in_specs=[pl.BlockSpec((R, D), lambda n: (0, 0)),
                  pl.BlockSpec((D, tn), lambda n: (0, n)),
                  pl.BlockSpec((1, tn), lambda n: (0, n))],
        out_specs=pl.BlockSpec((R, tn), lambda n: (0, n)),
        out_shape=jax.ShapeDtypeStruct((R, N), F32),
        compiler_params=_cparams(("arbitrary",)),
        name="adaln",
    )(c, w, b)


def _norm_mod(x, g, sc, sh):
    ms = jnp.mean(x * x, axis=-1, keepdims=True)
    y = x * lax.rsqrt(ms + EPS) * g
    return y * (1.0 + sc) + sh


def _proj_kernel(x_ref, sc_ref, sh_ref, g_ref, w_ref, o_ref, h_ref, *, sigmoid):
    bt, lt, D = x_ref.shape

    @pl.when(pl.program_id(2) == 0)
    def _():
        h = _norm_mod(x_ref[...], g_ref[...], sc_ref[...], sh_ref[...])
        h_ref[...] = h.reshape(bt * lt, D).astype(BF16)

    r = jnp.dot(h_ref[...], w_ref[...], preferred_element_type=F32)
    if sigmoid:
        r = jax.nn.sigmoid(r)
    o_ref[...] = r.reshape(bt, lt, r.shape[-1])


def _proj(x, sc, sh, g, w, *, bt, lt, tn, sigmoid):
    B, L, D = x.shape
    N = w.shape[1]
    return pl.pallas_call(
        functools.partial(_proj_kernel, sigmoid=sigmoid),
        grid=(B // bt, L // lt, N // tn),
        in_specs=[pl.BlockSpec((bt, lt, D), lambda b, t, n: (b, t, 0)),
                  pl.BlockSpec((bt, 1, D), lambda b, t, n: (b, 0, 0)),
                  pl.BlockSpec((bt, 1, D), lambda b, t, n: (b, 0, 0)),
                  pl.BlockSpec((1, D), lambda b, t, n: (0, 0)),
                  pl.BlockSpec((D, tn), lambda b, t, n: (0, n))],
        out_specs=pl.BlockSpec((bt, lt, tn), lambda b, t, n: (b, t, n)),
        out_shape=jax.ShapeDtypeStruct((B, L, N), F32),
        scratch_shapes=[pltpu.VMEM((bt * lt, D), BF16)],
        compiler_params=_cparams(("arbitrary", "arbitrary", "arbitrary")),
        name="proj_gate" if sigmoid else "proj_in",
    )(x, sc, sh, g, w)


def _sb_kernel(q_ref, k_ref, v_ref, u_ref, o_ref, *, tq, tk, past_len):
    i = pl.program_id(2)
    q = q_ref[...].astype(BF16)
    qpos0 = past_len + i * tq
    qpos = qpos0 + lax.broadcasted_iota(I32, (tq, 1), 0)
    j0 = (qpos0 + tq - 2) // tk
    u = u_ref[...]

    def cond(carry):
        j, go, _, _ = carry
        return jnp.logical_and(j >= 0, go)

    def body(carry):
        j, _, cum, acc = carry
        off = pl.multiple_of(j * tk, tk)
        k = k_ref[pl.ds(off, tk), :]
        v = v_ref[pl.ds(off, tk), :]
        z = lax.dot_general(q, k, (((1,), (1,)), ((), ())),
                            preferred_element_type=F32) * ATT_SCALE
        kpos = j * tk + lax.broadcasted_iota(I32, (1, tk), 1)
        mask = kpos < qpos
        t = jnp.log1p(jnp.exp(-jnp.abs(z)))
        lk = jnp.where(mask, -(jnp.maximum(z, 0.0) + t), 0.0)
        hi = lk.astype(BF16)
        lo = (lk - hi.astype(F32)).astype(BF16)
        after = cum + (jnp.dot(hi, u, preferred_element_type=F32)
                       + jnp.dot(lo, u, preferred_element_type=F32))
        lsz = jnp.minimum(z, 0.0) - t
        a = jnp.where(mask, jnp.exp(lsz + after), 0.0)
        acc = acc + jnp.dot(a.astype(BF16), v, preferred_element_type=F32)
        cum = cum + jnp.sum(lk, axis=1, keepdims=True)
        go = jnp.max(cum) > SB_DEAD
        return j - 1, go, cum, acc

    init = (j0, jnp.bool_(True), jnp.zeros((tq, 1), F32), jnp.zeros((tq, D_HEAD), F32))
    _, _, _, acc = lax.while_loop(cond, body, init)
    o_ref[...] = acc.astype(o_ref.dtype)


def _sb_attention(proj, k_all, v_all, *, tq, tk, past_len):
    B, L, _ = proj.shape
    Lkp = k_all.shape[1]
    r = np.arange(tk)
    u = jnp.asarray((r[:, None] > r[None, :]).astype(np.float32), BF16)
    return pl.pallas_call(
        functools.partial(_sb_kernel, tq=tq, tk=tk, past_len=past_len),
        grid=(B, H_SB, L // tq),
        in_specs=[pl.BlockSpec((None, tq, D_HEAD), lambda b, h, i: (b, i, h)),
                  pl.BlockSpec((None, Lkp, D_HEAD), lambda b, h, i: (b, 0, h)),
                  pl.BlockSpec((None, Lkp, D_HEAD), lambda b, h, i: (b, 0, h)),
                  pl.BlockSpec((tk, tk), lambda b, h, i: (0, 0))],
        out_specs=pl.BlockSpec((None, tq, D_HEAD), lambda b, h, i: (b, i, h)),
        out_shape=jax.ShapeDtypeStruct((B, L, W_SB), BF16),
        compiler_params=_cparams(("arbitrary", "arbitrary", "arbitrary")),
        name="sb_attn",
    )(proj, k_all, v_all, u)


def _bucket_edges():
    nb = N_BUCKETS // 2
    max_exact = nb // 2

    def bucket(rel):
        n = abs(rel)
        if n < max_exact:
            v = n
        else:
            v = max_exact + int(np.log(np.float32(n) / max_exact)
                                / np.log(REL_MAX_DIST / max_exact) * (nb - max_exact))
            v = min(v, nb - 1)
        return (nb if rel > 0 else 0) + v

    lo = -4 * REL_MAX_DIST
    assert bucket(lo) == nb - 1
    edges = []
    prev = bucket(lo)
    for rel in range(lo + 1, CHUNK):
        bk = bucket(rel)
        if bk != prev:
            edges.append((rel, prev))
            prev = bk
    edges.append((CHUNK, prev))
    return edges


_EDGES = _bucket_edges()
_FAR_BUCKET = _EDGES[0][1]
_NEAR_REL = _EDGES[0][0]


def _bias_kernel(d_ref, rel_ref, o_ref):
    tq, tk = o_ref.shape
    h = pl.program_id(1)
    rel = (lax.broadcasted_iota(I32, (tq, tk), 1) - lax.broadcasted_iota(I32, (tq, tk), 0)
           - d_ref[pl.program_id(0)])
    tile = jnp.full((tq, tk), rel_ref[_EDGES[-1][1], h], F32)
    for edge, bk in reversed(_EDGES[:-1]):
        tile = jnp.where(rel < edge, rel_ref[bk, h], tile)
    o_ref[...] = tile


def _bias_tiles(ds, rel_table, *, tq, tk):
    return pl.pallas_call(
        _bias_kernel,
        grid=(len(ds), H_SA),
        in_specs=[pl.BlockSpec(memory_space=pltpu.SMEM), pl.BlockSpec(memory_space=pltpu.SMEM)],
        out_specs=pl.BlockSpec((None, None, tq, tk), lambda n, h: (n, h, 0, 0)),
        out_shape=jax.ShapeDtypeStruct((len(ds), H_SA, tq, tk), F32),
        compiler_params=_cparams(("arbitrary", "arbitrary")),
        name="bias_tiles",
    )(jnp.asarray(np.asarray(ds, np.int32)), rel_table)


def _dsa_kernel(itab, jtab, ftab, ltab, ntab, nbtab,
                qi_ref, wix_ref, qsa_ref, ki_ref, k_ref, v_ref, bias_ref, o_ref,
                keys_ref, thr_ref, m_ref, l_ref, acc_ref, qb_ref,
                *, tq, tk, past_len, rows):
    p = pl.program_id(1)
    i = itab[p]
    j = jtab[p]
    qpos0 = past_len + i * tq
    qpos = qpos0 + lax.broadcasted_iota(I32, (tq, 1), 0)
    lim = (qpos // CHUNK + 1) * CHUNK

    @pl.when(ftab[p] == 1)
    def _():
        nblk = nbtab[p]
        qi = qi_ref[...].astype(BF16)
        w = wix_ref[:, D_IDX:D_IDX + H_IDX] * IDX_SCALE

        def score_blk(jj, c):
            kib = ki_ref[pl.ds(pl.multiple_of(jj * tk, tk), tk), :]
            s = jnp.zeros((tq, tk), F32)
            for h in range(H_IDX):
                sh = lax.dot_general(qi[:, h * D_IDX:(h + 1) * D_IDX], kib,
                                     (((1,), (1,)), ((), ())), preferred_element_type=F32)
                s = s + w[:, h:h + 1] * jnp.maximum(sh, 0.0)
            bits = pltpu.bitcast(s, I32)
            key = jnp.where(bits < 0, bits ^ jnp.int32(0x7FFFFFFF), bits)
            kpos = jj * tk + lax.broadcasted_iota(I32, (1, tk), 1)
            keys_ref[jj] = jnp.where(kpos < lim, key, jnp.int32(INT_MIN))
            return c

        lax.fori_loop(0, nblk, score_blk, 0)

        for r0 in range(0, tq, rows):
            need = jnp.minimum(TOPK_MAX, lim[r0:r0 + rows]).astype(F32)

            def bit_step(it, tpre):
                bitv = lax.shift_left(jnp.int32(1), 31 - it)
                cand = (tpre | bitv) ^ jnp.int32(INT_MIN)

                def cnt_blk(jj, c):
                    kb = keys_ref[jj, r0:r0 + rows, :]
                    return c + jnp.where(kb >= cand, 1.0, 0.0)

                c = lax.fori_loop(0, nblk, cnt_blk, jnp.zeros((rows, tk), F32))
                cnt = jnp.sum(c, axis=1, keepdims=True)
                return jnp.where(cnt >= need, tpre | bitv, tpre)

            tpre = lax.fori_loop(0, 32, bit_step, jnp.zeros((rows, 1), I32))
            thr_ref[r0:r0 + rows, :] = tpre ^ jnp.int32(INT_MIN)

        m_ref[...] = jnp.full(m_ref.shape, NEG, F32)
        l_ref[...] = jnp.zeros(l_ref.shape, F32)
        acc_ref[...] = jnp.zeros(acc_ref.shape, F32)
        qb_ref[...] = qsa_ref[...].astype(BF16)

    madd = jnp.where(keys_ref[j] >= thr_ref[...], 0.0, NEG)
    for h in range(H_SA):
        cs = slice(h * D_HEAD, (h + 1) * D_HEAD)
        s = lax.dot_general(qb_ref[:, cs], k_ref[:, cs], (((1,), (1,)), ((), ())),
                            preferred_element_type=F32)
        s = s * ATT_SCALE + (bias_ref[h] + madd)
        m_prev = m_ref[h]
        m_new = jnp.maximum(m_prev, jnp.max(s, axis=1, keepdims=True))
        alpha = jnp.exp(m_prev - m_new)
        pr = jnp.exp(s - m_new)
        l_ref[h] = alpha * l_ref[h] + jnp.sum(pr, axis=1, keepdims=True)
        acc_ref[:, cs] = alpha * acc_ref[:, cs] + jnp.dot(
            pr.astype(BF16), v_ref[:, cs], preferred_element_type=F32)
        m_ref[h] = m_new

    @pl.when(ltab[p] == 1)
    def _():
        for h in range(H_SA):
            cs = slice(h * D_HEAD, (h + 1) * D_HEAD)
            o_ref[:, cs] = (acc_ref[:, cs] / l_ref[h]).astype(o_ref.dtype)


def _dsa_attention(proj, ki_all, k_all, v_all, rel_table, *, tq, tk, past_len):
    B, L, _ = proj.shape
    Lkp = k_all.shape[1]
    nq = L // tq
    it, jt, ft, lt_, nt, nbt = [], [], [], [], [], []
    near_ds = []
    for i in range(nq):
        qpos0 = past_len + i * tq
        lim_max = ((qpos0 + tq - 1) // CHUNK + 1) * CHUNK
        jmax = (lim_max - 1) // tk
        for j in range(jmax + 1):
            d = qpos0 - j * tk
            if (tk - 1) - d >= _NEAR_REL:
                if d not in near_ds:
                    near_ds.append(d)
                nt.append(near_ds.index(d))
            else:
                nt.append(-1)
            it.append(i); jt.append(j); ft.append(int(j == 0)); lt_.append(int(j == jmax))
            nbt.append(jmax + 1)
    n_near = len(near_ds)
    nt = [n_near if v < 0 else v for v in nt]
    bias = _bias_tiles(near_ds + [tk - _NEAR_REL], rel_table, tq=tq, tk=tk)
    tabs = [jnp.asarray(np.asarray(t, np.int32)) for t in (it, jt, ft, lt_, nt, nbt)]
    rows = min(tq, 64)
    qi_blk = COL_Q_IX // (H_IDX * D_IDX)
    wix_blk = COL_K_IX // LANE
    qsa_blk = COL_Q_SA // W_SA
    assert COL_Q_IX % (H_IDX * D_IDX) == 0 and COL_K_IX % LANE == 0 and COL_Q_SA % W_SA == 0
    grid_spec = pltpu.PrefetchScalarGridSpec(
        num_scalar_prefetch=6,
        grid=(B, len(it)),
        in_specs=[
            pl.BlockSpec((None, tq, H_IDX * D_IDX), lambda b, p, it, jt, *_: (b, it[p], qi_blk)),
            pl.BlockSpec((None, tq, LANE), lambda b, p, it, jt, *_: (b, it[p], wix_blk)),
            pl.BlockSpec((None, tq, W_SA), lambda b, p, it, jt, *_: (b, it[p], qsa_blk)),
            pl.BlockSpec((None, Lkp, D_IDX), lambda b, p, it, jt, *_: (b, 0, 0)),
            pl.BlockSpec((None, tk, W_SA), lambda b, p, it, jt, *_: (b, jt[p], 0)),
            pl.BlockSpec((None, tk, W_SA), lambda b, p, it, jt, *_: (b, jt[p], 0)),
            pl.BlockSpec((None, H_SA, tq, tk),
                         lambda b, p, it, jt, ft, lt, nt, nbt: (nt[p], 0, 0, 0)),
        ],
        out_specs=pl.BlockSpec((None, tq, W_SA), lambda b, p, it, jt, *_: (b, it[p], 0)),
        scratch_shapes=[
            pltpu.VMEM((Lkp // tk, tq, tk), I32),
            pltpu.VMEM((tq, 1), I32),
            pltpu.VMEM((H_SA, tq, 1), F32),
            pltpu.VMEM((H_SA, tq, 1), F32),
            pltpu.VMEM((tq, W_SA), F32),
            pltpu.VMEM((tq, W_SA), BF16),
        ],
    )
    return pl.pallas_call(
        functools.partial(_dsa_kernel, tq=tq, tk=tk, past_len=past_len, rows=rows),
        grid_spec=grid_spec,
        out_shape=jax.ShapeDtypeStruct((B, L, W_SA), BF16),
        compiler_params=_cparams(("arbitrary", "arbitrary")),
        name="dsa_attn",
    )(*tabs, proj, proj, proj, ki_all, k_all, v_all, bias)


def _merge_kernel(osb_ref, osa_ref, g_ref, x_ref, gt_ref, wsb_ref, wsa_ref, wo_ref, o_ref):
    bt, lt, D = x_ref.shape
    tm = bt * lt
    a = jnp.dot(osb_ref[...].reshape(tm, W_SB), wsb_ref[...], preferred_element_type=F32)
    c = jnp.dot(osa_ref[...].reshape(tm, W_SA), wsa_ref[...], preferred_element_type=F32)
    g = g_ref[...].reshape(tm, 2 * D)
    merged = g[:, :D] * a + g[:, D:] * c
    y = jnp.dot(merged.astype(BF16), wo_ref[...], preferred_element_type=F32)
    o_ref[...] = x_ref[...] + gt_ref[...] * y.reshape(bt, lt, D)


def _merge(osb, osa, g, x, gt, wsb, wsa, wo, *, bt, lt):
    B, L, D = x.shape
    row = lambda w: pl.BlockSpec((bt, lt, w), lambda b, t: (b, t, 0))
    full = lambda a: pl.BlockSpec(a.shape, lambda b, t: (0, 0), pipeline_mode=pl.Buffered(1))
    return pl.pallas_call(
        _merge_kernel,
        grid=(B // bt, L // lt),
        in_specs=[row(W_SB), row(W_SA), row(2 * D), row(D),
                  pl.BlockSpec((bt, 1, D), lambda b, t: (b, 0, 0)),
                  full(wsb), full(wsa), full(wo)],
        out_specs=row(D),
        out_shape=jax.ShapeDtypeStruct((B, L, D), F32),
        compiler_params=_cparams(("arbitrary", "arbitrary")),
        name="merge_out",
    )(osb, osa, g, x, gt, wsb, wsa, wo)


HALO = 8


def _ffn_kernel(x_ref, sc_ref, sh_ref, gt_ref, g_ref, gfin_ref, prev_ref, wg_ref, wv_ref,
                cw_ref, cb_ref, wd_ref, x2_ref, st_ref,
                h_ref, acc_ref, ext_ref, carry_ref, *, final):
    bt, lt, D = x_ref.shape
    tm = bt * lt
    fc = wg_ref.shape[1]
    t = pl.program_id(1)
    f = pl.program_id(2)

    @pl.when(f == 0)
    def _():
        h = _norm_mod(x_ref[...], g_ref[...], sc_ref[...], sh_ref[...])
        h_ref[...] = h.reshape(tm, D).astype(BF16)
        acc_ref[...] = jnp.zeros(acc_ref.shape, F32)

    h = h_ref[...]
    ug = jnp.dot(h, wg_ref[...], preferred_element_type=F32).reshape(bt, lt, fc)
    uv = jnp.dot(h, wv_ref[...], preferred_element_type=F32).reshape(bt, lt, fc)

    @pl.when(t == 0)
    def _():
        ext_ref[:, HALO - 2:HALO, :] = prev_ref[...]

    @pl.when(t > 0)
    def _():
        ext_ref[:, HALO - 2:HALO, :] = carry_ref[f]

    ext_ref[:, HALO:HALO + lt, :] = ug
    tail = ug[:, lt - 2:lt, :]
    carry_ref[f] = tail
    st_ref[:, f] = tail
    cw = cw_ref[...]
    conv = (cb_ref[...] + cw[0:1, :] * ext_ref[:, HALO - 2:HALO - 2 + lt, :]
            + cw[1:2, :] * ext_ref[:, HALO - 1:HALO - 1 + lt, :] + cw[2:3, :] * ug)
    act = conv * jax.nn.sigmoid(conv) * uv
    acc_ref[...] += jnp.dot(act.reshape(tm, fc).astype(BF16), wd_ref[...],
                            preferred_element_type=F32)

    @pl.when(f == pl.num_programs(2) - 1)
    def _():
        x2 = x_ref[...] + gt_ref[...] * acc_ref[...].reshape(bt, lt, D)
        if final:
            ms = jnp.mean(x2 * x2, axis=-1, keepdims=True)
            x2 = x2 * lax.rsqrt(ms + EPS) * gfin_ref[...]
        x2_ref[...] = x2


def _ffn(x, sc, sh, gt, g, gfin, prev, w_up, cw, cb, w_down, *, bt, lt, fc, final):
    B, L, D = x.shape
    F = w_down.shape[0]
    nf = F // fc
    row = pl.BlockSpec((bt, lt, D), lambda b, t, f: (b, t, 0))
    mod = pl.BlockSpec((bt, 1, D), lambda b, t, f: (b, 0, 0))
    vec = pl.BlockSpec((1, D), lambda b, t, f: (0, 0))
    return pl.pallas_call(
        functools.partial(_ffn_kernel, final=final),
        grid=(B // bt, L // lt, nf),
        in_specs=[row, mod, mod, mod, vec, vec,
                  pl.BlockSpec((bt, CONV_W - 1, fc), lambda b, t, f: (b, 0, f)),
                  pl.BlockSpec((D, fc), lambda b, t, f: (0, f)),
                  pl.BlockSpec((D, fc), lambda b, t, f: (0, nf + f)),
                  pl.BlockSpec((CONV_W, fc), lambda b, t, f: (0, f)),
                  pl.BlockSpec((1, fc), lambda b, t, f: (0, f)),
                  pl.BlockSpec((fc, D), lambda b, t, f: (f, 0))],
        out_specs=[row, pl.BlockSpec((bt, nf, CONV_W - 1, fc), lambda b, t, f: (b, 0, 0, 0))],
        out_shape=[jax.ShapeDtypeStruct((B, L, D), F32),
                   jax.ShapeDtypeStruct((B, nf, CONV_W - 1, fc), F32)],
        scratch_shapes=[pltpu.VMEM((bt * lt, D), BF16),
                        pltpu.VMEM((bt * lt, D), F32),
                        pltpu.VMEM((bt, HALO + lt, fc), F32),
                        pltpu.VMEM((nf, bt, CONV_W - 1, fc), F32)],
        compiler_params=_cparams(("arbitrary", "arbitrary", "arbitrary")),
        name="conv_ffn",
    )(x, sc, sh, gt, g, gfin, prev, w_up, w_up, cw, cb, w_down)


def _pad_cols(w, mult):
    n = w.shape[1]
    npad = -(-n // mult) * mult
    return jnp.pad(w, ((0, 0), (0, npad - n)))


def _tiles(B, L, rows):
    if L >= rows:
        assert L % rows == 0
        return 1, rows
    assert rows % L == 0 and B % (rows // L) == 0
    return rows // L, L


def _layer(x, mod, past, wts, *, last, g_final):
    (g_mix, w_in, w_gate, w_br_sb, w_br_sa, w_out, rel_table, g_ffn, w_up, conv_w, conv_b,
     w_down) = wts
    B, L, D = x.shape
    sh1, sc1, gt1, sh2, sc2, gt2 = (m[:, None, :] for m in jnp.split(mod, 6, axis=-1))
    past_len = 0 if past is None else past[0].shape[1]

    bt, lt = _tiles(B, L, 1024)
    proj = _proj(x, sc1, sh1, g_mix, w_in, bt=bt, lt=lt, tn=512, sigmoid=False)
    gate = _proj(x, sc1, sh1, g_mix, w_gate, bt=bt, lt=lt, tn=512, sigmoid=True)

    k_sb = proj[..., COL_K_SB:COL_K_SB + W_SB]
    v_sb = proj[..., COL_V_SB:COL_V_SB + W_SB]
    k_sa = proj[..., COL_K_SA:COL_K_SA + W_SA]
    v_sa = proj[..., COL_V_SA:COL_V_SA + W_SA]
    k_ix = proj[..., COL_K_IX:COL_K_IX + D_IDX]

    tk = 256
    tq = min(256, L)
    l_keys = past_len + L
    lkp = -(-l_keys // tk) * tk

    def keys_of(new, old):
        parts = [] if old is None else [old.reshape(B, past_len, -1).astype(BF16)]
        parts.append(new.astype(BF16))
        if lkp > l_keys:
            parts.append(jnp.zeros((B, lkp - l_keys, new.shape[-1]), BF16))
        return parts[0] if len(parts) == 1 else jnp.concatenate(parts, axis=1)

    old = (None,) * 5 if past is None else past[:5]
    o_sb = _sb_attention(proj, keys_of(k_sb, old[0]), keys_of(v_sb, old[1]),
                         tq=tq, tk=tk, past_len=past_len)
    o_sa = _dsa_attention(proj, keys_of(k_ix, old[4]), keys_of(k_sa, old[2]),
                          keys_of(v_sa, old[3]), rel_table, tq=tq, tk=tk, past_len=past_len)

    bt, lt = _tiles(B, L, 256)
    x1 = _merge(o_sb, o_sa, gate, x, gt1, w_br_sb, w_br_sa, w_out, bt=bt, lt=lt)

    prev = jnp.zeros((B, CONV_W - 1, w_down.shape[0]), F32) if past is None else past[5]
    bt, lt = _tiles(B, L, 512)
    x2, conv_state = _ffn(x1, sc2, sh2, gt2, g_ffn, g_final, prev, w_up, conv_w, conv_b,
                          w_down, bt=bt, lt=lt, fc=512, final=last)
    conv_state = jnp.swapaxes(conv_state, 1, 2).reshape(B, CONV_W - 1, -1)
    heads = lambda a, h: a.reshape(B, L, h, D_HEAD)
    state = (heads(k_sb, H_SB), heads(v_sb, H_SB), heads(k_sa, H_SA), heads(v_sa, H_SA),
             k_ix, conv_state)
    return x2, state


def kernel(x_prompt, x_sample, cache_sb_k, cache_sb_v, cache_sa_k, cache_sa_v, cache_idx_k,
           state_ffn_conv, c_prompt, c_sample, w_ada, b_ada, g_mix, w_in, w_gate, w_br_sb,
           w_br_sa, w_out, rel_table, g_ffn, w_up, conv_w, conv_b, w_down, g_final):
    depth = w_ada.shape[0]
    nbp = c_prompt.shape[0]
    nbs = c_sample.shape[0]
    rows = -(-(nbp + nbs) // 8) * 8
    c_all = jnp.concatenate([c_prompt, c_sample,
                             jnp.zeros((rows - nbp - nbs, c_prompt.shape[1]), F32)], axis=0)
    xp, xs = x_prompt, x_sample
    new_p, new_s = [], []
    for l in range(depth):
        mod = _adaln(c_all, w_ada[l], b_ada[l][None, :])
        wts = (g_mix[l][None, :], _pad_cols(w_in[l], 512).astype(BF16), w_gate[l].astype(BF16),
               w_br_sb[l].astype(BF16), w_br_sa[l].astype(BF16), w_out[l].astype(BF16),
               rel_table, g_ffn[l][None, :], w_up[l].astype(BF16), conv_w[l],
               conv_b[l][None, :], w_down[l].astype(BF16))
        gfin = g_final[None, :]
        past = (cache_sb_k[l], cache_sb_v[l], cache_sa_k[l], cache_sa_v[l], cache_idx_k[l],
                state_ffn_conv[l])
        xp, sp = _layer(xp, mod[:nbp], None, wts, last=l == depth - 1, g_final=gfin)
        xs, ss = _layer(xs, mod[nbp:nbp + nbs], past, wts, last=l == depth - 1, g_final=gfin)
        new_p.append(sp)
        new_s.append(ss)
    stack = lambda states, n: jnp.stack([s[n] for s in states])
    return ((xp, xs) + tuple(stack(new_p, n) for n in range(6))
            + tuple(stack(new_s, n) for n in range(6)))
```

```python
import functools

import numpy as np
import jax
import jax.numpy as jnp
from jax import lax
from jax.experimental import pallas as pl
from jax.experimental.pallas import tpu as pltpu

F32 = jnp.float32
BF16 = jnp.bfloat16
I32 = jnp.int32

CHUNK = 64
D_HEAD = 128
H_SB = 8
H_SA = 8
W_SB = H_SB * D_HEAD
W_SA = H_SA * D_HEAD
H_IDX = 16
D_IDX = 64
TOPK_MAX = 256
N_BUCKETS = 32
REL_MAX_DIST = 1024
CONV_W = 3
EPS = 1e-6

COL_Q_SB = 0
COL_K_SB = W_SB
COL_V_SB = 2 * W_SB
COL_Q_SA = 3 * W_SB
COL_K_SA = 3 * W_SB + W_SA
COL_V_SA = 3 * W_SB + 2 * W_SA
COL_Q_IX = 3 * W_SB + 3 * W_SA
COL_K_IX = COL_Q_IX + H_IDX * D_IDX
COL_W_IX = COL_K_IX + D_IDX
IN_COLS = COL_W_IX + H_IDX

LANE = 128
VMEM_LIMIT = 56 * 1024 * 1024

ATT_SCALE = D_HEAD ** -0.5
IDX_SCALE = (D_IDX ** -0.5) * (H_IDX ** -0.5)
NEG = -1e30
INT_MIN = -2 ** 31
SB_DEAD = -104.0


def _cparams(sem):
    return pltpu.CompilerParams(dimension_semantics=sem, vmem_limit_bytes=VMEM_LIMIT)


def _adaln_kernel(c_ref, w_ref, b_ref, o_ref):
    c = c_ref[...]
    a = c * jax.nn.sigmoid(c)
    o_ref[...] = jnp.dot(a.astype(BF16), w_ref[...].astype(BF16),
                         preferred_element_type=F32) + b_ref[...]


def _adaln(c, w, b):
    R, D = c.shape
    N = w.shape[1]
    tn = 1024
    return pl.pallas_call(
        _adaln_kernel,
        grid=(N // tn,),
        in_specs=[pl.BlockSpec((R, D), lambda n: (0, 0)),
                  pl.BlockSpec((D, tn), lambda n: (0, n)),
                  pl.BlockSpec((1, tn), lambda n: (0, n))],
        out_specs=pl.BlockSpec((R, tn), lambda n: (0, n)),
        out_shape=jax.ShapeDtypeStruct((R, N), F32),
        compiler_params=_cparams(("arbitrary",)),
        name="adaln",
    )(c, w, b)


def _norm_mod(x, g, sc, sh):
    ms = jnp.mean(x * x, axis=-1, keepdims=True)
    y = x * lax.rsqrt(ms + EPS) * g
    return y * (1.0 + sc) + sh


def _proj_kernel(x_ref, sc_ref, sh_ref, g_ref, w_ref, o_ref, h_ref, *, sigmoid):
    bt, lt, D = x_ref.shape

    @pl.when(pl.program_id(2) == 0)
    def _():
        h = _norm_mod(x_ref[...], g_ref[...], sc_ref[...], sh_ref[...])
        h_ref[...] = h.reshape(bt * lt, D).astype(BF16)

    r = jnp.dot(h_ref[...], w_ref[...], preferred_element_type=F32)
    if sigmoid:
        r = jax.nn.sigmoid(r)
    o_ref[...] = r.reshape(bt, lt, r.shape[-1])


def _proj(x, sc, sh, g, w, *, bt, lt, tn, sigmoid):
    B, L, D = x.shape
    N = w.shape[1]
    return pl.pallas_call(
        functools.partial(_proj_kernel, sigmoid=sigmoid),
        grid=(B // bt, L // lt, N // tn),
        in_specs=[pl.BlockSpec((bt, lt, D), lambda b, t, n: (b, t, 0)),
                  pl.BlockSpec((bt, 1, D), lambda b, t, n: (b, 0, 0)),
                  pl.BlockSpec((bt, 1, D), lambda b, t, n: (b, 0, 0)),
                  pl.BlockSpec((1, D), lambda b, t, n: (0, 0)),
                  pl.BlockSpec((D, tn), lambda b, t, n: (0, n))],
        out_specs=pl.BlockSpec((bt, lt, tn), lambda b, t, n: (b, t, n)),
        out_shape=jax.ShapeDtypeStruct((B, L, N), F32),
        scratch_shapes=[pltpu.VMEM((bt * lt, D), BF16)],
        compiler_params=_cparams(("arbitrary", "arbitrary", "arbitrary")),
        name="proj_gate" if sigmoid else "proj_in",
    )(x, sc, sh, g, w)


def _sb_kernel(q_ref, k_ref, v_ref, u_ref, o_ref, *, tq, tk, past_len):
    i = pl.program_id(2)
    q = q_ref[...].astype(BF16)
    qpos0 = past_len + i * tq
    qpos = qpos0 + lax.broadcasted_iota(I32, (tq, 1), 0)
    j0 = (qpos0 + tq - 2) // tk
    u = u_ref[...]

    def cond(carry):
        j, go, _, _ = carry
        return jnp.logical_and(j >= 0, go)

    def body(carry):
        j, _, cum, acc = carry
        off = pl.multiple_of(j * tk, tk)
        k = k_ref[pl.ds(off, tk), :]
        v = v_ref[pl.ds(off, tk), :]
        z = lax.dot_general(q, k, (((1,), (1,)), ((), ())),
                            preferred_element_type=F32) * ATT_SCALE
        kpos = j * tk + lax.broadcasted_iota(I32, (1, tk), 1)
        mask = kpos < qpos
        t = jnp.log1p(jnp.exp(-jnp.abs(z)))
        lk = jnp.where(mask, -(jnp.maximum(z, 0.0) + t), 0.0)
        hi = lk.astype(BF16)
        lo = (lk - hi.astype(F32)).astype(BF16)
        after = cum + (jnp.dot(hi, u, preferred_element_type=F32)
                       + jnp.dot(lo, u, preferred_element_type=F32))
        lsz = jnp.minimum(z, 0.0) - t
        a = jnp.where(mask, jnp.exp(lsz + after), 0.0)
        acc = acc + jnp.dot(a.astype(BF16), v, preferred_element_type=F32)
        cum = cum + jnp.sum(lk, axis=1, keepdims=True)
        go = jnp.max(cum) > SB_DEAD
        return j - 1, go, cum, acc

    init = (j0, jnp.bool_(True), jnp.zeros((tq, 1), F32), jnp.zeros((tq, D_HEAD), F32))
    _, _, _, acc = lax.while_loop(cond, body, init)
    o_ref[...] = acc.astype(o_ref.dtype)


def _sb_attention(proj, k_all, v_all, *, tq, tk, past_len):
    B, L, _ = proj.shape
    Lkp = k_all.shape[1]
    r = np.arange(tk)
    u = jnp.asarray((r[:, None] > r[None, :]).astype(np.float32), BF16)
    return pl.pallas_call(
        functools.partial(_sb_kernel, tq=tq, tk=tk, past_len=past_len),
        grid=(B, H_SB, L // tq),
        in_specs=[pl.BlockSpec((None, tq, D_HEAD), lambda b, h, i: (b, i, h)),
                  pl.BlockSpec((None, Lkp, D_HEAD), lambda b, h, i: (b, 0, h)),
                  pl.BlockSpec((None, Lkp, D_HEAD), lambda b, h, i: (b, 0, h)),
                  pl.BlockSpec((tk, tk), lambda b, h, i: (0, 0))],
        out_specs=pl.BlockSpec((None, tq, D_HEAD), lambda b, h, i: (b, i, h)),
        out_shape=jax.ShapeDtypeStruct((B, L, W_SB), BF16),
        compiler_params=_cparams(("arbitrary", "arbitrary", "arbitrary")),
        name="sb_attn",
    )(proj, k_all, v_all, u)


def _bucket_edges():
    nb = N_BUCKETS // 2
    max_exact = nb // 2

    def bucket(rel):
        n = abs(rel)
        if n < max_exact:
            v = n
        else:
            v = max_exact + int(np.log(np.float32(n) / max_exact)
                                / np.log(REL_MAX_DIST / max_exact) * (nb - max_exact))
            v = min(v, nb - 1)
        return (nb if rel > 0 else 0) + v

    lo = -4 * REL_MAX_DIST
    assert bucket(lo) == nb - 1
    edges = []
    prev = bucket(lo)
    for rel in range(lo + 1, CHUNK):
        bk = bucket(rel)
        if bk != prev:
            edges.append((rel, prev))
            prev = bk
    edges.append((CHUNK, prev))
    return edges


_EDGES = _bucket_edges()
_FAR_BUCKET = _EDGES[0][1]
_NEAR_REL = _EDGES[0][0]


def _bias_kernel(d_ref, rel_ref, o_ref):
    tk, tq = o_ref.shape
    h = pl.program_id(1)
    rel = (lax.broadcasted_iota(I32, (tk, tq), 0) - lax.broadcasted_iota(I32, (tk, tq), 1)
           - d_ref[pl.program_id(0)])
    tile = jnp.full((tk, tq), rel_ref[_EDGES[-1][1], h], F32)
    for edge, bk in reversed(_EDGES[:-1]):
        tile = jnp.where(rel < edge, rel_ref[bk, h], tile)
    o_ref[...] = tile


def _bias_tiles(ds, rel_table, *, tq, tk):
    return pl.pallas_call(
        _bias_kernel,
        grid=(len(ds), H_SA),
        in_specs=[pl.BlockSpec(memory_space=pltpu.SMEM), pl.BlockSpec(memory_space=pltpu.SMEM)],
        out_specs=pl.BlockSpec((None, None, tk, tq), lambda n, h: (n, h, 0, 0)),
        out_shape=jax.ShapeDtypeStruct((len(ds), H_SA, tk, tq), F32),
        compiler_params=_cparams(("arbitrary", "arbitrary")),
        name="bias_tiles",
    )(jnp.asarray(np.asarray(ds, np.int32)), rel_table)


_NT = (((1,), (1,)), ((), ()))


def _dsa_kernel(itab, jtab, ftab, ltab, ntab, nbtab,
                qit_ref, wt_ref, qt_ref, ki_ref, k_ref, vt_ref, bias_ref, o_ref,
                keys_ref, thr_ref, m_ref, l_ref, acc_ref, madd_ref, s_ref, p_ref,
                *, tq, tk, past_len, sub, csub):
    p = pl.program_id(1)
    i = itab[p]
    j = jtab[p]
    qpos = past_len + i * tq + lax.broadcasted_iota(I32, (1, tq), 1)
    lim = (qpos // CHUNK + 1) * CHUNK

    @pl.when(ftab[p] == 1)
    def _():
        wt = wt_ref[...] * IDX_SCALE

        def score_blk(c, carry):
            off = pl.multiple_of(c * sub, sub)
            kib = ki_ref[pl.ds(off, sub), :]
            s = jnp.zeros((sub, tq), F32)
            for h in range(H_IDX):
                sh = jnp.dot(kib, qit_ref[h * D_IDX:(h + 1) * D_IDX, :],
                             preferred_element_type=F32)
                s = s + wt[h:h + 1, :] * jnp.maximum(sh, 0.0)
            bits = pltpu.bitcast(s, I32)
            key = jnp.where(bits < 0, bits ^ jnp.int32(0x7FFFFFFF), bits)
            kpos = c * sub + lax.broadcasted_iota(I32, (sub, 1), 0)
            keys_ref[pl.ds(off, sub), :] = jnp.where(kpos < lim, key, jnp.int32(INT_MIN))
            return carry

        per = tk // sub

        def score_grp(g, carry):
            for u in range(per):
                score_blk(g * per + u, carry)
            return carry

        lax.fori_loop(0, nbtab[p], score_grp, 0)

        need = jnp.minimum(TOPK_MAX, lim).astype(F32)

        def bit_step(it, tpre):
            bitv = lax.shift_left(jnp.int32(1), 31 - it)
            cand = (tpre | bitv) ^ jnp.int32(INT_MIN)

            def cnt_blk(c, acc):
                kb = keys_ref[pl.ds(pl.multiple_of(c * csub, csub), csub), :]
                hit = jnp.where(kb >= cand, 1.0, 0.0)
                return acc + jnp.sum(hit.reshape(csub // 8, 8, tq), axis=0)

            acc = lax.fori_loop(0, nbtab[p] * (tk // csub), cnt_blk, jnp.zeros((8, tq), F32))
            cnt = jnp.sum(acc, axis=0, keepdims=True)
            return jnp.where(cnt >= need, tpre | bitv, tpre)

        tpre = lax.fori_loop(0, 32, bit_step, jnp.zeros((1, tq), I32))
        thr_ref[...] = tpre ^ jnp.int32(INT_MIN)
        m_ref[...] = jnp.full(m_ref.shape, NEG, F32)
        l_ref[...] = jnp.zeros(l_ref.shape, F32)
        acc_ref[...] = jnp.zeros(acc_ref.shape, F32)

    thr = thr_ref[...]
    for c in range(tk // sub):
        rows = pl.ds(pl.multiple_of(j * tk + c * sub, sub), sub)
        madd_ref[c * sub:(c + 1) * sub, :] = jnp.where(keys_ref[rows, :] >= thr, 0.0, NEG)
    def logits(h):
        cs = slice(h * D_HEAD, (h + 1) * D_HEAD)
        qh = qt_ref[cs, :]
        mx = jnp.full((8, tq), NEG, F32)
        for c in range(tk // sub):
            rs = slice(c * sub, (c + 1) * sub)
            s = jnp.dot(k_ref[rs, cs], qh, preferred_element_type=F32)
            s = s * ATT_SCALE + (bias_ref[h, rs, :] + madd_ref[rs, :])
            s_ref[h % 2, rs, :] = s
            mx = jnp.maximum(mx, jnp.max(s.reshape(sub // 8, 8, tq), axis=0))
        return jnp.max(mx, axis=0, keepdims=True)

    smax = logits(0)
    for h in range(H_SA):
        cs = slice(h * D_HEAD, (h + 1) * D_HEAD)
        m_prev = m_ref[h:h + 1, :]
        m_new = jnp.maximum(m_prev, smax)
        if h + 1 < H_SA:
            smax = logits(h + 1)
        alpha = jnp.exp(m_prev - m_new)
        ps = jnp.zeros((8, tq), F32)
        for c in range(tk // sub):
            rs = slice(c * sub, (c + 1) * sub)
            pr = jnp.exp(s_ref[h % 2, rs, :] - m_new)
            ps = ps + jnp.sum(pr.reshape(sub // 8, 8, tq), axis=0)
            p_ref[h % 2, rs, :] = pr.astype(BF16)
        l_ref[h:h + 1, :] = alpha * l_ref[h:h + 1, :] + jnp.sum(ps, axis=0, keepdims=True)
        acc_ref[cs, :] = alpha * acc_ref[cs, :] + jnp.dot(
            vt_ref[cs, :], p_ref[h % 2], preferred_element_type=F32)
        m_ref[h:h + 1, :] = m_new

    @pl.when(ltab[p] == 1)
    def _():
        for h in range(H_SA):
            cs = slice(h * D_HEAD, (h + 1) * D_HEAD)
            o_ref[cs, :] = (acc_ref[cs, :] / l_ref[h:h + 1, :]).astype(o_ref.dtype)


def _dsa_attention(qit, wt, qt, ki_all, k_all, vt_all, rel_table, *, tq, tk, past_len):
    B, _, L = qt.shape
    Lkp = k_all.shape[1]
    assert Lkp % tk == 0 and L % tq == 0
    nq = L // tq
    it, jt, ft, lt_, nt, nbt = [], [], [], [], [], []
    near_ds = []
    for i in range(nq):
        qpos0 = past_len + i * tq
        lim_max = ((qpos0 + tq - 1) // CHUNK + 1) * CHUNK
        jmax = (lim_max - 1) // tk
        for j in range(jmax + 1):
            d = qpos0 - j * tk
            if (tk - 1) - d >= _NEAR_REL:
                if d not in near_ds:
                    near_ds.append(d)
                nt.append(near_ds.index(d))
            else:
                nt.append(-1)
            it.append(i); jt.append(j); ft.append(int(j == 0)); lt_.append(int(j == jmax))
            nbt.append(jmax + 1)
    n_near = len(near_ds)
    nt = [n_near if v < 0 else v for v in nt]
    bias = _bias_tiles(near_ds + [tk - _NEAR_REL], rel_table, tq=tq, tk=tk)
    tabs = [jnp.asarray(np.asarray(t, np.int32)) for t in (it, jt, ft, lt_, nt, nbt)]
    grid_spec = pltpu.PrefetchScalarGridSpec(
        num_scalar_prefetch=6,
        grid=(B, len(it)),
        in_specs=[
            pl.BlockSpec((None, H_IDX * D_IDX, tq), lambda b, p, it, jt, *_: (b, 0, it[p])),
            pl.BlockSpec((None, H_IDX, tq), lambda b, p, it, jt, *_: (b, 0, it[p])),
            pl.BlockSpec((None, W_SA, tq), lambda b, p, it, jt, *_: (b, 0, it[p])),
            pl.BlockSpec((None, Lkp, D_IDX), lambda b, p, it, jt, *_: (b, 0, 0)),
            pl.BlockSpec((None, tk, W_SA), lambda b, p, it, jt, *_: (b, jt[p], 0)),
            pl.BlockSpec((None, W_SA, tk), lambda b, p, it, jt, *_: (b, 0, jt[p])),
            pl.BlockSpec((None, H_SA, tk, tq),
                         lambda b, p, it, jt, ft, lt, nt, nbt: (nt[p], 0, 0, 0)),
        ],
        out_specs=pl.BlockSpec((None, W_SA, tq), lambda b, p, it, jt, *_: (b, 0, it[p])),
        scratch_shapes=[
            pltpu.VMEM((Lkp, tq), I32),
            pltpu.VMEM((1, tq), I32),
            pltpu.VMEM((H_SA, tq), F32),
            pltpu.VMEM((H_SA, tq), F32),
            pltpu.VMEM((W_SA, tq), F32),
            pltpu.VMEM((tk, tq), F32),
            pltpu.VMEM((2, tk, tq), F32),
            pltpu.VMEM((2, tk, tq), BF16),
        ],
    )
    return pl.pallas_call(
        functools.partial(_dsa_kernel, tq=tq, tk=tk, past_len=past_len,
                          sub=min(tk, 128), csub=min(tk, 256)),
        grid_spec=grid_spec,
        out_shape=jax.ShapeDtypeStruct((B, W_SA, L), BF16),
        compiler_params=_cparams(("arbitrary", "arbitrary")),
        name="dsa_attn",
    )(*tabs, qit, wt, qt, ki_all, k_all, vt_all, bias)


def _merge_kernel(osb_ref, osa_ref, g_ref, x_ref, gt_ref, wsb_ref, wsa_ref, wo_ref, o_ref):
    bt, lt, D = x_ref.shape
    tm = bt * lt
    a = jnp.dot(osb_ref[...].reshape(tm, W_SB), wsb_ref[...], preferred_element_type=F32)
    c = jnp.dot(osa_ref[...].reshape(tm, W_SA), wsa_ref[...], preferred_element_type=F32)
    g = g_ref[...].reshape(tm, 2 * D)
    merged = g[:, :D] * a + g[:, D:] * c
    y = jnp.dot(merged.astype(BF16), wo_ref[...], preferred_element_type=F32)
    o_ref[...] = x_ref[...] + gt_ref[...] * y.reshape(bt, lt, D)


def _merge(osb, osa, g, x, gt, wsb, wsa, wo, *, bt, lt):
    B, L, D = x.shape
    row = lambda w: pl.BlockSpec((bt, lt, w), lambda b, t: (b, t, 0))
    full = lambda a: pl.BlockSpec(a.shape, lambda b, t: (0, 0), pipeline_mode=pl.Buffered(1))
    return pl.pallas_call(
        _merge_kernel,
        grid=(B // bt, L // lt),
        in_specs=[row(W_SB), row(W_SA), row(2 * D), row(D),
                  pl.BlockSpec((bt, 1, D), lambda b, t: (b, 0, 0)),
                  full(wsb), full(wsa), full(wo)],
        out_specs=row(D),
        out_shape=jax.ShapeDtypeStruct((B, L, D), F32),
        compiler_params=_cparams(("arbitrary", "arbitrary")),
        name="merge_out",
    )(osb, osa, g, x, gt, wsb, wsa, wo)


HALO = 8


def _ffn_kernel(x_ref, sc_ref, sh_ref, gt_ref, g_ref, gfin_ref, prev_ref, wg_ref, wv_ref,
                cw_ref, cb_ref, wd_ref, x2_ref, st_ref,
                h_ref, acc_ref, ext_ref, carry_ref, *, final):
    bt, lt, D = x_ref.shape
    tm = bt * lt
    fc = wg_ref.shape[1]
    t = pl.program_id(1)
    f = pl.program_id(2)

    @pl.when(f == 0)
    def _():
        h = _norm_mod(x_ref[...], g_ref[...], sc_ref[...], sh_ref[...])
        h_ref[...] = h.reshape(tm, D).astype(BF16)
        acc_ref[...] = jnp.zeros(acc_ref.shape, F32)

    h = h_ref[...]
    ug = jnp.dot(h, wg_ref[...], preferred_element_type=F32).reshape(bt, lt, fc)
    uv = jnp.dot(h, wv_ref[...], preferred_element_type=F32).reshape(bt, lt, fc)

    @pl.when(t == 0)
    def _():
        ext_ref[:, HALO - 2:HALO, :] = prev_ref[...]

    @pl.when(t > 0)
    def _():
        ext_ref[:, HALO - 2:HALO, :] = carry_ref[f]

    ext_ref[:, HALO:HALO + lt, :] = ug
    tail = ug[:, lt - 2:lt, :]
    carry_ref[f] = tail
    st_ref[:, f] = tail
    cw = cw_ref[...]
    conv = (cb_ref[...] + cw[0:1, :] * ext_ref[:, HALO - 2:HALO - 2 + lt, :]
            + cw[1:2, :] * ext_ref[:, HALO - 1:HALO - 1 + lt, :] + cw[2:3, :] * ug)
    act = conv * jax.nn.sigmoid(conv) * uv
    acc_ref[...] += jnp.dot(act.reshape(tm, fc).astype(BF16), wd_ref[...],
                            preferred_element_type=F32)

    @pl.when(f == pl.num_programs(2) - 1)
    def _():
        x2 = x_ref[...] + gt_ref[...] * acc_ref[...].reshape(bt, lt, D)
        if final:
            ms = jnp.mean(x2 * x2, axis=-1, keepdims=True)
            x2 = x2 * lax.rsqrt(ms + EPS) * gfin_ref[...]
        x2_ref[...] = x2


def _ffn(x, sc, sh, gt, g, gfin, prev, w_up, cw, cb, w_down, *, bt, lt, fc, final):
    B, L, D = x.shape
    F = w_down.shape[0]
    nf = F // fc
    row = pl.BlockSpec((bt, lt, D), lambda b, t, f: (b, t, 0))
    mod = pl.BlockSpec((bt, 1, D), lambda b, t, f: (b, 0, 0))
    vec = pl.BlockSpec((1, D), lambda b, t, f: (0, 0))
    return pl.pallas_call(
        functools.partial(_ffn_kernel, final=final),
        grid=(B // bt, L // lt, nf),
        in_specs=[row, mod, mod, mod, vec, vec,
                  pl.BlockSpec((bt, CONV_W - 1, fc), lambda b, t, f: (b, 0, f)),
                  pl.BlockSpec((D, fc), lambda b, t, f: (0, f)),
                  pl.BlockSpec((D, fc), lambda b, t, f: (0, nf + f)),
                  pl.BlockSpec((CONV_W, fc), lambda b, t, f: (0, f)),
                  pl.BlockSpec((1, fc), lambda b, t, f: (0, f)),
                  pl.BlockSpec((fc, D), lambda b, t, f: (f, 0))],
        out_specs=[row, pl.BlockSpec((bt, nf, CONV_W - 1, fc), lambda b, t, f: (b, 0, 0, 0))],
        out_shape=[jax.ShapeDtypeStruct((B, L, D), F32),
                   jax.ShapeDtypeStruct((B, nf, CONV_W - 1, fc), F32)],
        scratch_shapes=[pltpu.VMEM((bt * lt, D), BF16),
                        pltpu.VMEM((bt * lt, D), F32),
                        pltpu.VMEM((bt, HALO + lt, fc), F32),
                        pltpu.VMEM((nf, bt, CONV_W - 1, fc), F32)],
        compiler_params=_cparams(("arbitrary", "arbitrary", "arbitrary")),
        name="conv_ffn",
    )(x, sc, sh, gt, g, gfin, prev, w_up, w_up, cw, cb, w_down)


def _pad_cols(w, mult):
    n = w.shape[1]
    npad = -(-n // mult) * mult
    return jnp.pad(w, ((0, 0), (0, npad - n)))


def _tiles(B, L, rows):
    if L >= rows:
        assert L % rows == 0
        return 1, rows
    assert rows % L == 0 and B % (rows // L) == 0
    return rows // L, L


def _layer(x, mod, past, wts, *, last, g_final):
    (g_mix, w_in, w_gate, w_br_sb, w_br_sa, w_out, rel_table, g_ffn, w_up, conv_w, conv_b,
     w_down) = wts
    B, L, D = x.shape
    sh1, sc1, gt1, sh2, sc2, gt2 = (m[:, None, :] for m in jnp.split(mod, 6, axis=-1))
    past_len = 0 if past is None else past[0].shape[1]

    bt, lt = _tiles(B, L, 1024)
    proj = _proj(x, sc1, sh1, g_mix, w_in, bt=bt, lt=lt, tn=512, sigmoid=False)
    gate = _proj(x, sc1, sh1, g_mix, w_gate, bt=bt, lt=lt, tn=512, sigmoid=True)

    k_sb = proj[..., COL_K_SB:COL_K_SB + W_SB]
    v_sb = proj[..., COL_V_SB:COL_V_SB + W_SB]
    k_sa = proj[..., COL_K_SA:COL_K_SA + W_SA]
    v_sa = proj[..., COL_V_SA:COL_V_SA + W_SA]
    k_ix = proj[..., COL_K_IX:COL_K_IX + D_IDX]

    tq = min(256, L)
    tk_sb = 256
    tk_sa = 512 if L >= 512 else 256
    l_keys = past_len + L

    def keys_of(new, old, tk):
        lkp = -(-l_keys // tk) * tk
        parts = [] if old is None else [old.reshape(B, past_len, -1).astype(BF16)]
        parts.append(new.astype(BF16))
        if lkp > l_keys:
            parts.append(jnp.zeros((B, lkp - l_keys, new.shape[-1]), BF16))
        return parts[0] if len(parts) == 1 else jnp.concatenate(parts, axis=1)

    old = (None,) * 5 if past is None else past[:5]
    o_sb = _sb_attention(proj, keys_of(k_sb, old[0], tk_sb), keys_of(v_sb, old[1], tk_sb),
                         tq=tq, tk=tk_sb, past_len=past_len)
    feat_major = lambda c, w, dt: jnp.swapaxes(proj[..., c:c + w], 1, 2).astype(dt)
    vt = jnp.swapaxes(keys_of(v_sa, old[3], tk_sa), 1, 2)
    o_sa_t = _dsa_attention(feat_major(COL_Q_IX, H_IDX * D_IDX, BF16), feat_major(COL_W_IX, H_IDX, F32),
                            feat_major(COL_Q_SA, W_SA, BF16), keys_of(k_ix, old[4], tk_sa),
                            keys_of(k_sa, old[2], tk_sa), vt, rel_table,
                            tq=tq, tk=tk_sa, past_len=past_len)
    o_sa = jnp.swapaxes(o_sa_t, 1, 2)

    bt, lt = _tiles(B, L, 256)
    x1 = _merge(o_sb, o_sa, gate, x, gt1, w_br_sb, w_br_sa, w_out, bt=bt, lt=lt)

    prev = jnp.zeros((B, CONV_W - 1, w_down.shape[0]), F32) if past is None else past[5]
    bt, lt = _tiles(B, L, 512)
    x2, conv_state = _ffn(x1, sc2, sh2, gt2, g_ffn, g_final, prev, w_up, conv_w, conv_b,
                          w_down, bt=bt, lt=lt, fc=512, final=last)
    conv_state = jnp.swapaxes(conv_state, 1, 2).reshape(B, CONV_W - 1, -1)
    heads = lambda a, h: a.reshape(B, L, h, D_HEAD)
    state = (heads(k_sb, H_SB), heads(v_sb, H_SB), heads(k_sa, H_SA), heads(v_sa, H_SA),
             k_ix, conv_state)
    return x2, state


def kernel(x_prompt, x_sample, cache_sb_k, cache_sb_v, cache_sa_k, cache_sa_v, cache_idx_k,
           state_ffn_conv, c_prompt, c_sample, w_ada, b_ada, g_mix, w_in, w_gate, w_br_sb,
           w_br_sa, w_out, rel_table, g_ffn, w_up, conv_w, conv_b, w_down, g_final):
    depth = w_ada.shape[0]
    nbp = c_prompt.shape[0]
    nbs = c_sample.shape[0]
    rows = -(-(nbp + nbs) // 8) * 8
    c_all = jnp.concatenate([c_prompt, c_sample,
                             jnp.zeros((rows - nbp - nbs, c_prompt.shape[1]), F32)], axis=0)
    xp, xs = x_prompt, x_sample
    new_p, new_s = [], []
    for l in range(depth):
        mod = _adaln(c_all, w_ada[l], b_ada[l][None, :])
        wts = (g_mix[l][None, :], _pad_cols(w_in[l], 512).astype(BF16), w_gate[l].astype(BF16),
               w_br_sb[l].astype(BF16), w_br_sa[l].astype(BF16), w_out[l].astype(BF16),
               rel_table, g_ffn[l][None, :], w_up[l].astype(BF16), conv_w[l],
               conv_b[l][None, :], w_down[l].astype(BF16))
        gfin = g_final[None, :]
        past = (cache_sb_k[l], cache_sb_v[l], cache_sa_k[l], cache_sa_v[l], cache_idx_k[l],
                state_ffn_conv[l])
        xp, sp = _layer(xp, mod[:nbp], None, wts, last=l == depth - 1, g_final=gfin)
        xs, ss = _layer(xs, mod[nbp:nbp + nbs], past, wts, last=l == depth - 1, g_final=gfin)
        new_p.append(sp)
        new_s.append(ss)
    stack = lambda states, n: jnp.stack([s[n] for s in states])
    return ((xp, xs) + tuple(stack(new_p, n) for n in range(6))
            + tuple(stack(new_s, n) for n in range(6)))
```

```python
import functools

import numpy as np
import jax
import jax.numpy as jnp
from jax import lax
from jax.experimental import pallas as pl
from jax.experimental.pallas import tpu as pltpu

F32 = jnp.float32
BF16 = jnp.bfloat16
I32 = jnp.int32

CHUNK = 64
D_HEAD = 128
H_SB = 8
H_SA = 8
W_SB = H_SB * D_HEAD
W_SA = H_SA * D_HEAD
H_IDX = 16
D_IDX = 64
TOPK_MAX = 256
N_BUCKETS = 32
REL_MAX_DIST = 1024
CONV_W = 3
EPS = 1e-6

COL_Q_SB = 0
COL_K_SB = W_SB
COL_V_SB = 2 * W_SB
COL_Q_SA = 3 * W_SB
COL_K_SA = 3 * W_SB + W_SA
COL_V_SA = 3 * W_SB + 2 * W_SA
COL_Q_IX = 3 * W_SB + 3 * W_SA
COL_K_IX = COL_Q_IX + H_IDX * D_IDX
COL_W_IX = COL_K_IX + D_IDX
IN_COLS = COL_W_IX + H_IDX

LANE = 128
VMEM_LIMIT = 56 * 1024 * 1024

ATT_SCALE = D_HEAD ** -0.5
IDX_SCALE = (D_IDX ** -0.5) * (H_IDX ** -0.5)
NEG = -1e30
INT_MIN = -2 ** 31
SB_DEAD = -104.0


def _cparams(sem):
    return pltpu.CompilerParams(dimension_semantics=sem, vmem_limit_bytes=VMEM_LIMIT)


def _adaln_kernel(c_ref, w_ref, b_ref, o_ref):
    c = c_ref[...]
    a = c * jax.nn.sigmoid(c)
    o_ref[...] = jnp.dot(a.astype(BF16), w_ref[...].astype(BF16),
                         preferred_element_type=F32) + b_ref[...]


def _adaln(c, w, b):
    R, D = c.shape
    N = w.shape[1]
    tn = 1024
    return pl.pallas_call(
        _adaln_kernel,
        grid=(N // tn,),
        in_specs=[pl.BlockSpec((R, D), lambda n: (0, 0)),
                  pl.BlockSpec((D, tn), lambda n: (0, n)),
                  pl.BlockSpec((1, tn), lambda n: (0, n))],
        out_specs=pl.BlockSpec((R, tn), lambda n: (0, n)),
        out_shape=jax.ShapeDtypeStruct((R, N), F32),
        compiler_params=_cparams(("arbitrary",)),
        name="adaln",
    )(c, w, b)


def _norm_mod(x, g, sc, sh):
    ms = jnp.mean(x * x, axis=-1, keepdims=True)
    y = x * lax.rsqrt(ms + EPS) * g
    return y * (1.0 + sc) + sh


class _Out(tuple):
    __slots__ = ()

    def __new__(cls, tile, lo, hi, dtype, feature_major=False, rows=None):
        return tuple.__new__(cls, (tile, lo, hi, rows or (0, hi - lo), feature_major, dtype))


def _proj_kernel(x_ref, sc_ref, sh_ref, g_ref, w_ref, *rest, plan, sigmoid):
    out_refs, h_ref = rest[:-1], rest[-1]
    bt, lt, D = x_ref.shape
    n = pl.program_id(2)

    @pl.when(n == 0)
    def _():
        h = _norm_mod(x_ref[...], g_ref[...], sc_ref[...], sh_ref[...])
        h_ref[...] = h.reshape(bt * lt, D).astype(BF16)

    r = jnp.dot(h_ref[...], w_ref[...], preferred_element_type=F32)
    if sigmoid:
        r = jax.nn.sigmoid(r)
    for (tile, lo, hi, rows, feature_major, _), o_ref in zip(plan, out_refs):
        def emit(o_ref=o_ref, lo=lo, hi=hi, rows=rows, feature_major=feature_major):
            v = r[:, lo:hi]
            if feature_major:
                o_ref[...] = v.T[rows[0]:rows[1], :].astype(o_ref.dtype)
            else:
                o_ref[...] = v.reshape(bt, lt, hi - lo).astype(o_ref.dtype)

        if tile is None:
            emit()
        else:
            pl.when(n == tile)(emit)


def _proj(x, sc, sh, g, w, plan, *, bt, lt, tn, sigmoid=False, name):
    B, L, D = x.shape
    N = w.shape[1]
    out_specs, out_shape = [], []
    for tile, lo, hi, rows, feature_major, dtype in plan:
        if tile is None:
            out_specs.append(pl.BlockSpec((bt, lt, tn), lambda b, t, n: (b, t, n)))
            out_shape.append(jax.ShapeDtypeStruct((B, L, N), dtype))
        elif feature_major:
            assert bt == 1
            width = rows[1] - rows[0]
            out_specs.append(pl.BlockSpec((None, width, lt), lambda b, t, n: (b, 0, t)))
            out_shape.append(jax.ShapeDtypeStruct((B, width, L), dtype))
        else:
            out_specs.append(pl.BlockSpec((bt, lt, hi - lo), lambda b, t, n: (b, t, 0)))
            out_shape.append(jax.ShapeDtypeStruct((B, L, hi - lo), dtype))
    return pl.pallas_call(
        functools.partial(_proj_kernel, plan=tuple(plan), sigmoid=sigmoid),
        grid=(B // bt, L // lt, N // tn),
        in_specs=[pl.BlockSpec((bt, lt, D), lambda b, t, n: (b, t, 0)),
                  pl.BlockSpec((bt, 1, D), lambda b, t, n: (b, 0, 0)),
                  pl.BlockSpec((bt, 1, D), lambda b, t, n: (b, 0, 0)),
                  pl.BlockSpec((1, D), lambda b, t, n: (0, 0)),
                  pl.BlockSpec((D, tn), lambda b, t, n: (0, n))],
        out_specs=out_specs,
        out_shape=out_shape,
        scratch_shapes=[pltpu.VMEM((bt * lt, D), BF16)],
        compiler_params=_cparams(("arbitrary", "arbitrary", "arbitrary")),
        name=name,
    )(x, sc, sh, g, w)


def _sb_kernel(q_ref, k_ref, v_ref, u_ref, o_ref, *, tq, tk, past_len):
    i = pl.program_id(2)
    q = q_ref[...].astype(BF16)
    qpos0 = past_len + i * tq
    qpos = qpos0 + lax.broadcasted_iota(I32, (tq, 1), 0)
    j0 = (qpos0 + tq - 2) // tk
    u = u_ref[...]

    def cond(carry):
        j, go, _, _ = carry
        return jnp.logical_and(j >= 0, go)

    def body(carry):
        j, _, cum, acc = carry
        off = pl.multiple_of(j * tk, tk)
        k = k_ref[pl.ds(off, tk), :]
        v = v_ref[pl.ds(off, tk), :]
        z = lax.dot_general(q, k, (((1,), (1,)), ((), ())),
                            preferred_element_type=F32) * ATT_SCALE
        kpos = j * tk + lax.broadcasted_iota(I32, (1, tk), 1)
        mask = kpos < qpos
        t = jnp.log1p(jnp.exp(-jnp.abs(z)))
        lk = jnp.where(mask, -(jnp.maximum(z, 0.0) + t), 0.0)
        hi = lk.astype(BF16)
        lo = (lk - hi.astype(F32)).astype(BF16)
        after = cum + (jnp.dot(hi, u, preferred_element_type=F32)
                       + jnp.dot(lo, u, preferred_element_type=F32))
        lsz = jnp.minimum(z, 0.0) - t
        a = jnp.where(mask, jnp.exp(lsz + after), 0.0)
        acc = acc + jnp.dot(a.astype(BF16), v, preferred_element_type=F32)
        cum = cum + jnp.sum(lk, axis=1, keepdims=True)
        go = jnp.max(cum) > SB_DEAD
        return j - 1, go, cum, acc

    init = (j0, jnp.bool_(True), jnp.zeros((tq, 1), F32), jnp.zeros((tq, D_HEAD), F32))
    _, _, _, acc = lax.while_loop(cond, body, init)
    o_ref[...] = acc.astype(o_ref.dtype)


def _sb_attention(proj, k_all, v_all, *, tq, tk, past_len):
    B, L, _ = proj.shape
    Lkp = k_all.shape[1]
    r = np.arange(tk)
    u = jnp.asarray((r[:, None] > r[None, :]).astype(np.float32), BF16)
    return pl.pallas_call(
        functools.partial(_sb_kernel, tq=tq, tk=tk, past_len=past_len),
        grid=(B, H_SB, L // tq),
        in_specs=[pl.BlockSpec((None, tq, D_HEAD), lambda b, h, i: (b, i, h)),
                  pl.BlockSpec((None, Lkp, D_HEAD), lambda b, h, i: (b, 0, h)),
                  pl.BlockSpec((None, Lkp, D_HEAD), lambda b, h, i: (b, 0, h)),
                  pl.BlockSpec((tk, tk), lambda b, h, i: (0, 0))],
        out_specs=pl.BlockSpec((None, tq, D_HEAD), lambda b, h, i: (b, i, h)),
        out_shape=jax.ShapeDtypeStruct((B, L, W_SB), BF16),
        compiler_params=_cparams(("arbitrary", "arbitrary", "arbitrary")),
        name="sb_attn",
    )(proj, k_all, v_all, u)


def _bucket_edges():
    nb = N_BUCKETS // 2
    max_exact = nb // 2

    def bucket(rel):
        n = abs(rel)
        if n < max_exact:
            v = n
        else:
            v = max_exact + int(np.log(np.float32(n) / max_exact)
                                / np.log(REL_MAX_DIST / max_exact) * (nb - max_exact))
            v = min(v, nb - 1)
        return (nb if rel > 0 else 0) + v

    lo = -4 * REL_MAX_DIST
    assert bucket(lo) == nb - 1
    edges = []
    prev = bucket(lo)
    for rel in range(lo + 1, CHUNK):
        bk = bucket(rel)
        if bk != prev:
            edges.append((rel, prev))
            prev = bk
    edges.append((CHUNK, prev))
    return edges


_EDGES = _bucket_edges()
_FAR_BUCKET = _EDGES[0][1]
_NEAR_REL = _EDGES[0][0]


def _bias_kernel(d_ref, rel_ref, o_ref):
    tk, tq = o_ref.shape
    h = pl.program_id(1)
    rel = (lax.broadcasted_iota(I32, (tk, tq), 0) - lax.broadcasted_iota(I32, (tk, tq), 1)
           - d_ref[pl.program_id(0)])
    tile = jnp.full((tk, tq), rel_ref[_EDGES[-1][1], h], F32)
    for edge, bk in reversed(_EDGES[:-1]):
        tile = jnp.where(rel < edge, rel_ref[bk, h], tile)
    o_ref[...] = tile


def _bias_tiles(ds, rel_table, *, tq, tk):
    return pl.pallas_call(
        _bias_kernel,
        grid=(len(ds), H_SA),
        in_specs=[pl.BlockSpec(memory_space=pltpu.SMEM), pl.BlockSpec(memory_space=pltpu.SMEM)],
        out_specs=pl.BlockSpec((None, None, tk, tq), lambda n, h: (n, h, 0, 0)),
        out_shape=jax.ShapeDtypeStruct((len(ds), H_SA, tk, tq), F32),
        compiler_params=_cparams(("arbitrary", "arbitrary")),
        name="bias_tiles",
    )(jnp.asarray(np.asarray(ds, np.int32)), rel_table)


_NT = (((1,), (1,)), ((), ()))


def _dsa_kernel(itab, jtab, ftab, ltab, ntab, nbtab,
                qit_ref, wt_ref, qt_ref, ki_ref, k_ref, vt_ref, bias_ref, o_ref,
                keys_ref, thr_ref, m_ref, l_ref, acc_ref, madd_ref, s_ref, p_ref,
                *, tq, tk, past_len, sub, csub):
    p = pl.program_id(1)
    i = itab[p]
    j = jtab[p]
    qpos = past_len + i * tq + lax.broadcasted_iota(I32, (1, tq), 1)
    lim = (qpos // CHUNK + 1) * CHUNK

    @pl.when(ftab[p] == 1)
    def _():
        wt = wt_ref[...] * IDX_SCALE

        def score_blk(c, carry):
            off = pl.multiple_of(c * sub, sub)
            kib = ki_ref[pl.ds(off, sub), :]
            s = jnp.zeros((sub, tq), F32)
            for h in range(H_IDX):
                sh = jnp.dot(kib, qit_ref[h * D_IDX:(h + 1) * D_IDX, :],
                             preferred_element_type=F32)
                s = s + wt[h:h + 1, :] * jnp.maximum(sh, 0.0)
            bits = pltpu.bitcast(s, I32)
            key = jnp.where(bits < 0, bits ^ jnp.int32(0x7FFFFFFF), bits)
            kpos = c * sub + lax.broadcasted_iota(I32, (sub, 1), 0)
            keys_ref[pl.ds(off, sub), :] = jnp.where(kpos < lim, key, jnp.int32(INT_MIN))
            return carry

        per = tk // sub

        def score_grp(g, carry):
            for u in range(per):
                score_blk(g * per + u, carry)
            return carry

        lax.fori_loop(0, nbtab[p], score_grp, 0)

        need = jnp.minimum(TOPK_MAX, lim).astype(F32)

        def bit_step(it, tpre):
            bitv = lax.shift_left(jnp.int32(1), 31 - it)
            cand = (tpre | bitv) ^ jnp.int32(INT_MIN)

            def cnt_blk(c, acc):
                kb = keys_ref[pl.ds(pl.multiple_of(c * csub, csub), csub), :]
                hit = jnp.where(kb >= cand, 1.0, 0.0)
                return acc + jnp.sum(hit.reshape(csub // 8, 8, tq), axis=0)

            acc = lax.fori_loop(0, nbtab[p] * (tk // csub), cnt_blk, jnp.zeros((8, tq), F32))
            cnt = jnp.sum(acc, axis=0, keepdims=True)
            return jnp.where(cnt >= need, tpre | bitv, tpre)

        tpre = lax.fori_loop(0, 32, bit_step, jnp.zeros((1, tq), I32))
        thr_ref[...] = tpre ^ jnp.int32(INT_MIN)
        m_ref[...] = jnp.full(m_ref.shape, NEG, F32)
        l_ref[...] = jnp.zeros(l_ref.shape, F32)
        acc_ref[...] = jnp.zeros(acc_ref.shape, F32)

    thr = thr_ref[...]
    for c in range(tk // sub):
        rows = pl.ds(pl.multiple_of(j * tk + c * sub, sub), sub)
        madd_ref[c * sub:(c + 1) * sub, :] = jnp.where(keys_ref[rows, :] >= thr, 0.0, NEG)
    def logits(h):
        cs = slice(h * D_HEAD, (h + 1) * D_HEAD)
        qh = qt_ref[cs, :]
        mx = jnp.full((8, tq), NEG, F32)
        for c in range(tk // sub):
            rs = slice(c * sub, (c + 1) * sub)
            s = jnp.dot(k_ref[rs, cs], qh, preferred_element_type=F32)
            s = s * ATT_SCALE + (bias_ref[h, rs, :] + madd_ref[rs, :])
            s_ref[h % 2, rs, :] = s
            mx = jnp.maximum(mx, jnp.max(s.reshape(sub // 8, 8, tq), axis=0))
        return jnp.max(mx, axis=0, keepdims=True)

    smax = logits(0)
    for h in range(H_SA):
        cs = slice(h * D_HEAD, (h + 1) * D_HEAD)
        m_prev = m_ref[h:h + 1, :]
        m_new = jnp.maximum(m_prev, smax)
        if h + 1 < H_SA:
            smax = logits(h + 1)
        alpha = jnp.exp(m_prev - m_new)
        ps = jnp.zeros((8, tq), F32)
        for c in range(tk // sub):
            rs = slice(c * sub, (c + 1) * sub)
            pr = jnp.exp(s_ref[h % 2, rs, :] - m_new)
            ps = ps + jnp.sum(pr.reshape(sub // 8, 8, tq), axis=0)
            p_ref[h % 2, rs, :] = pr.astype(BF16)
        l_ref[h:h + 1, :] = alpha * l_ref[h:h + 1, :] + jnp.sum(ps, axis=0, keepdims=True)
        acc_ref[cs, :] = alpha * acc_ref[cs, :] + jnp.dot(
            vt_ref[cs, :], p_ref[h % 2], preferred_element_type=F32)
        m_ref[h:h + 1, :] = m_new

    @pl.when(ltab[p] == 1)
    def _():
        for h in range(H_SA):
            cs = slice(h * D_HEAD, (h + 1) * D_HEAD)
            o_ref[cs, :] = (acc_ref[cs, :] / l_ref[h:h + 1, :]).astype(o_ref.dtype)


def _dsa_attention(qit, wt, qt, ki_all, k_all, vt_all, rel_table, *, tq, tk, past_len):
    B, _, L = qt.shape
    Lkp = k_all.shape[1]
    assert Lkp % tk == 0 and L % tq == 0
    nq = L // tq
    it, jt, ft, lt_, nt, nbt = [], [], [], [], [], []
    near_ds = []
    for i in range(nq):
        qpos0 = past_len + i * tq
        lim_max = ((qpos0 + tq - 1) // CHUNK + 1) * CHUNK
        jmax = (lim_max - 1) // tk
        for j in range(jmax + 1):
            d = qpos0 - j * tk
            if (tk - 1) - d >= _NEAR_REL:
                if d not in near_ds:
                    near_ds.append(d)
                nt.append(near_ds.index(d))
            else:
                nt.append(-1)
            it.append(i); jt.append(j); ft.append(int(j == 0)); lt_.append(int(j == jmax))
            nbt.append(jmax + 1)
    n_near = len(near_ds)
    nt = [n_near if v < 0 else v for v in nt]
    bias = _bias_tiles(near_ds + [tk - _NEAR_REL], rel_table, tq=tq, tk=tk)
    tabs = [jnp.asarray(np.asarray(t, np.int32)) for t in (it, jt, ft, lt_, nt, nbt)]
    grid_spec = pltpu.PrefetchScalarGridSpec(
        num_scalar_prefetch=6,
        grid=(B, len(it)),
        in_specs=[
            pl.BlockSpec((None, H_IDX * D_IDX, tq), lambda b, p, it, jt, *_: (b, 0, it[p])),
            pl.BlockSpec((None, H_IDX, tq), lambda b, p, it, jt, *_: (b, 0, it[p])),
            pl.BlockSpec((None, W_SA, tq), lambda b, p, it, jt, *_: (b, 0, it[p])),
            pl.BlockSpec((None, Lkp, D_IDX), lambda b, p, it, jt, *_: (b, 0, 0)),
            pl.BlockSpec((None, tk, W_SA), lambda b, p, it, jt, *_: (b, jt[p], 0)),
            pl.BlockSpec((None, W_SA, tk), lambda b, p, it, jt, *_: (b, 0, jt[p])),
            pl.BlockSpec((None, H_SA, tk, tq),
                         lambda b, p, it, jt, ft, lt, nt, nbt: (nt[p], 0, 0, 0)),
        ],
        out_specs=pl.BlockSpec((None, W_SA, tq), lambda b, p, it, jt, *_: (b, 0, it[p])),
        scratch_shapes=[
            pltpu.VMEM((Lkp, tq), I32),
            pltpu.VMEM((1, tq), I32),
            pltpu.VMEM((H_SA, tq), F32),
            pltpu.VMEM((H_SA, tq), F32),
            pltpu.VMEM((W_SA, tq), F32),
            pltpu.VMEM((tk, tq), F32),
            pltpu.VMEM((2, tk, tq), F32),
            pltpu.VMEM((2, tk, tq), BF16),
        ],
    )
    return pl.pallas_call(
        functools.partial(_dsa_kernel, tq=tq, tk=tk, past_len=past_len,
                          sub=min(tk, 128), csub=min(tk, 256)),
        grid_spec=grid_spec,
        out_shape=jax.ShapeDtypeStruct((B, W_SA, L), BF16),
        compiler_params=_cparams(("arbitrary", "arbitrary")),
        name="dsa_attn",
    )(*tabs, qit, wt, qt, ki_all, k_all, vt_all, bias)


def _merge_kernel(osb_ref, osa_ref, g_ref, x_ref, gt_ref, wsb_ref, wsa_ref, wo_ref, o_ref):
    bt, lt, D = x_ref.shape
    tm = bt * lt
    a = jnp.dot(osb_ref[...].reshape(tm, W_SB), wsb_ref[...], preferred_element_type=F32)
    c = jnp.dot(osa_ref[...].reshape(tm, W_SA), wsa_ref[...], preferred_element_type=F32)
    g = g_ref[...].reshape(tm, 2 * D)
    merged = g[:, :D] * a + g[:, D:] * c
    y = jnp.dot(merged.astype(BF16), wo_ref[...], preferred_element_type=F32)
    o_ref[...] = x_ref[...] + gt_ref[...] * y.reshape(bt, lt, D)


def _merge(osb, osa, g, x, gt, wsb, wsa, wo, *, bt, lt):
    B, L, D = x.shape
    row = lambda w: pl.BlockSpec((bt, lt, w), lambda b, t: (b, t, 0))
    full = lambda a: pl.BlockSpec(a.shape, lambda b, t: (0, 0), pipeline_mode=pl.Buffered(1))
    return pl.pallas_call(
        _merge_kernel,
        grid=(B // bt, L // lt),
        in_specs=[row(W_SB), row(W_SA), row(2 * D), row(D),
                  pl.BlockSpec((bt, 1, D), lambda b, t: (b, 0, 0)),
                  full(wsb), full(wsa), full(wo)],
        out_specs=row(D),
        out_shape=jax.ShapeDtypeStruct((B, L, D), F32),
        compiler_params=_cparams(("arbitrary", "arbitrary")),
        name="merge_out",
    )(osb, osa, g, x, gt, wsb, wsa, wo)


HALO = 8


def _ffn_kernel(x_ref, sc_ref, sh_ref, gt_ref, g_ref, gfin_ref, prev_ref, wg_ref, wv_ref,
                cw_ref, cb_ref, wd_ref, x2_ref, st_ref,
                h_ref, acc_ref, ext_ref, carry_ref, *, final):
    bt, lt, D = x_ref.shape
    tm = bt * lt
    fc = wg_ref.shape[1]
    t = pl.program_id(1)
    f = pl.program_id(2)

    @pl.when(f == 0)
    def _():
        h = _norm_mod(x_ref[...], g_ref[...], sc_ref[...], sh_ref[...])
        h_ref[...] = h.reshape(tm, D).astype(BF16)
        acc_ref[...] = jnp.zeros(acc_ref.shape, F32)

    h = h_ref[...]
    ug = jnp.dot(h, wg_ref[...], preferred_element_type=F32).reshape(bt, lt, fc)
    uv = jnp.dot(h, wv_ref[...], preferred_element_type=F32).reshape(bt, lt, fc)

    @pl.when(t == 0)
    def _():
        ext_ref[:, HALO - 2:HALO, :] = prev_ref[...]

    @pl.when(t > 0)
    def _():
        ext_ref[:, HALO - 2:HALO, :] = carry_ref[f]

    ext_ref[:, HALO:HALO + lt, :] = ug
    tail = ug[:, lt - 2:lt, :]
    carry_ref[f] = tail
    st_ref[:, f] = tail
    cw = cw_ref[...]
    conv = (cb_ref[...] + cw[0:1, :] * ext_ref[:, HALO - 2:HALO - 2 + lt, :]
            + cw[1:2, :] * ext_ref[:, HALO - 1:HALO - 1 + lt, :] + cw[2:3, :] * ug)
    act = conv * jax.nn.sigmoid(conv) * uv
    acc_ref[...] += jnp.dot(act.reshape(tm, fc).astype(BF16), wd_ref[...],
                            preferred_element_type=F32)

    @pl.when(f == pl.num_programs(2) - 1)
    def _():
        x2 = x_ref[...] + gt_ref[...] * acc_ref[...].reshape(bt, lt, D)
        if final:
            ms = jnp.mean(x2 * x2, axis=-1, keepdims=True)
            x2 = x2 * lax.rsqrt(ms + EPS) * gfin_ref[...]
        x2_ref[...] = x2


def _ffn(x, sc, sh, gt, g, gfin, prev, w_up, cw, cb, w_down, *, bt, lt, fc, final):
    B, L, D = x.shape
    F = w_down.shape[0]
    nf = F // fc
    row = pl.BlockSpec((bt, lt, D), lambda b, t, f: (b, t, 0))
    mod = pl.BlockSpec((bt, 1, D), lambda b, t, f: (b, 0, 0))
    vec = pl.BlockSpec((1, D), lambda b, t, f: (0, 0))
    return pl.pallas_call(
        functools.partial(_ffn_kernel, final=final),
        grid=(B // bt, L // lt, nf),
        in_specs=[row, mod, mod, mod, vec, vec,
                  pl.BlockSpec((bt, CONV_W - 1, fc), lambda b, t, f: (b, 0, f)),
                  pl.BlockSpec((D, fc), lambda b, t, f: (0, f)),
                  pl.BlockSpec((D, fc), lambda b, t, f: (0, nf + f)),
                  pl.BlockSpec((CONV_W, fc), lambda b, t, f: (0, f)),
                  pl.BlockSpec((1, fc), lambda b, t, f: (0, f)),
                  pl.BlockSpec((fc, D), lambda b, t, f: (f, 0))],
        out_specs=[row, pl.BlockSpec((bt, nf, CONV_W - 1, fc), lambda b, t, f: (b, 0, 0, 0))],
        out_shape=[jax.ShapeDtypeStruct((B, L, D), F32),
                   jax.ShapeDtypeStruct((B, nf, CONV_W - 1, fc), F32)],
        scratch_shapes=[pltpu.VMEM((bt * lt, D), BF16),
                        pltpu.VMEM((bt * lt, D), F32),
                        pltpu.VMEM((bt, HALO + lt, fc), F32),
                        pltpu.VMEM((nf, bt, CONV_W - 1, fc), F32)],
        compiler_params=_cparams(("arbitrary", "arbitrary", "arbitrary")),
        name="conv_ffn",
    )(x, sc, sh, gt, g, gfin, prev, w_up, w_up, cw, cb, w_down)


def _pad_cols(w, mult):
    n = w.shape[1]
    npad = -(-n // mult) * mult
    return jnp.pad(w, ((0, 0), (0, npad - n)))


def _tiles(B, L, rows):
    if L >= rows:
        assert L % rows == 0
        return 1, rows
    assert rows % L == 0 and B % (rows // L) == 0
    return rows // L, L


def _layer(x, mod, past, wts, *, last, g_final):
    (g_mix, w_in_sb, w_in_sa, w_in_ix, w_gate, w_br_sb, w_br_sa, w_out, rel_table, g_ffn, w_up,
     conv_w, conv_b, w_down) = wts
    B, L, D = x.shape
    sh1, sc1, gt1, sh2, sc2, gt2 = (m[:, None, :] for m in jnp.split(mod, 6, axis=-1))
    past_len = 0 if past is None else past[0].shape[1]

    wn = W_SB
    bt, lt = _tiles(B, L, 512)
    fm = bt == 1
    call = functools.partial(_proj, x, sc1, sh1, g_mix, bt=bt, lt=lt)
    q_sb, k_sb, kb_sb, v_sb, vb_sb = call(
        w_in_sb, [_Out(0, 0, wn, BF16), _Out(1, 0, wn, F32), _Out(1, 0, wn, BF16),
                  _Out(2, 0, wn, F32), _Out(2, 0, wn, BF16)], tn=wn, name="proj_sb")
    q_sa, k_sa, kb_sa, v_sa, vb_sa = call(
        w_in_sa, [_Out(0, 0, wn, BF16, fm), _Out(1, 0, wn, F32), _Out(1, 0, wn, BF16),
                  _Out(2, 0, wn, F32), _Out(2, 0, wn, BF16, fm)], tn=wn, name="proj_sa")
    w_ix_out = (_Out(1, 0, LANE, F32, True, (D_IDX, D_IDX + H_IDX)) if fm
                else _Out(1, D_IDX, D_IDX + H_IDX, F32))
    q_ix, k_ix, kb_ix, w_ix = call(
        w_in_ix, [_Out(0, 0, wn, BF16, fm), _Out(1, 0, D_IDX, F32), _Out(1, 0, D_IDX, BF16),
                  w_ix_out], tn=wn, name="proj_ix")
    bt, lt = _tiles(B, L, 1024)
    gate, = _proj(x, sc1, sh1, g_mix, w_gate, [_Out(None, 0, 512, F32)], bt=bt, lt=lt, tn=512,
                  sigmoid=True, name="proj_gate")
    if not fm:
        q_sa, q_ix, w_ix = (jnp.swapaxes(a, 1, 2) for a in (q_sa, q_ix, w_ix))

    tq = min(256, L)
    tk_sb = 256
    tk_sa = 512 if L >= 512 else 256
    l_keys = past_len + L

    def keys_of(new, old, tk, feature_major=False):
        lkp = -(-l_keys // tk) * tk
        parts = []
        if old is not None:
            o = old.reshape(B, past_len, -1).astype(BF16)
            parts.append(jnp.swapaxes(o, 1, 2) if feature_major else o)
        parts.append(new)
        if lkp > l_keys:
            nf = new.shape[1] if feature_major else new.shape[2]
            parts.append(jnp.zeros((B, nf, lkp - l_keys) if feature_major
                                   else (B, lkp - l_keys, nf), BF16))
        axis = 2 if feature_major else 1
        return parts[0] if len(parts) == 1 else jnp.concatenate(parts, axis=axis)

    old = (None,) * 5 if past is None else past[:5]
    o_sb = _sb_attention(q_sb, keys_of(kb_sb, old[0], tk_sb), keys_of(vb_sb, old[1], tk_sb),
                         tq=tq, tk=tk_sb, past_len=past_len)
    vt = (keys_of(vb_sa, old[3], tk_sa, True) if fm
          else jnp.swapaxes(keys_of(vb_sa, old[3], tk_sa), 1, 2))
    o_sa_t = _dsa_attention(q_ix, w_ix, q_sa, keys_of(kb_ix, old[4], tk_sa),
                            keys_of(kb_sa, old[2], tk_sa), vt, rel_table,
                            tq=tq, tk=tk_sa, past_len=past_len)
    o_sa = jnp.swapaxes(o_sa_t, 1, 2)

    bt, lt = _tiles(B, L, 256)
    x1 = _merge(o_sb, o_sa, gate, x, gt1, w_br_sb, w_br_sa, w_out, bt=bt, lt=lt)

    prev = jnp.zeros((B, CONV_W - 1, w_down.shape[0]), F32) if past is None else past[5]
    bt, lt = _tiles(B, L, 512)
    x2, conv_state = _ffn(x1, sc2, sh2, gt2, g_ffn, g_final, prev, w_up, conv_w, conv_b,
                          w_down, bt=bt, lt=lt, fc=512, final=last)
    conv_state = jnp.swapaxes(conv_state, 1, 2).reshape(B, CONV_W - 1, -1)
    heads = lambda a, h: a.reshape(B, L, h, D_HEAD)
    state = (heads(k_sb, H_SB), heads(v_sb, H_SB), heads(k_sa, H_SA), heads(v_sa, H_SA),
             k_ix, conv_state)
    return x2, state


def kernel(x_prompt, x_sample, cache_sb_k, cache_sb_v, cache_sa_k, cache_sa_v, cache_idx_k,
           state_ffn_conv, c_prompt, c_sample, w_ada, b_ada, g_mix, w_in, w_gate, w_br_sb,
           w_br_sa, w_out, rel_table, g_ffn, w_up, conv_w, conv_b, w_down, g_final):
    depth = w_ada.shape[0]
    nbp = c_prompt.shape[0]
    nbs = c_sample.shape[0]
    rows = -(-(nbp + nbs) // 8) * 8
    c_all = jnp.concatenate([c_prompt, c_sample,
                             jnp.zeros((rows - nbp - nbs, c_prompt.shape[1]), F32)], axis=0)
    xp, xs = x_prompt, x_sample
    new_p, new_s = [], []
    for l in range(depth):
        mod = _adaln(c_all, w_ada[l], b_ada[l][None, :])
        wb = w_in[l].astype(BF16)
        wts = (g_mix[l][None, :], wb[:, :COL_Q_SA], wb[:, COL_Q_SA:COL_Q_IX],
               _pad_cols(wb[:, COL_Q_IX:], 2 * W_SB), w_gate[l].astype(BF16),
               w_br_sb[l].astype(BF16), w_br_sa[l].astype(BF16), w_out[l].astype(BF16),
               rel_table, g_ffn[l][None, :], w_up[l].astype(BF16), conv_w[l],
               conv_b[l][None, :], w_down[l].astype(BF16))
        gfin = g_final[None, :]
        past = (cache_sb_k[l], cache_sb_v[l], cache_sa_k[l], cache_sa_v[l], cache_idx_k[l],
                state_ffn_conv[l])
        xp, sp = _layer(xp, mod[:nbp], None, wts, last=l == depth - 1, g_final=gfin)
        xs, ss = _layer(xs, mod[nbp:nbp + nbs], past, wts, last=l == depth - 1, g_final=gfin)
        new_p.append(sp)
        new_s.append(ss)
    stack = lambda states, n: jnp.stack([s[n] for s in states])
    return ((xp, xs) + tuple(stack(new_p, n) for n in range(6))
            + tuple(stack(new_s, n) for n in range(6)))
```

```python
import functools

import numpy as np
import jax
import jax.numpy as jnp
from jax import lax
from jax.experimental import pallas as pl
from jax.experimental.pallas import tpu as pltpu

F32 = jnp.float32
BF16 = jnp.bfloat16
I32 = jnp.int32

CHUNK = 64
D_HEAD = 128
H_SB = 8
H_SA = 8
W_SB = H_SB * D_HEAD
W_SA = H_SA * D_HEAD
H_IDX = 16
D_IDX = 64
TOPK_MAX = 256
N_BUCKETS = 32
REL_MAX_DIST = 1024
CONV_W = 3
EPS = 1e-6

COL_Q_SB = 0
COL_K_SB = W_SB
COL_V_SB = 2 * W_SB
COL_Q_SA = 3 * W_SB
COL_K_SA = 3 * W_SB + W_SA
COL_V_SA = 3 * W_SB + 2 * W_SA
COL_Q_IX = 3 * W_SB + 3 * W_SA
COL_K_IX = COL_Q_IX + H_IDX * D_IDX
COL_W_IX = COL_K_IX + D_IDX
IN_COLS = COL_W_IX + H_IDX

LANE = 128
VMEM_LIMIT = 56 * 1024 * 1024

ATT_SCALE = D_HEAD ** -0.5
LOG2E = 1.4426950408889634
IDX_SCALE = (D_IDX ** -0.5) * (H_IDX ** -0.5)
NEG = -1e30
INT_MIN = -2 ** 31
SB_DEAD = -104.0


def _cparams(sem):
    return pltpu.CompilerParams(dimension_semantics=sem, vmem_limit_bytes=VMEM_LIMIT)


def _adaln_kernel(c_ref, w_ref, b_ref, o_ref):
    c = c_ref[...]
    a = c * jax.nn.sigmoid(c)
    o_ref[...] = jnp.dot(a.astype(BF16), w_ref[...].astype(BF16),
                         preferred_element_type=F32) + b_ref[...]


def _adaln(c, w, b):
    R, D = c.shape
    N = w.shape[1]
    tn = 1024
    return pl.pallas_call(
        _adaln_kernel,
        grid=(N // tn,),
        in_specs=[pl.BlockSpec((R, D), lambda n: (0, 0)),
                  pl.BlockSpec((D, tn), lambda n: (0, n)),
                  pl.BlockSpec((1, tn), lambda n: (0, n))],
        out_specs=pl.BlockSpec((R, tn), lambda n: (0, n)),
        out_shape=jax.ShapeDtypeStruct((R, N), F32),
        compiler_params=_cparams(("arbitrary",)),
        name="adaln",
    )(c, w, b)


def _norm_mod(x, g, sc, sh):
    ms = jnp.mean(x * x, axis=-1, keepdims=True)
    y = x * lax.rsqrt(ms + EPS) * g
    return y * (1.0 + sc) + sh


class _Out(tuple):
    __slots__ = ()

    def __new__(cls, tile, lo, hi, dtype, feature_major=False, rows=None):
        return tuple.__new__(cls, (tile, lo, hi, rows or (0, hi - lo), feature_major, dtype))


def _proj_kernel(x_ref, sc_ref, sh_ref, g_ref, w_ref, *rest, plan, sigmoid):
    out_refs, h_ref = rest[:-1], rest[-1]
    bt, lt, D = x_ref.shape
    n = pl.program_id(2)

    @pl.when(n == 0)
    def _():
        h = _norm_mod(x_ref[...], g_ref[...], sc_ref[...], sh_ref[...])
        h_ref[...] = h.reshape(bt * lt, D).astype(BF16)

    r = jnp.dot(h_ref[...], w_ref[...], preferred_element_type=F32)
    if sigmoid:
        r = jax.nn.sigmoid(r)
    for (tile, lo, hi, rows, feature_major, _), o_ref in zip(plan, out_refs):
        def emit(o_ref=o_ref, lo=lo, hi=hi, rows=rows, feature_major=feature_major):
            v = r[:, lo:hi]
            if feature_major:
                o_ref[...] = v.T[rows[0]:rows[1], :].astype(o_ref.dtype)
            else:
                o_ref[...] = v.reshape(bt, lt, hi - lo).astype(o_ref.dtype)

        if tile is None:
            emit()
        else:
            pl.when(n == tile)(emit)


def _proj(x, sc, sh, g, w, plan, *, bt, lt, tn, sigmoid=False, name):
    B, L, D = x.shape
    N = w.shape[1]
    out_specs, out_shape = [], []
    for tile, lo, hi, rows, feature_major, dtype in plan:
        if tile is None:
            out_specs.append(pl.BlockSpec((bt, lt, tn), lambda b, t, n: (b, t, n)))
            out_shape.append(jax.ShapeDtypeStruct((B, L, N), dtype))
        elif feature_major:
            assert bt == 1
            width = rows[1] - rows[0]
            out_specs.append(pl.BlockSpec((None, width, lt), lambda b, t, n: (b, 0, t)))
            out_shape.append(jax.ShapeDtypeStruct((B, width, L), dtype))
        else:
            out_specs.append(pl.BlockSpec((bt, lt, hi - lo), lambda b, t, n: (b, t, 0)))
            out_shape.append(jax.ShapeDtypeStruct((B, L, hi - lo), dtype))
    return pl.pallas_call(
        functools.partial(_proj_kernel, plan=tuple(plan), sigmoid=sigmoid),
        grid=(B // bt, L // lt, N // tn),
        in_specs=[pl.BlockSpec((bt, lt, D), lambda b, t, n: (b, t, 0)),
                  pl.BlockSpec((bt, 1, D), lambda b, t, n: (b, 0, 0)),
                  pl.BlockSpec((bt, 1, D), lambda b, t, n: (b, 0, 0)),
                  pl.BlockSpec((1, D), lambda b, t, n: (0, 0)),
                  pl.BlockSpec((D, tn), lambda b, t, n: (0, n))],
        out_specs=out_specs,
        out_shape=out_shape,
        scratch_shapes=[pltpu.VMEM((bt * lt, D), BF16)],
        compiler_params=_cparams(("arbitrary", "arbitrary", "arbitrary")),
        name=name,
    )(x, sc, sh, g, w)


def _sb_kernel(q_ref, k_ref, v_ref, u_ref, o_ref, *, tq, tk, past_len):
    i = pl.program_id(2)
    q = q_ref[...].astype(BF16)
    qpos0 = past_len + i * tq
    qpos = qpos0 + lax.broadcasted_iota(I32, (tq, 1), 0)
    j0 = (qpos0 + tq - 2) // tk
    u = u_ref[...]

    def cond(carry):
        j, go, _, _ = carry
        return jnp.logical_and(j >= 0, go)

    def body(carry):
        j, _, cum, acc = carry
        off = pl.multiple_of(j * tk, tk)
        k = k_ref[pl.ds(off, tk), :]
        v = v_ref[pl.ds(off, tk), :]
        z = lax.dot_general(q, k, (((1,), (1,)), ((), ())),
                            preferred_element_type=F32) * ATT_SCALE
        kpos = j * tk + lax.broadcasted_iota(I32, (1, tk), 1)
        mask = kpos < qpos
        t = jnp.log1p(jnp.exp(-jnp.abs(z)))
        lk = jnp.where(mask, -(jnp.maximum(z, 0.0) + t), 0.0)
        hi = lk.astype(BF16)
        lo = (lk - hi.astype(F32)).astype(BF16)
        after = cum + (jnp.dot(hi, u, preferred_element_type=F32)
                       + jnp.dot(lo, u, preferred_element_type=F32))
        lsz = jnp.minimum(z, 0.0) - t
        a = jnp.where(mask, jnp.exp(lsz + after), 0.0)
        acc = acc + jnp.dot(a.astype(BF16), v, preferred_element_type=F32)
        cum = cum + jnp.sum(lk, axis=1, keepdims=True)
        go = jnp.max(cum) > SB_DEAD
        return j - 1, go, cum, acc

    init = (j0, jnp.bool_(True), jnp.zeros((tq, 1), F32), jnp.zeros((tq, D_HEAD), F32))
    _, _, _, acc = lax.while_loop(cond, body, init)
    o_ref[...] = acc.astype(o_ref.dtype)


def _sb_attention(proj, k_all, v_all, *, tq, tk, past_len):
    B, L, _ = proj.shape
    Lkp = k_all.shape[1]
    r = np.arange(tk)
    u = jnp.asarray((r[:, None] > r[None, :]).astype(np.float32), BF16)
    return pl.pallas_call(
        functools.partial(_sb_kernel, tq=tq, tk=tk, past_len=past_len),
        grid=(B, H_SB, L // tq),
        in_specs=[pl.BlockSpec((None, tq, D_HEAD), lambda b, h, i: (b, i, h)),
                  pl.BlockSpec((None, Lkp, D_HEAD), lambda b, h, i: (b, 0, h)),
                  pl.BlockSpec((None, Lkp, D_HEAD), lambda b, h, i: (b, 0, h)),
                  pl.BlockSpec((tk, tk), lambda b, h, i: (0, 0))],
        out_specs=pl.BlockSpec((None, tq, D_HEAD), lambda b, h, i: (b, i, h)),
        out_shape=jax.ShapeDtypeStruct((B, L, W_SB), BF16),
        compiler_params=_cparams(("arbitrary", "arbitrary", "arbitrary")),
        name="sb_attn",
    )(proj, k_all, v_all, u)


def _bucket_edges():
    nb = N_BUCKETS // 2
    max_exact = nb // 2

    def bucket(rel):
        n = abs(rel)
        if n < max_exact:
            v = n
        else:
            v = max_exact + int(np.log(np.float32(n) / max_exact)
                                / np.log(REL_MAX_DIST / max_exact) * (nb - max_exact))
            v = min(v, nb - 1)
        return (nb if rel > 0 else 0) + v

    lo = -4 * REL_MAX_DIST
    assert bucket(lo) == nb - 1
    edges = []
    prev = bucket(lo)
    for rel in range(lo + 1, CHUNK):
        bk = bucket(rel)
        if bk != prev:
            edges.append((rel, prev))
            prev = bk
    edges.append((CHUNK, prev))
    return edges


_EDGES = _bucket_edges()
_FAR_BUCKET = _EDGES[0][1]
_NEAR_REL = _EDGES[0][0]


def _bias_kernel(d_ref, rel_ref, o_ref):
    tk, tq = o_ref.shape
    h = pl.program_id(1)
    rel = (lax.broadcasted_iota(I32, (tk, tq), 0) - lax.broadcasted_iota(I32, (tk, tq), 1)
           - d_ref[pl.program_id(0)])
    tile = jnp.full((tk, tq), rel_ref[_EDGES[-1][1], h], F32)
    for edge, bk in reversed(_EDGES[:-1]):
        tile = jnp.where(rel < edge, rel_ref[bk, h], tile)
    o_ref[...] = tile * LOG2E


def _bias_tiles(ds, rel_table, *, tq, tk):
    return pl.pallas_call(
        _bias_kernel,
        grid=(len(ds), H_SA),
        in_specs=[pl.BlockSpec(memory_space=pltpu.SMEM), pl.BlockSpec(memory_space=pltpu.SMEM)],
        out_specs=pl.BlockSpec((None, None, tk, tq), lambda n, h: (n, h, 0, 0)),
        out_shape=jax.ShapeDtypeStruct((len(ds), H_SA, tk, tq), F32),
        compiler_params=_cparams(("arbitrary", "arbitrary")),
        name="bias_tiles",
    )(jnp.asarray(np.asarray(ds, np.int32)), rel_table)


_NT = (((1,), (1,)), ((), ()))


def _dsa_kernel(itab, jtab, ftab, ltab, ntab, nbtab,
                qit_ref, wt_ref, qt_ref, ki_ref, k_ref, vt_ref, bias_ref, o_ref,
                keys_ref, half_ref, thr_ref, m_ref, l_ref, acc_ref, madd_ref, s_ref, p_ref,
                *, tq, tk, past_len, sub, csub):
    p = pl.program_id(1)
    i = itab[p]
    j = jtab[p]
    qpos = past_len + i * tq + lax.broadcasted_iota(I32, (1, tq), 1)
    lim = (qpos // CHUNK + 1) * CHUNK

    @pl.when(ftab[p] == 1)
    def _():
        wt = wt_ref[...] * IDX_SCALE

        def score_blk(c, carry):
            off = pl.multiple_of(c * sub, sub)
            kib = ki_ref[pl.ds(off, sub), :]
            s = jnp.zeros((sub, tq), F32)
            for h in range(H_IDX):
                sh = jnp.dot(kib, qit_ref[h * D_IDX:(h + 1) * D_IDX, :],
                             preferred_element_type=F32)
                s = s + wt[h:h + 1, :] * jnp.maximum(sh, 0.0)
            bits = pltpu.bitcast(s, I32)
            key = jnp.where(bits < 0, bits ^ jnp.int32(0x7FFFFFFF), bits)
            kpos = c * sub + lax.broadcasted_iota(I32, (sub, 1), 0)
            key = jnp.where(kpos < lim, key, jnp.int32(INT_MIN))
            keys_ref[pl.ds(off, sub), :] = key
            half_ref[pl.ds(off, sub), :] = (key >> 16).astype(jnp.int16)
            return carry

        per = tk // sub

        def score_grp(g, carry):
            for u in range(per):
                score_blk(g * per + u, carry)
            return carry

        lax.fori_loop(0, nbtab[p], score_grp, 0)

        need = jnp.minimum(TOPK_MAX, lim)
        nchunk = nbtab[p] * (tk // csub)
        i16_min = -2 ** 15

        def count_ge(cand):
            c16 = jnp.broadcast_to(cand, (16, tq)).astype(jnp.int16)

            def cnt_blk(c, acc):
                kb = half_ref[pl.ds(pl.multiple_of(c * csub, csub), csub), :]
                parts = [jnp.where(kb[16 * r:16 * (r + 1), :] >= c16, jnp.int16(1), jnp.int16(0))
                         for r in range(csub // 16)]
                while len(parts) > 1:
                    parts = [a + b for a, b in zip(parts[::2], parts[1::2])]
                return acc + parts[0]

            acc = lax.fori_loop(0, nchunk, cnt_blk, jnp.zeros((16, tq), jnp.int16))
            return jnp.sum(acc.astype(I32), axis=0, keepdims=True)

        def kth_half(want):
            def bit_step(it, tpre):
                bitv = lax.shift_left(jnp.int32(1), 15 - it)
                hit = count_ge((tpre | bitv) + i16_min) >= want
                return jnp.where(hit, tpre | bitv, tpre)

            return lax.fori_loop(0, 16, bit_step, jnp.zeros((1, tq), I32))

        hi = kth_half(need) + i16_min
        above = jnp.where(hi == 2 ** 15 - 1, 0, count_ge(jnp.minimum(hi + 1, 2 ** 15 - 1)))

        def low_halves(c, carry):
            rows = pl.ds(pl.multiple_of(c * csub, csub), csub)
            k32 = keys_ref[rows, :]
            lo = (k32 & 0xFFFF) + i16_min
            half_ref[rows, :] = jnp.where((k32 >> 16) == hi, lo, i16_min).astype(jnp.int16)
            return carry

        lax.fori_loop(0, nchunk, low_halves, 0)
        thr_ref[...] = lax.shift_left(hi, 16) | kth_half(need - above)
        m_ref[...] = jnp.full(m_ref.shape, NEG, F32)
        l_ref[...] = jnp.zeros(l_ref.shape, F32)
        acc_ref[...] = jnp.zeros(acc_ref.shape, F32)

    thr = thr_ref[...]
    for c in range(tk // sub):
        rows = pl.ds(pl.multiple_of(j * tk + c * sub, sub), sub)
        madd_ref[c * sub:(c + 1) * sub, :] = jnp.where(keys_ref[rows, :] >= thr, 0.0, NEG)
    def logits(h):
        cs = slice(h * D_HEAD, (h + 1) * D_HEAD)
        qh = qt_ref[cs, :]
        mx = jnp.full((8, tq), NEG, F32)
        for c in range(tk // sub):
            rs = slice(c * sub, (c + 1) * sub)
            s = jnp.dot(k_ref[rs, cs], qh, preferred_element_type=F32)
            s = s * (ATT_SCALE * LOG2E) + (bias_ref[h, rs, :] + madd_ref[rs, :])
            s_ref[h % 2, rs, :] = s
            mx = jnp.maximum(mx, jnp.max(s.reshape(sub // 8, 8, tq), axis=0))
        return jnp.max(mx, axis=0, keepdims=True)

    smax = logits(0)
    for h in range(H_SA):
        cs = slice(h * D_HEAD, (h + 1) * D_HEAD)
        m_prev = m_ref[h:h + 1, :]
        m_new = jnp.maximum(m_prev, smax)
        if h + 1 < H_SA:
            smax = logits(h + 1)
        alpha = jnp.exp2(m_prev - m_new)
        ps = jnp.zeros((8, tq), F32)
        for c in range(tk // sub):
            rs = slice(c * sub, (c + 1) * sub)
            pr = jnp.exp2(s_ref[h % 2, rs, :] - m_new)
            ps = ps + jnp.sum(pr.reshape(sub // 8, 8, tq), axis=0)
            p_ref[h % 2, rs, :] = pr.astype(BF16)
        l_ref[h:h + 1, :] = alpha * l_ref[h:h + 1, :] + jnp.sum(ps, axis=0, keepdims=True)
        acc_ref[cs, :] = alpha * acc_ref[cs, :] + jnp.dot(
            vt_ref[cs, :], p_ref[h % 2], preferred_element_type=F32)
        m_ref[h:h + 1, :] = m_new

    @pl.when(ltab[p] == 1)
    def _():
        for h in range(H_SA):
            cs = slice(h * D_HEAD, (h + 1) * D_HEAD)
            o_ref[cs, :] = (acc_ref[cs, :] / l_ref[h:h + 1, :]).astype(o_ref.dtype)


def _dsa_attention(qit, wt, qt, ki_all, k_all, vt_all, rel_table, *, tq, tk, past_len):
    B, _, L = qt.shape
    Lkp = k_all.shape[1]
    assert Lkp % tk == 0 and L % tq == 0
    nq = L // tq
    it, jt, ft, lt_, nt, nbt = [], [], [], [], [], []
    near_ds = []
    for i in range(nq):
        qpos0 = past_len + i * tq
        lim_max = ((qpos0 + tq - 1) // CHUNK + 1) * CHUNK
        jmax = (lim_max - 1) // tk
        for j in range(jmax + 1):
            d = qpos0 - j * tk
            if (tk - 1) - d >= _NEAR_REL:
                if d not in near_ds:
                    near_ds.append(d)
                nt.append(near_ds.index(d))
            else:
                nt.append(-1)
            it.append(i); jt.append(j); ft.append(int(j == 0)); lt_.append(int(j == jmax))
            nbt.append(jmax + 1)
    n_near = len(near_ds)
    nt = [n_near if v < 0 else v for v in nt]
    bias = _bias_tiles(near_ds + [tk - _NEAR_REL], rel_table, tq=tq, tk=tk)
    tabs = [jnp.asarray(np.asarray(t, np.int32)) for t in (it, jt, ft, lt_, nt, nbt)]
    grid_spec = pltpu.PrefetchScalarGridSpec(
        num_scalar_prefetch=6,
        grid=(B, len(it)),
        in_specs=[
            pl.BlockSpec((None, H_IDX * D_IDX, tq), lambda b, p, it, jt, *_: (b, 0, it[p])),
            pl.BlockSpec((None, H_IDX, tq), lambda b, p, it, jt, *_: (b, 0, it[p])),
            pl.BlockSpec((None, W_SA, tq), lambda b, p, it, jt, *_: (b, 0, it[p])),
            pl.BlockSpec((None, Lkp, D_IDX), lambda b, p, it, jt, *_: (b, 0, 0),
                         pipeline_mode=pl.Buffered(1)),
            pl.BlockSpec((None, tk, W_SA), lambda b, p, it, jt, *_: (b, jt[p], 0)),
            pl.BlockSpec((None, W_SA, tk), lambda b, p, it, jt, *_: (b, 0, jt[p])),
            pl.BlockSpec((None, H_SA, tk, tq),
                         lambda b, p, it, jt, ft, lt, nt, nbt: (nt[p], 0, 0, 0)),
        ],
        out_specs=pl.BlockSpec((None, W_SA, tq), lambda b, p, it, jt, *_: (b, 0, it[p])),
        scratch_shapes=[
            pltpu.VMEM((Lkp, tq), I32),
            pltpu.VMEM((Lkp, tq), jnp.int16),
            pltpu.VMEM((1, tq), I32),
            pltpu.VMEM((H_SA, tq), F32),
            pltpu.VMEM((H_SA, tq), F32),
            pltpu.VMEM((W_SA, tq), F32),
            pltpu.VMEM((tk, tq), F32),
            pltpu.VMEM((2, tk, tq), F32),
            pltpu.VMEM((2, tk, tq), BF16),
        ],
    )
    return pl.pallas_call(
        functools.partial(_dsa_kernel, tq=tq, tk=tk, past_len=past_len,
                          sub=min(tk, 128), csub=min(tk, 256)),
        grid_spec=grid_spec,
        out_shape=jax.ShapeDtypeStruct((B, W_SA, L), BF16),
        compiler_params=_cparams(("arbitrary", "arbitrary")),
        name="dsa_attn",
    )(*tabs, qit, wt, qt, ki_all, k_all, vt_all, bias)


def _merge_kernel(osb_ref, osa_ref, g_ref, x_ref, gt_ref, wsb_ref, wsa_ref, wo_ref, o_ref):
    bt, lt, D = x_ref.shape
    tm = bt * lt
    a = jnp.dot(osb_ref[...].reshape(tm, W_SB), wsb_ref[...], preferred_element_type=F32)
    c = jnp.dot(osa_ref[...].reshape(tm, W_SA), wsa_ref[...], preferred_element_type=F32)
    g = g_ref[...].reshape(tm, 2 * D)
    merged = g[:, :D] * a + g[:, D:] * c
    y = jnp.dot(merged.astype(BF16), wo_ref[...], preferred_element_type=F32)
    o_ref[...] = x_ref[...] + gt_ref[...] * y.reshape(bt, lt, D)


def _merge(osb, osa, g, x, gt, wsb, wsa, wo, *, bt, lt):
    B, L, D = x.shape
    row = lambda w: pl.BlockSpec((bt, lt, w), lambda b, t: (b, t, 0))
    full = lambda a: pl.BlockSpec(a.shape, lambda b, t: (0, 0), pipeline_mode=pl.Buffered(1))
    return pl.pallas_call(
        _merge_kernel,
        grid=(B // bt, L // lt),
        in_specs=[row(W_SB), row(W_SA), row(2 * D), row(D),
                  pl.BlockSpec((bt, 1, D), lambda b, t: (b, 0, 0)),
                  full(wsb), full(wsa), full(wo)],
        out_specs=row(D),
        out_shape=jax.ShapeDtypeStruct((B, L, D), F32),
        compiler_params=_cparams(("arbitrary", "arbitrary")),
        name="merge_out",
    )(osb, osa, g, x, gt, wsb, wsa, wo)


HALO = 8


def _ffn_kernel(x_ref, sc_ref, sh_ref, gt_ref, g_ref, gfin_ref, prev_ref, wg_ref, wv_ref,
                cw_ref, cb_ref, wd_ref, x2_ref, st_ref,
                h_ref, acc_ref, ext_ref, carry_ref, *, final):
    bt, lt, D = x_ref.shape
    tm = bt * lt
    fc = wg_ref.shape[1]
    t = pl.program_id(1)
    f = pl.program_id(2)

    @pl.when(f == 0)
    def _():
        h = _norm_mod(x_ref[...], g_ref[...], sc_ref[...], sh_ref[...])
        h_ref[...] = h.reshape(tm, D).astype(BF16)
        acc_ref[...] = jnp.zeros(acc_ref.shape, F32)

    h = h_ref[...]
    ug = jnp.dot(h, wg_ref[...], preferred_element_type=F32).reshape(bt, lt, fc)
    uv = jnp.dot(h, wv_ref[...], preferred_element_type=F32).reshape(bt, lt, fc)

    @pl.when(t == 0)
    def _():
        ext_ref[:, HALO - 2:HALO, :] = prev_ref[...]

    @pl.when(t > 0)
    def _():
        ext_ref[:, HALO - 2:HALO, :] = carry_ref[f]

    ext_ref[:, HALO:HALO + lt, :] = ug
    tail = ug[:, lt - 2:lt, :]
    carry_ref[f] = tail
    st_ref[:, f] = tail
    cw = cw_ref[...]
    conv = (cb_ref[...] + cw[0:1, :] * ext_ref[:, HALO - 2:HALO - 2 + lt, :]
            + cw[1:2, :] * ext_ref[:, HALO - 1:HALO - 1 + lt, :] + cw[2:3, :] * ug)
    act = conv * jax.nn.sigmoid(conv) * uv
    acc_ref[...] += jnp.dot(act.reshape(tm, fc).astype(BF16), wd_ref[...],
                            preferred_element_type=F32)

    @pl.when(f == pl.num_programs(2) - 1)
    def _():
        x2 = x_ref[...] + gt_ref[...] * acc_ref[...].reshape(bt, lt, D)
        if final:
            ms = jnp.mean(x2 * x2, axis=-1, keepdims=True)
            x2 = x2 * lax.rsqrt(ms + EPS) * gfin_ref[...]
        x2_ref[...] = x2


def _ffn(x, sc, sh, gt, g, gfin, prev, w_up, cw, cb, w_down, *, bt, lt, fc, final):
    B, L, D = x.shape
    F = w_down.shape[0]
    nf = F // fc
    row = pl.BlockSpec((bt, lt, D), lambda b, t, f: (b, t, 0))
    mod = pl.BlockSpec((bt, 1, D), lambda b, t, f: (b, 0, 0))
    vec = pl.BlockSpec((1, D), lambda b, t, f: (0, 0))
    return pl.pallas_call(
        functools.partial(_ffn_kernel, final=final),
        grid=(B // bt, L // lt, nf),
        in_specs=[row, mod, mod, mod, vec, vec,
                  pl.BlockSpec((bt, CONV_W - 1, fc), lambda b, t, f: (b, 0, f)),
                  pl.BlockSpec((D, fc), lambda b, t, f: (0, f)),
                  pl.BlockSpec((D, fc), lambda b, t, f: (0, nf + f)),
                  pl.BlockSpec((CONV_W, fc), lambda b, t, f: (0, f)),
                  pl.BlockSpec((1, fc), lambda b, t, f: (0, f)),
                  pl.BlockSpec((fc, D), lambda b, t, f: (f, 0))],
        out_specs=[row, pl.BlockSpec((bt, nf, CONV_W - 1, fc), lambda b, t, f: (b, 0, 0, 0))],
        out_shape=[jax.ShapeDtypeStruct((B, L, D), F32),
                   jax.ShapeDtypeStruct((B, nf, CONV_W - 1, fc), F32)],
        scratch_shapes=[pltpu.VMEM((bt * lt, D), BF16),
                        pltpu.VMEM((bt * lt, D), F32),
                        pltpu.VMEM((bt, HALO + lt, fc), F32),
                        pltpu.VMEM((nf, bt, CONV_W - 1, fc), F32)],
        compiler_params=_cparams(("arbitrary", "arbitrary", "arbitrary")),
        name="conv_ffn",
    )(x, sc, sh, gt, g, gfin, prev, w_up, w_up, cw, cb, w_down)


def _pad_cols(w, mult):
    n = w.shape[1]
    npad = -(-n // mult) * mult
    return jnp.pad(w, ((0, 0), (0, npad - n)))


def _tiles(B, L, rows):
    if L >= rows:
        assert L % rows == 0
        return 1, rows
    assert rows % L == 0 and B % (rows // L) == 0
    return rows // L, L


def _layer(x, mod, past, wts, *, last, g_final):
    (g_mix, w_in_sb, w_in_sa, w_in_ix, w_gate, w_br_sb, w_br_sa, w_out, rel_table, g_ffn, w_up,
     conv_w, conv_b, w_down) = wts
    B, L, D = x.shape
    sh1, sc1, gt1, sh2, sc2, gt2 = (m[:, None, :] for m in jnp.split(mod, 6, axis=-1))
    past_len = 0 if past is None else past[0].shape[1]

    wn = W_SB
    bt, lt = _tiles(B, L, 512)
    fm = bt == 1
    call = functools.partial(_proj, x, sc1, sh1, g_mix, bt=bt, lt=lt)
    q_sb, k_sb, kb_sb, v_sb, vb_sb = call(
        w_in_sb, [_Out(0, 0, wn, BF16), _Out(1, 0, wn, F32), _Out(1, 0, wn, BF16),
                  _Out(2, 0, wn, F32), _Out(2, 0, wn, BF16)], tn=wn, name="proj_sb")
    q_sa, k_sa, kb_sa, v_sa, vb_sa = call(
        w_in_sa, [_Out(0, 0, wn, BF16, fm), _Out(1, 0, wn, F32), _Out(1, 0, wn, BF16),
                  _Out(2, 0, wn, F32), _Out(2, 0, wn, BF16, fm)], tn=wn, name="proj_sa")
    w_ix_out = (_Out(1, 0, LANE, F32, True, (D_IDX, D_IDX + H_IDX)) if fm
                else _Out(1, D_IDX, D_IDX + H_IDX, F32))
    q_ix, k_ix, kb_ix, w_ix = call(
        w_in_ix, [_Out(0, 0, wn, BF16, fm), _Out(1, 0, D_IDX, F32), _Out(1, 0, D_IDX, BF16),
                  w_ix_out], tn=wn, name="proj_ix")
    bt, lt = _tiles(B, L, 1024)
    gate, = _proj(x, sc1, sh1, g_mix, w_gate, [_Out(None, 0, 512, F32)], bt=bt, lt=lt, tn=512,
                  sigmoid=True, name="proj_gate")
    if not fm:
        q_sa, q_ix, w_ix = (jnp.swapaxes(a, 1, 2) for a in (q_sa, q_ix, w_ix))

    tq = min(256, L)
    tk_sb = 256
    tk_sa = 512 if L >= 512 else 256
    l_keys = past_len + L

    def keys_of(new, old, tk, feature_major=False):
        lkp = -(-l_keys // tk) * tk
        parts = []
        if old is not None:
            o = old.reshape(B, past_len, -1).astype(BF16)
            parts.append(jnp.swapaxes(o, 1, 2) if feature_major else o)
        parts.append(new)
        if lkp > l_keys:
            nf = new.shape[1] if feature_major else new.shape[2]
            parts.append(jnp.zeros((B, nf, lkp - l_keys) if feature_major
                                   else (B, lkp - l_keys, nf), BF16))
        axis = 2 if feature_major else 1
        return parts[0] if len(parts) == 1 else jnp.concatenate(parts, axis=axis)

    old = (None,) * 5 if past is None else past[:5]
    o_sb = _sb_attention(q_sb, keys_of(kb_sb, old[0], tk_sb), keys_of(vb_sb, old[1], tk_sb),
                         tq=tq, tk=tk_sb, past_len=past_len)
    vt = (keys_of(vb_sa, old[3], tk_sa, True) if fm
          else jnp.swapaxes(keys_of(vb_sa, old[3], tk_sa), 1, 2))
    o_sa_t = _dsa_attention(q_ix, w_ix, q_sa, keys_of(kb_ix, old[4], tk_sa),
                            keys_of(kb_sa, old[2], tk_sa), vt, rel_table,
                            tq=tq, tk=tk_sa, past_len=past_len)
    o_sa = jnp.swapaxes(o_sa_t, 1, 2)

    bt, lt = _tiles(B, L, 256)
    x1 = _merge(o_sb, o_sa, gate, x, gt1, w_br_sb, w_br_sa, w_out, bt=bt, lt=lt)

    prev = jnp.zeros((B, CONV_W - 1, w_down.shape[0]), F32) if past is None else past[5]
    bt, lt = _tiles(B, L, 512)
    x2, conv_state = _ffn(x1, sc2, sh2, gt2, g_ffn, g_final, prev, w_up, conv_w, conv_b,
                          w_down, bt=bt, lt=lt, fc=512, final=last)
    conv_state = jnp.swapaxes(conv_state, 1, 2).reshape(B, CONV_W - 1, -1)
    heads = lambda a, h: a.reshape(B, L, h, D_HEAD)
    state = (heads(k_sb, H_SB), heads(v_sb, H_SB), heads(k_sa, H_SA), heads(v_sa, H_SA),
             k_ix, conv_state)
    return x2, state


def kernel(x_prompt, x_sample, cache_sb_k, cache_sb_v, cache_sa_k, cache_sa_v, cache_idx_k,
           state_ffn_conv, c_prompt, c_sample, w_ada, b_ada, g_mix, w_in, w_gate, w_br_sb,
           w_br_sa, w_out, rel_table, g_ffn, w_up, conv_w, conv_b, w_down, g_final):
    depth = w_ada.shape[0]
    nbp = c_prompt.shape[0]
    nbs = c_sample.shape[0]
    rows = -(-(nbp + nbs) // 8) * 8
    c_all = jnp.concatenate([c_prompt, c_sample,
                             jnp.zeros((rows - nbp - nbs, c_prompt.shape[1]), F32)], axis=0)
    xp, xs = x_prompt, x_sample
    new_p, new_s = [], []
    for l in range(depth):
        mod = _adaln(c_all, w_ada[l], b_ada[l][None, :])
        wb = w_in[l].astype(BF16)
        wts = (g_mix[l][None, :], wb[:, :COL_Q_SA], wb[:, COL_Q_SA:COL_Q_IX],
               _pad_cols(wb[:, COL_Q_IX:], 2 * W_SB), w_gate[l].astype(BF16),
               w_br_sb[l].astype(BF16), w_br_sa[l].astype(BF16), w_out[l].astype(BF16),
               rel_table, g_ffn[l][None, :], w_up[l].astype(BF16), conv_w[l],
               conv_b[l][None, :], w_down[l].astype(BF16))
        gfin = g_final[None, :]
        past = (cache_sb_k[l], cache_sb_v[l], cache_sa_k[l], cache_sa_v[l], cache_idx_k[l],
                state_ffn_conv[l])
        xp, sp = _layer(xp, mod[:nbp], None, wts, last=l == depth - 1, g_final=gfin)
        xs, ss = _layer(xs, mod[nbp:nbp + nbs], past, wts, last=l == depth - 1, g_final=gfin)
        new_p.append(sp)
        new_s.append(ss)
    stack = lambda states, n: jnp.stack([s[n] for s in states])
    return ((xp, xs) + tuple(stack(new_p, n) for n in range(6))
            + tuple(stack(new_s, n) for n in range(6)))
```

```python
import functools

import numpy as np
import jax
import jax.numpy as jnp
from jax import lax
from jax.experimental import pallas as pl
from jax.experimental.pallas import tpu as pltpu

F32 = jnp.float32
BF16 = jnp.bfloat16
I32 = jnp.int32

CHUNK = 64
D_HEAD = 128
H_SB = 8
H_SA = 8
W_SB = H_SB * D_HEAD
W_SA = H_SA * D_HEAD
H_IDX = 16
D_IDX = 64
TOPK_MAX = 256
N_BUCKETS = 32
REL_MAX_DIST = 1024
CONV_W = 3
EPS = 1e-6

COL_Q_SB = 0
COL_K_SB = W_SB
COL_V_SB = 2 * W_SB
COL_Q_SA = 3 * W_SB
COL_K_SA = 3 * W_SB + W_SA
COL_V_SA = 3 * W_SB + 2 * W_SA
COL_Q_IX = 3 * W_SB + 3 * W_SA
COL_K_IX = COL_Q_IX + H_IDX * D_IDX
COL_W_IX = COL_K_IX + D_IDX
IN_COLS = COL_W_IX + H_IDX

LANE = 128
VMEM_LIMIT = 56 * 1024 * 1024

ATT_SCALE = D_HEAD ** -0.5
LOG2E = 1.4426950408889634
IDX_SCALE = (D_IDX ** -0.5) * (H_IDX ** -0.5)
NEG = -1e30
INT_MIN = -2 ** 31
SB_DEAD = -104.0


def _cparams(sem):
    return pltpu.CompilerParams(dimension_semantics=sem, vmem_limit_bytes=VMEM_LIMIT)


def _adaln_kernel(c_ref, w_ref, b_ref, o_ref):
    c = c_ref[...]
    a = c * jax.nn.sigmoid(c)
    o_ref[...] = jnp.dot(a.astype(BF16), w_ref[...].astype(BF16),
                         preferred_element_type=F32) + b_ref[...]


def _adaln(c, w, b):
    R, D = c.shape
    N = w.shape[1]
    tn = 1024
    return pl.pallas_call(
        _adaln_kernel,
        grid=(N // tn,),
        in_specs=[pl.BlockSpec((R, D), lambda n: (0, 0)),
                  pl.BlockSpec((D, tn), lambda n: (0, n)),
                  pl.BlockSpec((1, tn), lambda n: (0, n))],
        out_specs=pl.BlockSpec((R, tn), lambda n: (0, n)),
        out_shape=jax.ShapeDtypeStruct((R, N), F32),
        compiler_params=_cparams(("arbitrary",)),
        name="adaln",
    )(c, w, b)


def _norm_mod(x, g, sc, sh):
    ms = jnp.mean(x * x, axis=-1, keepdims=True)
    y = x * lax.rsqrt(ms + EPS) * g
    return y * (1.0 + sc) + sh


class _Out(tuple):
    __slots__ = ()

    def __new__(cls, tile, lo, hi, dtype, feature_major=False, rows=None):
        return tuple.__new__(cls, (tile, lo, hi, rows or (0, hi - lo), feature_major, dtype))


def _proj_kernel(x_ref, sc_ref, sh_ref, g_ref, w_ref, *rest, plan, sigmoid):
    out_refs, h_ref = rest[:-1], rest[-1]
    bt, lt, D = x_ref.shape
    n = pl.program_id(2)

    @pl.when(n == 0)
    def _():
        h = _norm_mod(x_ref[...], g_ref[...], sc_ref[...], sh_ref[...])
        h_ref[...] = h.reshape(bt * lt, D).astype(BF16)

    r = jnp.dot(h_ref[...], w_ref[...], preferred_element_type=F32)
    if sigmoid:
        r = jax.nn.sigmoid(r)
    for (tile, lo, hi, rows, feature_major, _), o_ref in zip(plan, out_refs):
        def emit(o_ref=o_ref, lo=lo, hi=hi, rows=rows, feature_major=feature_major):
            v = r[:, lo:hi]
            if feature_major:
                o_ref[...] = v.T[rows[0]:rows[1], :].astype(o_ref.dtype)
            else:
                o_ref[...] = v.reshape(bt, lt, hi - lo).astype(o_ref.dtype)

        if tile is None:
            emit()
        else:
            pl.when(n == tile)(emit)


def _proj(x, sc, sh, g, w, plan, *, bt, lt, tn, sigmoid=False, name):
    B, L, D = x.shape
    N = w.shape[1]
    out_specs, out_shape = [], []
    for tile, lo, hi, rows, feature_major, dtype in plan:
        if tile is None:
            out_specs.append(pl.BlockSpec((bt, lt, tn), lambda b, t, n: (b, t, n)))
            out_shape.append(jax.ShapeDtypeStruct((B, L, N), dtype))
        elif feature_major:
            assert bt == 1
            width = rows[1] - rows[0]
            out_specs.append(pl.BlockSpec((None, width, lt), lambda b, t, n: (b, 0, t)))
            out_shape.append(jax.ShapeDtypeStruct((B, width, L), dtype))
        else:
            out_specs.append(pl.BlockSpec((bt, lt, hi - lo), lambda b, t, n: (b, t, 0)))
            out_shape.append(jax.ShapeDtypeStruct((B, L, hi - lo), dtype))
    return pl.pallas_call(
        functools.partial(_proj_kernel, plan=tuple(plan), sigmoid=sigmoid),
        grid=(B // bt, L // lt, N // tn),
        in_specs=[pl.BlockSpec((bt, lt, D), lambda b, t, n: (b, t, 0)),
                  pl.BlockSpec((bt, 1, D), lambda b, t, n: (b, 0, 0)),
                  pl.BlockSpec((bt, 1, D), lambda b, t, n: (b, 0, 0)),
                  pl.BlockSpec((1, D), lambda b, t, n: (0, 0)),
                  pl.BlockSpec((D, tn), lambda b, t, n: (0, n))],
        out_specs=out_specs,
        out_shape=out_shape,
        scratch_shapes=[pltpu.VMEM((bt * lt, D), BF16)],
        compiler_params=_cparams(("arbitrary", "arbitrary", "arbitrary")),
        name=name,
    )(x, sc, sh, g, w)


def _sb_kernel(q_ref, k_ref, v_ref, u_ref, o_ref, *, tq, tk, past_len, hp):
    i = pl.program_id(2)
    qpos0 = past_len + i * tq
    qpos = qpos0 + lax.broadcasted_iota(I32, (tq, 1), 0)
    j0 = (qpos0 + tq - 2) // tk
    u = u_ref[...]

    def cond(carry):
        j, go, _, _ = carry
        return jnp.logical_and(j >= 0, go)

    def body(carry):
        j, _, cums, accs = carry
        off = pl.multiple_of(j * tk, tk)
        kpos = j * tk + lax.broadcasted_iota(I32, (1, tk), 1)
        mask = kpos < qpos
        new_cums, new_accs = [], []
        for h in range(hp):
            cs = slice(h * D_HEAD, (h + 1) * D_HEAD)
            z = lax.dot_general(q_ref[:, cs], k_ref[pl.ds(off, tk), cs], _NT,
                                preferred_element_type=F32) * ATT_SCALE
            t = jnp.log(1.0 + jnp.exp(-jnp.abs(z)))
            lk = jnp.where(mask, -(jnp.maximum(z, 0.0) + t), 0.0)
            hi = lk.astype(BF16)
            lo = (lk - hi.astype(F32)).astype(BF16)
            after = cums[h] + (jnp.dot(hi, u, preferred_element_type=F32)
                               + jnp.dot(lo, u, preferred_element_type=F32))
            lsz = jnp.minimum(z, 0.0) - t
            a = jnp.where(mask, jnp.exp(lsz + after), 0.0)
            new_accs.append(accs[h] + jnp.dot(a.astype(BF16), v_ref[pl.ds(off, tk), cs],
                                              preferred_element_type=F32))
            new_cums.append(cums[h] + jnp.sum(lk, axis=1, keepdims=True))
        top = new_cums[0]
        for c in new_cums[1:]:
            top = jnp.maximum(top, c)
        return j - 1, jnp.max(top) > SB_DEAD, tuple(new_cums), tuple(new_accs)

    init = (j0, jnp.bool_(True), (jnp.zeros((tq, 1), F32),) * hp,
            (jnp.zeros((tq, D_HEAD), F32),) * hp)
    _, _, _, accs = lax.while_loop(cond, body, init)
    for h in range(hp):
        o_ref[:, h * D_HEAD:(h + 1) * D_HEAD] = accs[h].astype(o_ref.dtype)


def _sb_attention(q, k_all, v_all, *, tq, tk, past_len, hp=2):
    B, L, _ = q.shape
    Lkp = k_all.shape[1]
    r = np.arange(tk)
    u = jnp.asarray((r[:, None] > r[None, :]).astype(np.float32), BF16)
    wp = hp * D_HEAD
    return pl.pallas_call(
        functools.partial(_sb_kernel, tq=tq, tk=tk, past_len=past_len, hp=hp),
        grid=(B, H_SB // hp, L // tq),
        in_specs=[pl.BlockSpec((None, tq, wp), lambda b, h, i: (b, i, h)),
                  pl.BlockSpec((None, Lkp, wp), lambda b, h, i: (b, 0, h)),
                  pl.BlockSpec((None, Lkp, wp), lambda b, h, i: (b, 0, h)),
                  pl.BlockSpec((tk, tk), lambda b, h, i: (0, 0))],
        out_specs=pl.BlockSpec((None, tq, wp), lambda b, h, i: (b, i, h)),
        out_shape=jax.ShapeDtypeStruct((B, L, W_SB), BF16),
        compiler_params=_cparams(("arbitrary", "arbitrary", "arbitrary")),
        name="sb_attn",
    )(q, k_all, v_all, u)


def _bucket_edges():
    nb = N_BUCKETS // 2
    max_exact = nb // 2

    def bucket(rel):
        n = abs(rel)
        if n < max_exact:
            v = n
        else:
            v = max_exact + int(np.log(np.float32(n) / max_exact)
                                / np.log(REL_MAX_DIST / max_exact) * (nb - max_exact))
            v = min(v, nb - 1)
        return (nb if rel > 0 else 0) + v

    lo = -4 * REL_MAX_DIST
    assert bucket(lo) == nb - 1
    edges = []
    prev = bucket(lo)
    for rel in range(lo + 1, CHUNK):
        bk = bucket(rel)
        if bk != prev:
            edges.append((rel, prev))
            prev = bk
    edges.append((CHUNK, prev))
    return edges


_EDGES = _bucket_edges()
_FAR_BUCKET = _EDGES[0][1]
_NEAR_REL = _EDGES[0][0]


def _bias_kernel(d_ref, rel_ref, o_ref):
    tk, tq = o_ref.shape
    h = pl.program_id(1)
    rel = (lax.broadcasted_iota(I32, (tk, tq), 0) - lax.broadcasted_iota(I32, (tk, tq), 1)
           - d_ref[pl.program_id(0)])
    tile = jnp.full((tk, tq), rel_ref[_EDGES[-1][1], h], F32)
    for edge, bk in reversed(_EDGES[:-1]):
        tile = jnp.where(rel < edge, rel_ref[bk, h], tile)
    o_ref[...] = (tile - rel_ref[_FAR_BUCKET, h]) * LOG2E


def _bias_tiles(ds, rel_table, *, tq, tk):
    return pl.pallas_call(
        _bias_kernel,
        grid=(len(ds), H_SA),
        in_specs=[pl.BlockSpec(memory_space=pltpu.SMEM), pl.BlockSpec(memory_space=pltpu.SMEM)],
        out_specs=pl.BlockSpec((None, None, tk, tq), lambda n, h: (n, h, 0, 0)),
        out_shape=jax.ShapeDtypeStruct((len(ds), H_SA, tk, tq), F32),
        compiler_params=_cparams(("arbitrary", "arbitrary")),
        name="bias_tiles",
    )(jnp.asarray(np.asarray(ds, np.int32)), rel_table)


_NT = (((1,), (1,)), ((), ()))


def _dsa_kernel(itab, jtab, ftab, ltab, ntab, nbtab, fartab,
                qit_ref, wt_ref, qt_ref, ki_ref, k_ref, vt_ref, bias_ref, o_ref,
                keys_ref, half_ref, thr_ref, m_ref, l_ref, acc_ref, madd_ref, s_ref, p_ref,
                *, tq, tk, past_len, sub, csub):
    p = pl.program_id(1)
    i = itab[p]
    j = jtab[p]
    qpos = past_len + i * tq + lax.broadcasted_iota(I32, (1, tq), 1)
    lim = (qpos // CHUNK + 1) * CHUNK

    @pl.when(ftab[p] == 1)
    def _():
        wt = wt_ref[...] * IDX_SCALE

        def score_blk(c, carry):
            off = pl.multiple_of(c * sub, sub)
            kib = ki_ref[pl.ds(off, sub), :]
            s = jnp.zeros((sub, tq), F32)
            for h in range(H_IDX):
                sh = jnp.dot(kib, qit_ref[h * D_IDX:(h + 1) * D_IDX, :],
                             preferred_element_type=F32)
                s = s + wt[h:h + 1, :] * jnp.maximum(sh, 0.0)
            bits = pltpu.bitcast(s, I32)
            key = jnp.where(bits < 0, bits ^ jnp.int32(0x7FFFFFFF), bits)
            kpos = c * sub + lax.broadcasted_iota(I32, (sub, 1), 0)
            key = jnp.where(kpos < lim, key, jnp.int32(INT_MIN))
            keys_ref[pl.ds(off, sub), :] = key
            half_ref[pl.ds(off, sub), :] = (key >> 16).astype(jnp.int16)
            return carry

        per = tk // sub

        def score_grp(g, carry):
            for u in range(per):
                score_blk(g * per + u, carry)
            return carry

        lax.fori_loop(0, nbtab[p], score_grp, 0)

        need = jnp.minimum(TOPK_MAX, lim)
        nchunk = nbtab[p] * (tk // csub)
        i16_min = -2 ** 15

        def count_ge(cand):
            c16 = jnp.broadcast_to(cand, (16, tq)).astype(jnp.int16)

            def cnt_blk(c, acc):
                kb = half_ref[pl.ds(pl.multiple_of(c * csub, csub), csub), :]
                parts = [jnp.where(kb[16 * r:16 * (r + 1), :] >= c16, jnp.int16(1), jnp.int16(0))
                         for r in range(csub // 16)]
                while len(parts) > 1:
                    parts = [a + b for a, b in zip(parts[::2], parts[1::2])]
                return acc + parts[0]

            acc = lax.fori_loop(0, nchunk, cnt_blk, jnp.zeros((16, tq), jnp.int16))
            return jnp.sum(acc.astype(I32), axis=0, keepdims=True)

        def kth_half(want):
            def bit_step(it, tpre):
                bitv = lax.shift_left(jnp.int32(1), 15 - it)
                hit = count_ge((tpre | bitv) + i16_min) >= want
                return jnp.where(hit, tpre | bitv, tpre)

            return lax.fori_loop(0, 16, bit_step, jnp.zeros((1, tq), I32))

        hi = kth_half(need) + i16_min
        above = jnp.where(hi == 2 ** 15 - 1, 0, count_ge(jnp.minimum(hi + 1, 2 ** 15 - 1)))

        def low_halves(c, carry):
            rows = pl.ds(pl.multiple_of(c * csub, csub), csub)
            k32 = keys_ref[rows, :]
            lo = (k32 & 0xFFFF) + i16_min
            half_ref[rows, :] = jnp.where((k32 >> 16) == hi, lo, i16_min).astype(jnp.int16)
            return carry

        lax.fori_loop(0, nchunk, low_halves, 0)
        thr_ref[...] = lax.shift_left(hi, 16) | kth_half(need - above)
        m_ref[...] = jnp.full(m_ref.shape, NEG, F32)
        l_ref[...] = jnp.zeros(l_ref.shape, F32)
        acc_ref[...] = jnp.zeros(acc_ref.shape, F32)

    thr = thr_ref[...]
    for c in range(tk // sub):
        rows = pl.ds(pl.multiple_of(j * tk + c * sub, sub), sub)
        madd_ref[c * sub:(c + 1) * sub, :] = jnp.where(keys_ref[rows, :] >= thr, 0.0, NEG)

    def logits(h, near):
        cs = slice(h * D_HEAD, (h + 1) * D_HEAD)
        qh = qt_ref[cs, :]
        mx = jnp.full((8, tq), NEG, F32)
        for c in range(tk // sub):
            rs = slice(c * sub, (c + 1) * sub)
            s = jnp.dot(k_ref[rs, cs], qh, preferred_element_type=F32)
            extra = bias_ref[h, rs, :] + madd_ref[rs, :] if near else madd_ref[rs, :]
            s = s * (ATT_SCALE * LOG2E) + extra
            s_ref[h % 2, rs, :] = s
            mx = jnp.maximum(mx, jnp.max(s.reshape(sub // 8, 8, tq), axis=0))
        return jnp.max(mx, axis=0, keepdims=True)

    def attend(near):
        smax = logits(0, near)
        for h in range(H_SA):
            cs = slice(h * D_HEAD, (h + 1) * D_HEAD)
            m_prev = m_ref[h:h + 1, :]
            m_new = jnp.maximum(m_prev, smax)
            if h + 1 < H_SA:
                smax = logits(h + 1, near)
            alpha = jnp.exp2(m_prev - m_new)
            ps = jnp.zeros((8, tq), F32)
            for c in range(tk // sub):
                rs = slice(c * sub, (c + 1) * sub)
                pr = jnp.exp2(s_ref[h % 2, rs, :] - m_new)
                ps = ps + jnp.sum(pr.reshape(sub // 8, 8, tq), axis=0)
                p_ref[h % 2, rs, :] = pr.astype(BF16)
            l_ref[h:h + 1, :] = alpha * l_ref[h:h + 1, :] + jnp.sum(ps, axis=0, keepdims=True)
            acc_ref[cs, :] = alpha * acc_ref[cs, :] + jnp.dot(
                vt_ref[cs, :], p_ref[h % 2], preferred_element_type=F32)
            m_ref[h:h + 1, :] = m_new

    pl.when(fartab[p] == 0)(functools.partial(attend, True))
    pl.when(fartab[p] == 1)(functools.partial(attend, False))

    @pl.when(ltab[p] == 1)
    def _():
        for h in range(H_SA):
            cs = slice(h * D_HEAD, (h + 1) * D_HEAD)
            o_ref[cs, :] = (acc_ref[cs, :] / l_ref[h:h + 1, :]).astype(o_ref.dtype)


def _dsa_attention(qit, wt, qt, ki_all, k_all, vt_all, rel_table, *, tq, tk, past_len):
    B, _, L = qt.shape
    Lkp = k_all.shape[1]
    assert Lkp % tk == 0 and L % tq == 0
    nq = L // tq
    it, jt, ft, lt_, nt, nbt, fart = [], [], [], [], [], [], []
    near_ds = []
    for i in range(nq):
        qpos0 = past_len + i * tq
        lim_max = ((qpos0 + tq - 1) // CHUNK + 1) * CHUNK
        jmax = (lim_max - 1) // tk
        for j in range(jmax + 1):
            d = qpos0 - j * tk
            if (tk - 1) - d >= _NEAR_REL:
                if d not in near_ds:
                    near_ds.append(d)
                nt.append(near_ds.index(d))
                fart.append(0)
            else:
                nt.append(nt[-1] if nt else 0)
                fart.append(1)
            it.append(i); jt.append(j); ft.append(int(j == 0)); lt_.append(int(j == jmax))
            nbt.append(jmax + 1)
    bias = _bias_tiles(near_ds, rel_table, tq=tq, tk=tk)
    tabs = [jnp.asarray(np.asarray(t, np.int32)) for t in (it, jt, ft, lt_, nt, nbt, fart)]
    grid_spec = pltpu.PrefetchScalarGridSpec(
        num_scalar_prefetch=7,
        grid=(B, len(it)),
        in_specs=[
            pl.BlockSpec((None, H_IDX * D_IDX, tq), lambda b, p, it, jt, *_: (b, 0, it[p])),
            pl.BlockSpec((None, H_IDX, tq), lambda b, p, it, jt, *_: (b, 0, it[p])),
            pl.BlockSpec((None, W_SA, tq), lambda b, p, it, jt, *_: (b, 0, it[p])),
            pl.BlockSpec((None, Lkp, D_IDX), lambda b, p, it, jt, *_: (b, 0, 0),
                         pipeline_mode=pl.Buffered(1)),
            pl.BlockSpec((None, tk, W_SA), lambda b, p, it, jt, *_: (b, jt[p], 0)),
            pl.BlockSpec((None, W_SA, tk), lambda b, p, it, jt, *_: (b, 0, jt[p])),
            pl.BlockSpec((None, H_SA, tk, tq),
                         lambda b, p, it, jt, ft, lt, nt, *_: (nt[p], 0, 0, 0)),
        ],
        out_specs=pl.BlockSpec((None, W_SA, tq), lambda b, p, it, jt, *_: (b, 0, it[p])),
        scratch_shapes=[
            pltpu.VMEM((Lkp, tq), I32),
            pltpu.VMEM((Lkp, tq), jnp.int16),
            pltpu.VMEM((1, tq), I32),
            pltpu.VMEM((H_SA, tq), F32),
            pltpu.VMEM((H_SA, tq), F32),
            pltpu.VMEM((W_SA, tq), F32),
            pltpu.VMEM((tk, tq), F32),
            pltpu.VMEM((2, tk, tq), F32),
            pltpu.VMEM((2, tk, tq), BF16),
        ],
    )
    return pl.pallas_call(
        functools.partial(_dsa_kernel, tq=tq, tk=tk, past_len=past_len,
                          sub=min(tk, 128), csub=min(tk, 256)),
        grid_spec=grid_spec,
        out_shape=jax.ShapeDtypeStruct((B, W_SA, L), BF16),
        compiler_params=_cparams(("arbitrary", "arbitrary")),
        name="dsa_attn",
    )(*tabs, qit, wt, qt, ki_all, k_all, vt_all, bias)


def _merge_kernel(osb_ref, osa_ref, g_ref, x_ref, gt_ref, wsb_ref, wsa_ref, wo_ref, o_ref):
    bt, lt, D = x_ref.shape
    tm = bt * lt
    a = jnp.dot(osb_ref[...].reshape(tm, W_SB), wsb_ref[...], preferred_element_type=F32)
    c = jnp.dot(osa_ref[...].reshape(tm, W_SA), wsa_ref[...], preferred_element_type=F32)
    g = g_ref[...].reshape(tm, 2 * D)
    merged = g[:, :D] * a + g[:, D:] * c
    y = jnp.dot(merged.astype(BF16), wo_ref[...], preferred_element_type=F32)
    o_ref[...] = x_ref[...] + gt_ref[...] * y.reshape(bt, lt, D)


def _merge(osb, osa, g, x, gt, wsb, wsa, wo, *, bt, lt):
    B, L, D = x.shape
    row = lambda w: pl.BlockSpec((bt, lt, w), lambda b, t: (b, t, 0))
    full = lambda a: pl.BlockSpec(a.shape, lambda b, t: (0, 0), pipeline_mode=pl.Buffered(1))
    return pl.pallas_call(
        _merge_kernel,
        grid=(B // bt, L // lt),
        in_specs=[row(W_SB), row(W_SA), row(2 * D), row(D),
                  pl.BlockSpec((bt, 1, D), lambda b, t: (b, 0, 0)),
                  full(wsb), full(wsa), full(wo)],
        out_specs=row(D),
        out_shape=jax.ShapeDtypeStruct((B, L, D), F32),
        compiler_params=_cparams(("arbitrary", "arbitrary")),
        name="merge_out",
    )(osb, osa, g, x, gt, wsb, wsa, wo)


HALO = 8


def _ffn_kernel(x_ref, sc_ref, sh_ref, gt_ref, g_ref, gfin_ref, prev_ref, wg_ref, wv_ref,
                cw_ref, cb_ref, wd_ref, x2_ref, st_ref,
                h_ref, acc_ref, ext_ref, carry_ref, *, final):
    bt, lt, D = x_ref.shape
    tm = bt * lt
    fc = wg_ref.shape[1]
    t = pl.program_id(1)
    f = pl.program_id(2)

    @pl.when(f == 0)
    def _():
        h = _norm_mod(x_ref[...], g_ref[...], sc_ref[...], sh_ref[...])
        h_ref[...] = h.reshape(tm, D).astype(BF16)
        acc_ref[...] = jnp.zeros(acc_ref.shape, F32)

    h = h_ref[...]
    ug = jnp.dot(h, wg_ref[...], preferred_element_type=F32).reshape(bt, lt, fc)
    uv = jnp.dot(h, wv_ref[...], preferred_element_type=F32).reshape(bt, lt, fc)

    @pl.when(t == 0)
    def _():
        ext_ref[:, HALO - 2:HALO, :] = prev_ref[...]

    @pl.when(t > 0)
    def _():
        ext_ref[:, HALO - 2:HALO, :] = carry_ref[f]

    ext_ref[:, HALO:HALO + lt, :] = ug
    tail = ug[:, lt - 2:lt, :]
    carry_ref[f] = tail
    st_ref[:, f] = tail
    cw = cw_ref[...]
    conv = (cb_ref[...] + cw[0:1, :] * ext_ref[:, HALO - 2:HALO - 2 + lt, :]
            + cw[1:2, :] * ext_ref[:, HALO - 1:HALO - 1 + lt, :] + cw[2:3, :] * ug)
    act = conv * jax.nn.sigmoid(conv) * uv
    acc_ref[...] += jnp.dot(act.reshape(tm, fc).astype(BF16), wd_ref[...],
                            preferred_element_type=F32)

    @pl.when(f == pl.num_programs(2) - 1)
    def _():
        x2 = x_ref[...] + gt_ref[...] * acc_ref[...].reshape(bt, lt, D)
        if final:
            ms = jnp.mean(x2 * x2, axis=-1, keepdims=True)
            x2 = x2 * lax.rsqrt(ms + EPS) * gfin_ref[...]
        x2_ref[...] = x2


def _ffn(x, sc, sh, gt, g, gfin, prev, w_up, cw, cb, w_down, *, bt, lt, fc, final):
    B, L, D = x.shape
    F = w_down.shape[0]
    nf = F // fc
    row = pl.BlockSpec((bt, lt, D), lambda b, t, f: (b, t, 0))
    mod = pl.BlockSpec((bt, 1, D), lambda b, t, f: (b, 0, 0))
    vec = pl.BlockSpec((1, D), lambda b, t, f: (0, 0))
    return pl.pallas_call(
        functools.partial(_ffn_kernel, final=final),
        grid=(B // bt, L // lt, nf),
        in_specs=[row, mod, mod, mod, vec, vec,
                  pl.BlockSpec((bt, CONV_W - 1, fc), lambda b, t, f: (b, 0, f)),
                  pl.BlockSpec((D, fc), lambda b, t, f: (0, f)),
                  pl.BlockSpec((D, fc), lambda b, t, f: (0, nf + f)),
                  pl.BlockSpec((CONV_W, fc), lambda b, t, f: (0, f)),
                  pl.BlockSpec((1, fc), lambda b, t, f: (0, f)),
                  pl.BlockSpec((fc, D), lambda b, t, f: (f, 0))],
        out_specs=[row, pl.BlockSpec((bt, nf, CONV_W - 1, fc), lambda b, t, f: (b, 0, 0, 0))],
        out_shape=[jax.ShapeDtypeStruct((B, L, D), F32),
                   jax.ShapeDtypeStruct((B, nf, CONV_W - 1, fc), F32)],
        scratch_shapes=[pltpu.VMEM((bt * lt, D), BF16),
                        pltpu.VMEM((bt * lt, D), F32),
                        pltpu.VMEM((bt, HALO + lt, fc), F32),
                        pltpu.VMEM((nf, bt, CONV_W - 1, fc), F32)],
        compiler_params=_cparams(("arbitrary", "arbitrary", "arbitrary")),
        name="conv_ffn",
    )(x, sc, sh, gt, g, gfin, prev, w_up, w_up, cw, cb, w_down)


def _pad_cols(w, mult):
    n = w.shape[1]
    npad = -(-n // mult) * mult
    return jnp.pad(w, ((0, 0), (0, npad - n)))


def _tiles(B, L, rows):
    if L >= rows:
        assert L % rows == 0
        return 1, rows
    assert rows % L == 0 and B % (rows // L) == 0
    return rows // L, L


def _layer(x, mod, past, wts, *, last, g_final):
    (g_mix, w_in_sb, w_in_sa, w_in_ix, w_gate, w_br_sb, w_br_sa, w_out, rel_table, g_ffn, w_up,
     conv_w, conv_b, w_down) = wts
    B, L, D = x.shape
    sh1, sc1, gt1, sh2, sc2, gt2 = (m[:, None, :] for m in jnp.split(mod, 6, axis=-1))
    past_len = 0 if past is None else past[0].shape[1]

    wn = W_SB
    bt, lt = _tiles(B, L, 512)
    fm = bt == 1
    call = functools.partial(_proj, x, sc1, sh1, g_mix, bt=bt, lt=lt)
    q_sb, k_sb, kb_sb, v_sb, vb_sb = call(
        w_in_sb, [_Out(0, 0, wn, BF16), _Out(1, 0, wn, F32), _Out(1, 0, wn, BF16),
                  _Out(2, 0, wn, F32), _Out(2, 0, wn, BF16)], tn=wn, name="proj_sb")
    q_sa, k_sa, kb_sa, v_sa, vb_sa = call(
        w_in_sa, [_Out(0, 0, wn, BF16, fm), _Out(1, 0, wn, F32), _Out(1, 0, wn, BF16),
                  _Out(2, 0, wn, F32), _Out(2, 0, wn, BF16, fm)], tn=wn, name="proj_sa")
    w_ix_out = (_Out(1, 0, LANE, F32, True, (D_IDX, D_IDX + H_IDX)) if fm
                else _Out(1, D_IDX, D_IDX + H_IDX, F32))
    q_ix, k_ix, kb_ix, w_ix = call(
        w_in_ix, [_Out(0, 0, wn, BF16, fm), _Out(1, 0, D_IDX, F32), _Out(1, 0, D_IDX, BF16),
                  w_ix_out], tn=wn, name="proj_ix")
    bt, lt = _tiles(B, L, 1024)
    gate, = _proj(x, sc1, sh1, g_mix, w_gate, [_Out(None, 0, 512, F32)], bt=bt, lt=lt, tn=512,
                  sigmoid=True, name="proj_gate")
    if not fm:
        q_sa, q_ix, w_ix = (jnp.swapaxes(a, 1, 2) for a in (q_sa, q_ix, w_ix))

    tq = min(256, L)
    tk_sb = 256
    tk_sa = 512 if L >= 512 else 256
    l_keys = past_len + L

    def keys_of(new, old, tk, feature_major=False):
        lkp = -(-l_keys // tk) * tk
        parts = []
        if old is not None:
            o = old.reshape(B, past_len, -1).astype(BF16)
            parts.append(jnp.swapaxes(o, 1, 2) if feature_major else o)
        parts.append(new)
        if lkp > l_keys:
            nf = new.shape[1] if feature_major else new.shape[2]
            parts.append(jnp.zeros((B, nf, lkp - l_keys) if feature_major
                                   else (B, lkp - l_keys, nf), BF16))
        axis = 2 if feature_major else 1
        return parts[0] if len(parts) == 1 else jnp.concatenate(parts, axis=axis)

    old = (None,) * 5 if past is None else past[:5]
    o_sb = _sb_attention(q_sb, keys_of(kb_sb, old[0], tk_sb), keys_of(vb_sb, old[1], tk_sb),
                         tq=tq, tk=tk_sb, past_len=past_len)
    vt = (keys_of(vb_sa, old[3], tk_sa, True) if fm
          else jnp.swapaxes(keys_of(vb_sa, old[3], tk_sa), 1, 2))
    o_sa_t = _dsa_attention(q_ix, w_ix, q_sa, keys_of(kb_ix, old[4], tk_sa),
                            keys_of(kb_sa, old[2], tk_sa), vt, rel_table,
                            tq=tq, tk=tk_sa, past_len=past_len)
    o_sa = jnp.swapaxes(o_sa_t, 1, 2)

    bt, lt = _tiles(B, L, 256)
    x1 = _merge(o_sb, o_sa, gate, x, gt1, w_br_sb, w_br_sa, w_out, bt=bt, lt=lt)

    prev = jnp.zeros((B, CONV_W - 1, w_down.shape[0]), F32) if past is None else past[5]
    bt, lt = _tiles(B, L, 512)
    x2, conv_state = _ffn(x1, sc2, sh2, gt2, g_ffn, g_final, prev, w_up, conv_w, conv_b,
                          w_down, bt=bt, lt=lt, fc=512, final=last)
    conv_state = jnp.swapaxes(conv_state, 1, 2).reshape(B, CONV_W - 1, -1)
    heads = lambda a, h: a.reshape(B, L, h, D_HEAD)
    state = (heads(k_sb, H_SB), heads(v_sb, H_SB), heads(k_sa, H_SA), heads(v_sa, H_SA),
             k_ix, conv_state)
    return x2, state


def kernel(x_prompt, x_sample, cache_sb_k, cache_sb_v, cache_sa_k, cache_sa_v, cache_idx_k,
           state_ffn_conv, c_prompt, c_sample, w_ada, b_ada, g_mix, w_in, w_gate, w_br_sb,
           w_br_sa, w_out, rel_table, g_ffn, w_up, conv_w, conv_b, w_down, g_final):
    depth = w_ada.shape[0]
    nbp = c_prompt.shape[0]
    nbs = c_sample.shape[0]
    rows = -(-(nbp + nbs) // 8) * 8
    c_all = jnp.concatenate([c_prompt, c_sample,
                             jnp.zeros((rows - nbp - nbs, c_prompt.shape[1]), F32)], axis=0)
    xp, xs = x_prompt, x_sample
    new_p, new_s = [], []
    for l in range(depth):
        mod = _adaln(c_all, w_ada[l], b_ada[l][None, :])
        wb = w_in[l].astype(BF16)
        wts = (g_mix[l][None, :], wb[:, :COL_Q_SA], wb[:, COL_Q_SA:COL_Q_IX],
               _pad_cols(wb[:, COL_Q_IX:], 2 * W_SB), w_gate[l].astype(BF16),
               w_br_sb[l].astype(BF16), w_br_sa[l].astype(BF16), w_out[l].astype(BF16),
               rel_table, g_ffn[l][None, :], w_up[l].astype(BF16), conv_w[l],
               conv_b[l][None, :], w_down[l].astype(BF16))
        gfin = g_final[None, :]
        past = (cache_sb_k[l], cache_sb_v[l], cache_sa_k[l], cache_sa_v[l], cache_idx_k[l],
                state_ffn_conv[l])
        xp, sp = _layer(xp, mod[:nbp], None, wts, last=l == depth - 1, g_final=gfin)
        xs, ss = _layer(xs, mod[nbp:nbp + nbs], past, wts, last=l == depth - 1, g_final=gfin)
        new_p.append(sp)
        new_s.append(ss)
    stack = lambda states, n: jnp.stack([s[n] for s in states])
    return ((xp, xs) + tuple(stack(new_p, n) for n in range(6))
            + tuple(stack(new_s, n) for n in range(6)))
```

```python
import functools

import numpy as np
import jax
import jax.numpy as jnp
from jax import lax
from jax.experimental import pallas as pl
from jax.experimental.pallas import tpu as pltpu

F32 = jnp.float32
BF16 = jnp.bfloat16
I32 = jnp.int32

CHUNK = 64
D_HEAD = 128
H_SB = 8
H_SA = 8
W_SB = H_SB * D_HEAD
W_SA = H_SA * D_HEAD
H_IDX = 16
D_IDX = 64
TOPK_MAX = 256
N_BUCKETS = 32
REL_MAX_DIST = 1024
CONV_W = 3
EPS = 1e-6

COL_Q_SB = 0
COL_K_SB = W_SB
COL_V_SB = 2 * W_SB
COL_Q_SA = 3 * W_SB
COL_K_SA = 3 * W_SB + W_SA
COL_V_SA = 3 * W_SB + 2 * W_SA
COL_Q_IX = 3 * W_SB + 3 * W_SA
COL_K_IX = COL_Q_IX + H_IDX * D_IDX
COL_W_IX = COL_K_IX + D_IDX
IN_COLS = COL_W_IX + H_IDX

LANE = 128
VMEM_LIMIT = 56 * 1024 * 1024

ATT_SCALE = D_HEAD ** -0.5
LOG2E = 1.4426950408889634
IDX_SCALE = (D_IDX ** -0.5) * (H_IDX ** -0.5)
NEG = -1e30
INT_MIN = -2 ** 31
SB_DEAD = -104.0


def _cparams(sem):
    return pltpu.CompilerParams(dimension_semantics=sem, vmem_limit_bytes=VMEM_LIMIT)


def _adaln_kernel(c_ref, w_ref, b_ref, o_ref):
    c = c_ref[...]
    a = c * jax.nn.sigmoid(c)
    o_ref[...] = jnp.dot(a.astype(BF16), w_ref[...].astype(BF16),
                         preferred_element_type=F32) + b_ref[...]


def _adaln(c, w, b):
    R, D = c.shape
    N = w.shape[1]
    tn = 1024
    return pl.pallas_call(
        _adaln_kernel,
        grid=(N // tn,),
        in_specs=[pl.BlockSpec((R, D), lambda n: (0, 0)),
                  pl.BlockSpec((D, tn), lambda n: (0, n)),
                  pl.BlockSpec((1, tn), lambda n: (0, n))],
        out_specs=pl.BlockSpec((R, tn), lambda n: (0, n)),
        out_shape=jax.ShapeDtypeStruct((R, N), F32),
        compiler_params=_cparams(("arbitrary",)),
        name="adaln",
    )(c, w, b)


def _norm_mod(x, g, sc, sh):
    ms = jnp.mean(x * x, axis=-1, keepdims=True)
    y = x * lax.rsqrt(ms + EPS) * g
    return y * (1.0 + sc) + sh


class _Out(tuple):
    __slots__ = ()

    def __new__(cls, tile, lo, hi, dtype, layout="row", rows=None):
        return tuple.__new__(cls, (tile, lo, hi, rows or (0, hi - lo), layout, dtype))


def _proj_kernel(x_ref, sc_ref, sh_ref, g_ref, w_ref, *rest, plan, sigmoid):
    out_refs, h_ref = rest[:-1], rest[-1]
    bt, lt, D = x_ref.shape
    n = pl.program_id(2)

    @pl.when(n == 0)
    def _():
        h = _norm_mod(x_ref[...], g_ref[...], sc_ref[...], sh_ref[...])
        h_ref[...] = h.reshape(bt * lt, D).astype(BF16)

    r = jnp.dot(h_ref[...], w_ref[...], preferred_element_type=F32)
    if sigmoid:
        r = jax.nn.sigmoid(r)
    for (tile, lo, hi, rows, layout, _), o_ref in zip(plan, out_refs):
        def emit(o_ref=o_ref, lo=lo, hi=hi, rows=rows, layout=layout):
            v = r[:, lo:hi]
            if layout == "feature":
                o_ref[...] = v.T[rows[0]:rows[1], :].astype(o_ref.dtype)
            elif layout == "heads":
                nh = (hi - lo) // D_HEAD
                v = pltpu.einshape("t(hd)->thd", v, h=nh)
                o_ref[...] = v.reshape(bt, lt, nh, D_HEAD).astype(o_ref.dtype)
            else:
                o_ref[...] = v.reshape(bt, lt, hi - lo).astype(o_ref.dtype)

        if tile is None:
            emit()
        else:
            pl.when(n == tile)(emit)


def _proj(x, sc, sh, g, w, plan, *, bt, lt, tn, sigmoid=False, name):
    B, L, D = x.shape
    N = w.shape[1]
    out_specs, out_shape = [], []
    for tile, lo, hi, rows, layout, dtype in plan:
        if tile is None:
            out_specs.append(pl.BlockSpec((bt, lt, tn), lambda b, t, n: (b, t, n)))
            out_shape.append(jax.ShapeDtypeStruct((B, L, N), dtype))
        elif layout == "heads":
            nh = (hi - lo) // D_HEAD
            out_specs.append(pl.BlockSpec((bt, lt, nh, D_HEAD), lambda b, t, n: (b, t, 0, 0)))
            out_shape.append(jax.ShapeDtypeStruct((B, L, nh, D_HEAD), dtype))
        elif layout == "feature":
            assert bt == 1
            width = rows[1] - rows[0]
            out_specs.append(pl.BlockSpec((None, width, lt), lambda b, t, n: (b, 0, t)))
            out_shape.append(jax.ShapeDtypeStruct((B, width, L), dtype))
        else:
            out_specs.append(pl.BlockSpec((bt, lt, hi - lo), lambda b, t, n: (b, t, 0)))
            out_shape.append(jax.ShapeDtypeStruct((B, L, hi - lo), dtype))
    return pl.pallas_call(
        functools.partial(_proj_kernel, plan=tuple(plan), sigmoid=sigmoid),
        grid=(B // bt, L // lt, N // tn),
        in_specs=[pl.BlockSpec((bt, lt, D), lambda b, t, n: (b, t, 0)),
                  pl.BlockSpec((bt, 1, D), lambda b, t, n: (b, 0, 0)),
                  pl.BlockSpec((bt, 1, D), lambda b, t, n: (b, 0, 0)),
                  pl.BlockSpec((1, D), lambda b, t, n: (0, 0)),
                  pl.BlockSpec((D, tn), lambda b, t, n: (0, n))],
        out_specs=out_specs,
        out_shape=out_shape,
        scratch_shapes=[pltpu.VMEM((bt * lt, D), BF16)],
        compiler_params=_cparams(("arbitrary", "arbitrary", "arbitrary")),
        name=name,
    )(x, sc, sh, g, w)


def _sb_kernel(q_ref, k_ref, v_ref, u_ref, o_ref, *, tq, tk, past_len, hp):
    i = pl.program_id(2)
    qpos0 = past_len + i * tq
    qpos = qpos0 + lax.broadcasted_iota(I32, (tq, 1), 0)
    j0 = (qpos0 + tq - 2) // tk
    u = u_ref[...]

    def cond(carry):
        j, go, _, _ = carry
        return jnp.logical_and(j >= 0, go)

    def body(carry):
        j, _, cums, accs = carry
        off = pl.multiple_of(j * tk, tk)
        kpos = j * tk + lax.broadcasted_iota(I32, (1, tk), 1)
        mask = kpos < qpos
        new_cums, new_accs = [], []
        for h in range(hp):
            cs = slice(h * D_HEAD, (h + 1) * D_HEAD)
            z = lax.dot_general(q_ref[:, cs], k_ref[pl.ds(off, tk), cs], _NT,
                                preferred_element_type=F32) * ATT_SCALE
            t = jnp.log(1.0 + jnp.exp(-jnp.abs(z)))
            lk = jnp.where(mask, -(jnp.maximum(z, 0.0) + t), 0.0)
            hi = lk.astype(BF16)
            lo = (lk - hi.astype(F32)).astype(BF16)
            after = cums[h] + (jnp.dot(hi, u, preferred_element_type=F32)
                               + jnp.dot(lo, u, preferred_element_type=F32))
            lsz = jnp.minimum(z, 0.0) - t
            a = jnp.where(mask, jnp.exp(lsz + after), 0.0)
            new_accs.append(accs[h] + jnp.dot(a.astype(BF16), v_ref[pl.ds(off, tk), cs],
                                              preferred_element_type=F32))
            new_cums.append(cums[h] + jnp.sum(lk, axis=1, keepdims=True))
        top = new_cums[0]
        for c in new_cums[1:]:
            top = jnp.maximum(top, c)
        return j - 1, jnp.max(top) > SB_DEAD, tuple(new_cums), tuple(new_accs)

    init = (j0, jnp.bool_(True), (jnp.zeros((tq, 1), F32),) * hp,
            (jnp.zeros((tq, D_HEAD), F32),) * hp)
    _, _, _, accs = lax.while_loop(cond, body, init)
    for h in range(hp):
        o_ref[:, h * D_HEAD:(h + 1) * D_HEAD] = accs[h].astype(o_ref.dtype)


def _sb_attention(q, k_all, v_all, *, tq, tk, past_len, hp=2):
    B, L, _ = q.shape
    Lkp = k_all.shape[1]
    r = np.arange(tk)
    u = jnp.asarray((r[:, None] > r[None, :]).astype(np.float32), BF16)
    wp = hp * D_HEAD
    return pl.pallas_call(
        functools.partial(_sb_kernel, tq=tq, tk=tk, past_len=past_len, hp=hp),
        grid=(B, H_SB // hp, L // tq),
        in_specs=[pl.BlockSpec((None, tq, wp), lambda b, h, i: (b, i, h)),
                  pl.BlockSpec((None, Lkp, wp), lambda b, h, i: (b, 0, h)),
                  pl.BlockSpec((None, Lkp, wp), lambda b, h, i: (b, 0, h)),
                  pl.BlockSpec((tk, tk), lambda b, h, i: (0, 0))],
        out_specs=pl.BlockSpec((None, tq, wp), lambda b, h, i: (b, i, h)),
        out_shape=jax.ShapeDtypeStruct((B, L, W_SB), BF16),
        compiler_params=_cparams(("arbitrary", "arbitrary", "arbitrary")),
        name="sb_attn",
    )(q, k_all, v_all, u)


def _bucket_edges():
    nb = N_BUCKETS // 2
    max_exact = nb // 2

    def bucket(rel):
        n = abs(rel)
        if n < max_exact:
            v = n
        else:
            v = max_exact + int(np.log(np.float32(n) / max_exact)
                                / np.log(REL_MAX_DIST / max_exact) * (nb - max_exact))
            v = min(v, nb - 1)
        return (nb if rel > 0 else 0) + v

    lo = -4 * REL_MAX_DIST
    assert bucket(lo) == nb - 1
    edges = []
    prev = bucket(lo)
    for rel in range(lo + 1, CHUNK):
        bk = bucket(rel)
        if bk != prev:
            edges.append((rel, prev))
            prev = bk
    edges.append((CHUNK, prev))
    return edges


_EDGES = _bucket_edges()
_FAR_BUCKET = _EDGES[0][1]
_NEAR_REL = _EDGES[0][0]


def _bias_kernel(d_ref, rel_ref, o_ref):
    tk, tq = o_ref.shape
    h = pl.program_id(1)
    rel = (lax.broadcasted_iota(I32, (tk, tq), 0) - lax.broadcasted_iota(I32, (tk, tq), 1)
           - d_ref[pl.program_id(0)])
    tile = jnp.full((tk, tq), rel_ref[_EDGES[-1][1], h], F32)
    for edge, bk in reversed(_EDGES[:-1]):
        tile = jnp.where(rel < edge, rel_ref[bk, h], tile)
    o_ref[...] = (tile - rel_ref[_FAR_BUCKET, h]) * LOG2E


def _bias_tiles(ds, rel_table, *, tq, tk):
    return pl.pallas_call(
        _bias_kernel,
        grid=(len(ds), H_SA),
        in_specs=[pl.BlockSpec(memory_space=pltpu.SMEM), pl.BlockSpec(memory_space=pltpu.SMEM)],
        out_specs=pl.BlockSpec((None, None, tk, tq), lambda n, h: (n, h, 0, 0)),
        out_shape=jax.ShapeDtypeStruct((len(ds), H_SA, tk, tq), F32),
        compiler_params=_cparams(("arbitrary", "arbitrary")),
        name="bias_tiles",
    )(jnp.asarray(np.asarray(ds, np.int32)), rel_table)


_NT = (((1,), (1,)), ((), ()))


def _dsa_kernel(itab, jtab, ftab, ltab, ntab, nbtab, fartab,
                qit_ref, wt_ref, qt_ref, ki_ref, k_ref, vt_ref, bias_ref, o_ref,
                keys_ref, half_ref, thr_ref, m_ref, l_ref, acc_ref, madd_ref, s_ref, p_ref,
                *, tq, tk, past_len, sub, csub):
    p = pl.program_id(1)
    i = itab[p]
    j = jtab[p]
    qpos = past_len + i * tq + lax.broadcasted_iota(I32, (1, tq), 1)
    lim = (qpos // CHUNK + 1) * CHUNK

    @pl.when(ftab[p] == 1)
    def _():
        wt = wt_ref[...] * IDX_SCALE

        def score_blk(c, carry):
            off = pl.multiple_of(c * sub, sub)
            kib = ki_ref[pl.ds(off, sub), :]
            s = jnp.zeros((sub, tq), F32)
            for h in range(H_IDX):
                sh = jnp.dot(kib, qit_ref[h * D_IDX:(h + 1) * D_IDX, :],
                             preferred_element_type=F32)
                s = s + wt[h:h + 1, :] * jnp.maximum(sh, 0.0)
            bits = pltpu.bitcast(s, I32)
            key = jnp.where(bits < 0, bits ^ jnp.int32(0x7FFFFFFF), bits)
            kpos = c * sub + lax.broadcasted_iota(I32, (sub, 1), 0)
            key = jnp.where(kpos < lim, key, jnp.int32(INT_MIN))
            keys_ref[pl.ds(off, sub), :] = key
            half_ref[pl.ds(off, sub), :] = (key >> 16).astype(jnp.int16)
            return carry

        per = tk // sub

        def score_grp(g, carry):
            for u in range(per):
                score_blk(g * per + u, carry)
            return carry

        lax.fori_loop(0, nbtab[p], score_grp, 0)

        need = jnp.minimum(TOPK_MAX, lim)
        nchunk = nbtab[p] * (tk // csub)
        i16_min = -2 ** 15

        def count_ge(cand):
            c16 = jnp.broadcast_to(cand, (16, tq)).astype(jnp.int16)

            def cnt_blk(c, acc):
                kb = half_ref[pl.ds(pl.multiple_of(c * csub, csub), csub), :]
                parts = [jnp.where(kb[16 * r:16 * (r + 1), :] >= c16, jnp.int16(1), jnp.int16(0))
                         for r in range(csub // 16)]
                while len(parts) > 1:
                    parts = [a + b for a, b in zip(parts[::2], parts[1::2])]
                return acc + parts[0]

            acc = lax.fori_loop(0, nchunk, cnt_blk, jnp.zeros((16, tq), jnp.int16))
            return jnp.sum(acc.astype(I32), axis=0, keepdims=True)

        def kth_half(want):
            def cond(carry):
                it, _, done = carry
                return jnp.logical_and(it < 16, jnp.logical_not(done))

            def bit_step(carry):
                it, (tpre, cur), _ = carry
                bitv = lax.shift_left(jnp.int32(1), 15 - it)
                cnt = count_ge((tpre | bitv) + i16_min)
                hit = cnt >= want
                tpre = jnp.where(hit, tpre | bitv, tpre)
                cur = jnp.where(hit, cnt, cur)
                return it + 1, (tpre, cur), jnp.min(jnp.where(cur == want, 1.0, 0.0)) == 1.0

            init = (jnp.zeros((1, tq), I32), jnp.full((1, tq), -1, I32))
            return lax.while_loop(cond, bit_step, (jnp.int32(0), init, jnp.bool_(False)))[1][0]

        hi = kth_half(need) + i16_min
        above = jnp.where(hi == 2 ** 15 - 1, 0, count_ge(jnp.minimum(hi + 1, 2 ** 15 - 1)))

        def low_halves(c, carry):
            rows = pl.ds(pl.multiple_of(c * csub, csub), csub)
            k32 = keys_ref[rows, :]
            lo = (k32 & 0xFFFF) + i16_min
            half_ref[rows, :] = jnp.where((k32 >> 16) == hi, lo, i16_min).astype(jnp.int16)
            return carry

        lax.fori_loop(0, nchunk, low_halves, 0)
        thr_ref[...] = lax.shift_left(hi, 16) | kth_half(need - above)
        m_ref[...] = jnp.full(m_ref.shape, NEG, F32)
        l_ref[...] = jnp.zeros(l_ref.shape, F32)
        acc_ref[...] = jnp.zeros(acc_ref.shape, F32)

    thr = thr_ref[...]
    for c in range(tk // sub):
        rows = pl.ds(pl.multiple_of(j * tk + c * sub, sub), sub)
        madd_ref[c * sub:(c + 1) * sub, :] = jnp.where(keys_ref[rows, :] >= thr, 0.0, NEG)

    def logits(h, near):
        cs = slice(h * D_HEAD, (h + 1) * D_HEAD)
        qh = qt_ref[cs, :]
        mx = jnp.full((8, tq), NEG, F32)
        for c in range(tk // sub):
            rs = slice(c * sub, (c + 1) * sub)
            s = jnp.dot(k_ref[rs, cs], qh, preferred_element_type=F32)
            extra = bias_ref[h, rs, :] + madd_ref[rs, :] if near else madd_ref[rs, :]
            s = s * (ATT_SCALE * LOG2E) + extra
            s_ref[h % 2, rs, :] = s
            mx = jnp.maximum(mx, jnp.max(s.reshape(sub // 8, 8, tq), axis=0))
        return jnp.max(mx, axis=0, keepdims=True)

    def attend(near):
        smax = logits(0, near)
        for h in range(H_SA):
            cs = slice(h * D_HEAD, (h + 1) * D_HEAD)
            m_prev = m_ref[h:h + 1, :]
            m_new = jnp.maximum(m_prev, smax)
            if h + 1 < H_SA:
                smax = logits(h + 1, near)
            alpha = jnp.exp2(m_prev - m_new)
            ps = jnp.zeros((8, tq), F32)
            for c in range(tk // sub):
                rs = slice(c * sub, (c + 1) * sub)
                pr = jnp.exp2(s_ref[h % 2, rs, :] - m_new)
                ps = ps + jnp.sum(pr.reshape(sub // 8, 8, tq), axis=0)
                p_ref[h % 2, rs, :] = pr.astype(BF16)
            l_ref[h:h + 1, :] = alpha * l_ref[h:h + 1, :] + jnp.sum(ps, axis=0, keepdims=True)
            acc_ref[cs, :] = alpha * acc_ref[cs, :] + jnp.dot(
                vt_ref[cs, :], p_ref[h % 2], preferred_element_type=F32)
            m_ref[h:h + 1, :] = m_new

    pl.when(fartab[p] == 0)(functools.partial(attend, True))
    pl.when(fartab[p] == 1)(functools.partial(attend, False))

    @pl.when(ltab[p] == 1)
    def _():
        for h in range(H_SA):
            cs = slice(h * D_HEAD, (h + 1) * D_HEAD)
            o_ref[cs, :] = (acc_ref[cs, :] / l_ref[h:h + 1, :]).astype(o_ref.dtype)


def _dsa_attention(qit, wt, qt, ki_all, k_all, vt_all, rel_table, *, tq, tk, past_len):
    B, _, L = qt.shape
    Lkp = k_all.shape[1]
    assert Lkp % tk == 0 and L % tq == 0
    nq = L // tq
    it, jt, ft, lt_, nt, nbt, fart = [], [], [], [], [], [], []
    near_ds = []
    for i in range(nq):
        qpos0 = past_len + i * tq
        lim_max = ((qpos0 + tq - 1) // CHUNK + 1) * CHUNK
        jmax = (lim_max - 1) // tk
        for j in range(jmax + 1):
            d = qpos0 - j * tk
            if (tk - 1) - d >= _NEAR_REL:
                if d not in near_ds:
                    near_ds.append(d)
                nt.append(near_ds.index(d))
                fart.append(0)
            else:
                nt.append(nt[-1] if nt else 0)
                fart.append(1)
            it.append(i); jt.append(j); ft.append(int(j == 0)); lt_.append(int(j == jmax))
            nbt.append(jmax + 1)
    bias = _bias_tiles(near_ds, rel_table, tq=tq, tk=tk)
    tabs = [jnp.asarray(np.asarray(t, np.int32)) for t in (it, jt, ft, lt_, nt, nbt, fart)]
    grid_spec = pltpu.PrefetchScalarGridSpec(
        num_scalar_prefetch=7,
        grid=(B, len(it)),
        in_specs=[
            pl.BlockSpec((None, H_IDX * D_IDX, tq), lambda b, p, it, jt, *_: (b, 0, it[p])),
            pl.BlockSpec((None, H_IDX, tq), lambda b, p, it, jt, *_: (b, 0, it[p])),
            pl.BlockSpec((None, W_SA, tq), lambda b, p, it, jt, *_: (b, 0, it[p])),
            pl.BlockSpec((None, Lkp, D_IDX), lambda b, p, it, jt, *_: (b, 0, 0),
                         pipeline_mode=pl.Buffered(1)),
            pl.BlockSpec((None, tk, W_SA), lambda b, p, it, jt, *_: (b, jt[p], 0)),
            pl.BlockSpec((None, W_SA, tk), lambda b, p, it, jt, *_: (b, 0, jt[p])),
            pl.BlockSpec((None, H_SA, tk, tq),
                         lambda b, p, it, jt, ft, lt, nt, *_: (nt[p], 0, 0, 0)),
        ],
        out_specs=pl.BlockSpec((None, W_SA, tq), lambda b, p, it, jt, *_: (b, 0, it[p])),
        scratch_shapes=[
            pltpu.VMEM((Lkp, tq), I32),
            pltpu.VMEM((Lkp, tq), jnp.int16),
            pltpu.VMEM((1, tq), I32),
            pltpu.VMEM((H_SA, tq), F32),
            pltpu.VMEM((H_SA, tq), F32),
            pltpu.VMEM((W_SA, tq), F32),
            pltpu.VMEM((tk, tq), F32),
            pltpu.VMEM((2, tk, tq), F32),
            pltpu.VMEM((2, tk, tq), BF16),
        ],
    )
    return pl.pallas_call(
        functools.partial(_dsa_kernel, tq=tq, tk=tk, past_len=past_len,
                          sub=min(tk, 128), csub=min(tk, 256)),
        grid_spec=grid_spec,
        out_shape=jax.ShapeDtypeStruct((B, W_SA, L), BF16),
        compiler_params=_cparams(("arbitrary", "arbitrary")),
        name="dsa_attn",
    )(*tabs, qit, wt, qt, ki_all, k_all, vt_all, bias)


def _merge_kernel(osb_ref, osa_ref, g_ref, x_ref, gt_ref, wsb_ref, wsa_ref, wo_ref, o_ref):
    bt, lt, D = x_ref.shape
    tm = bt * lt
    a = jnp.dot(osb_ref[...].reshape(tm, W_SB), wsb_ref[...], preferred_element_type=F32)
    c = jnp.dot(osa_ref[...].reshape(tm, W_SA), wsa_ref[...], preferred_element_type=F32)
    g = g_ref[...].reshape(tm, 2 * D)
    merged = g[:, :D] * a + g[:, D:] * c
    y = jnp.dot(merged.astype(BF16), wo_ref[...], preferred_element_type=F32)
    o_ref[...] = x_ref[...] + gt_ref[...] * y.reshape(bt, lt, D)


def _merge(osb, osa, g, x, gt, wsb, wsa, wo, *, bt, lt):
    B, L, D = x.shape
    row = lambda w: pl.BlockSpec((bt, lt, w), lambda b, t: (b, t, 0))
    full = lambda a: pl.BlockSpec(a.shape, lambda b, t: (0, 0), pipeline_mode=pl.Buffered(1))
    return pl.pallas_call(
        _merge_kernel,
        grid=(B // bt, L // lt),
        in_specs=[row(W_SB), row(W_SA), row(2 * D), row(D),
                  pl.BlockSpec((bt, 1, D), lambda b, t: (b, 0, 0)),
                  full(wsb), full(wsa), full(wo)],
        out_specs=row(D),
        out_shape=jax.ShapeDtypeStruct((B, L, D), F32),
        compiler_params=_cparams(("arbitrary", "arbitrary")),
        name="merge_out",
    )(osb, osa, g, x, gt, wsb, wsa, wo)


HALO = 8


def _ffn_kernel(x_ref, sc_ref, sh_ref, gt_ref, g_ref, gfin_ref, prev_ref, wg_ref, wv_ref,
                cw_ref, cb_ref, wd_ref, x2_ref, st_ref,
                h_ref, acc_ref, ext_ref, carry_ref, *, final):
    bt, lt, D = x_ref.shape
    tm = bt * lt
    fc = wg_ref.shape[1]
    t = pl.program_id(1)
    f = pl.program_id(2)

    @pl.when(f == 0)
    def _():
        h = _norm_mod(x_ref[...], g_ref[...], sc_ref[...], sh_ref[...])
        h_ref[...] = h.reshape(tm, D).astype(BF16)
        acc_ref[...] = jnp.zeros(acc_ref.shape, F32)

    h = h_ref[...]
    ug = jnp.dot(h, wg_ref[...], preferred_element_type=F32).reshape(bt, lt, fc)
    uv = jnp.dot(h, wv_ref[...], preferred_element_type=F32).reshape(bt, lt, fc)

    @pl.when(t == 0)
    def _():
        ext_ref[:, HALO - 2:HALO, :] = prev_ref[...]

    @pl.when(t > 0)
    def _():
        ext_ref[:, HALO - 2:HALO, :] = carry_ref[f]

    ext_ref[:, HALO:HALO + lt, :] = ug
    tail = ug[:, lt - 2:lt, :]
    carry_ref[f] = tail
    st_ref[:, f] = tail
    cw = cw_ref[...]
    conv = (cb_ref[...] + cw[0:1, :] * ext_ref[:, HALO - 2:HALO - 2 + lt, :]
            + cw[1:2, :] * ext_ref[:, HALO - 1:HALO - 1 + lt, :] + cw[2:3, :] * ug)
    act = conv * jax.nn.sigmoid(conv) * uv
    acc_ref[...] += jnp.dot(act.reshape(tm, fc).astype(BF16), wd_ref[...],
                            preferred_element_type=F32)

    @pl.when(f == pl.num_programs(2) - 1)
    def _():
        x2 = x_ref[...] + gt_ref[...] * acc_ref[...].reshape(bt, lt, D)
        if final:
            ms = jnp.mean(x2 * x2, axis=-1, keepdims=True)
            x2 = x2 * lax.rsqrt(ms + EPS) * gfin_ref[...]
        x2_ref[...] = x2


def _ffn(x, sc, sh, gt, g, gfin, prev, w_up, cw, cb, w_down, *, bt, lt, fc, final):
    B, L, D = x.shape
    F = w_down.shape[0]
    nf = F // fc
    row = pl.BlockSpec((bt, lt, D), lambda b, t, f: (b, t, 0))
    mod = pl.BlockSpec((bt, 1, D), lambda b, t, f: (b, 0, 0))
    vec = pl.BlockSpec((1, D), lambda b, t, f: (0, 0))
    return pl.pallas_call(
        functools.partial(_ffn_kernel, final=final),
        grid=(B // bt, L // lt, nf),
        in_specs=[row, mod, mod, mod, vec, vec,
                  pl.BlockSpec((bt, CONV_W - 1, fc), lambda b, t, f: (b, 0, f)),
                  pl.BlockSpec((D, fc), lambda b, t, f: (0, f)),
                  pl.BlockSpec((D, fc), lambda b, t, f: (0, nf + f)),
                  pl.BlockSpec((CONV_W, fc), lambda b, t, f: (0, f)),
                  pl.BlockSpec((1, fc), lambda b, t, f: (0, f)),
                  pl.BlockSpec((fc, D), lambda b, t, f: (f, 0))],
        out_specs=[row, pl.BlockSpec((bt, nf, CONV_W - 1, fc), lambda b, t, f: (b, 0, 0, 0))],
        out_shape=[jax.ShapeDtypeStruct((B, L, D), F32),
                   jax.ShapeDtypeStruct((B, nf, CONV_W - 1, fc), F32)],
        scratch_shapes=[pltpu.VMEM((bt * lt, D), BF16),
                        pltpu.VMEM((bt * lt, D), F32),
                        pltpu.VMEM((bt, HALO + lt, fc), F32),
                        pltpu.VMEM((nf, bt, CONV_W - 1, fc), F32)],
        compiler_params=_cparams(("arbitrary", "arbitrary", "arbitrary")),
        name="conv_ffn",
    )(x, sc, sh, gt, g, gfin, prev, w_up, w_up, cw, cb, w_down)


def _pad_cols(w, mult):
    n = w.shape[1]
    npad = -(-n // mult) * mult
    return jnp.pad(w, ((0, 0), (0, npad - n)))


def _tiles(B, L, rows):
    if L >= rows:
        assert L % rows == 0
        return 1, rows
    assert rows % L == 0 and B % (rows // L) == 0
    return rows // L, L


def _layer(x, mod, past, wts, *, last, g_final):
    (g_mix, w_in_sb, w_in_sa, w_in_ix, w_gate, w_br_sb, w_br_sa, w_out, rel_table, g_ffn, w_up,
     conv_w, conv_b, w_down) = wts
    B, L, D = x.shape
    sh1, sc1, gt1, sh2, sc2, gt2 = (m[:, None, :] for m in jnp.split(mod, 6, axis=-1))
    past_len = 0 if past is None else past[0].shape[1]

    wn = W_SB
    bt, lt = _tiles(B, L, 512)
    fm = bt == 1
    qlay = "feature" if fm else "row"
    call = functools.partial(_proj, x, sc1, sh1, g_mix, bt=bt, lt=lt)
    q_sb, k_sb, kb_sb, v_sb, vb_sb = call(
        w_in_sb, [_Out(0, 0, wn, BF16), _Out(1, 0, wn, F32, "heads"), _Out(1, 0, wn, BF16),
                  _Out(2, 0, wn, F32, "heads"), _Out(2, 0, wn, BF16)], tn=wn, name="proj_sb")
    q_sa, k_sa, kb_sa, v_sa, vb_sa = call(
        w_in_sa, [_Out(0, 0, wn, BF16, qlay), _Out(1, 0, wn, F32, "heads"), _Out(1, 0, wn, BF16),
                  _Out(2, 0, wn, F32, "heads"), _Out(2, 0, wn, BF16, qlay)],
        tn=wn, name="proj_sa")
    w_ix_out = (_Out(1, 0, LANE, F32, "feature", (D_IDX, D_IDX + H_IDX)) if fm
                else _Out(1, D_IDX, D_IDX + H_IDX, F32))
    q_ix, k_ix, kb_ix, w_ix = call(
        w_in_ix, [_Out(0, 0, wn, BF16, qlay), _Out(1, 0, D_IDX, F32), _Out(1, 0, D_IDX, BF16),
                  w_ix_out], tn=wn, name="proj_ix")
    bt, lt = _tiles(B, L, 1024)
    gate, = _proj(x, sc1, sh1, g_mix, w_gate, [_Out(None, 0, 512, F32)], bt=bt, lt=lt, tn=512,
                  sigmoid=True, name="proj_gate")
    if not fm:
        q_sa, q_ix, w_ix = (jnp.swapaxes(a, 1, 2) for a in (q_sa, q_ix, w_ix))

    tq = min(256, L)
    tk_sb = 256
    tk_sa = 512 if L >= 512 else 256
    l_keys = past_len + L

    def keys_of(new, old, tk, feature_major=False):
        lkp = -(-l_keys // tk) * tk
        parts = []
        if old is not None:
            o = old.reshape(B, past_len, -1).astype(BF16)
            parts.append(jnp.swapaxes(o, 1, 2) if feature_major else o)
        parts.append(new)
        if lkp > l_keys:
            nf = new.shape[1] if feature_major else new.shape[2]
            parts.append(jnp.zeros((B, nf, lkp - l_keys) if feature_major
                                   else (B, lkp - l_keys, nf), BF16))
        axis = 2 if feature_major else 1
        return parts[0] if len(parts) == 1 else jnp.concatenate(parts, axis=axis)

    old = (None,) * 5 if past is None else past[:5]
    o_sb = _sb_attention(q_sb, keys_of(kb_sb, old[0], tk_sb), keys_of(vb_sb, old[1], tk_sb),
                         tq=tq, tk=tk_sb, past_len=past_len)
    vt = (keys_of(vb_sa, old[3], tk_sa, True) if fm
          else jnp.swapaxes(keys_of(vb_sa, old[3], tk_sa), 1, 2))
    o_sa_t = _dsa_attention(q_ix, w_ix, q_sa, keys_of(kb_ix, old[4], tk_sa),
                            keys_of(kb_sa, old[2], tk_sa), vt, rel_table,
                            tq=tq, tk=tk_sa, past_len=past_len)
    o_sa = jnp.swapaxes(o_sa_t, 1, 2)

    bt, lt = _tiles(B, L, 256)
    x1 = _merge(o_sb, o_sa, gate, x, gt1, w_br_sb, w_br_sa, w_out, bt=bt, lt=lt)

    prev = jnp.zeros((B, CONV_W - 1, w_down.shape[0]), F32) if past is None else past[5]
    bt, lt = _tiles(B, L, 512)
    x2, conv_state = _ffn(x1, sc2, sh2, gt2, g_ffn, g_final, prev, w_up, conv_w, conv_b,
                          w_down, bt=bt, lt=lt, fc=512, final=last)
    conv_state = jnp.swapaxes(conv_state, 1, 2).reshape(B, CONV_W - 1, -1)
    return x2, (k_sb, v_sb, k_sa, v_sa, k_ix, conv_state)


def kernel(x_prompt, x_sample, cache_sb_k, cache_sb_v, cache_sa_k, cache_sa_v, cache_idx_k,
           state_ffn_conv, c_prompt, c_sample, w_ada, b_ada, g_mix, w_in, w_gate, w_br_sb,
           w_br_sa, w_out, rel_table, g_ffn, w_up, conv_w, conv_b, w_down, g_final):
    depth = w_ada.shape[0]
    nbp = c_prompt.shape[0]
    nbs = c_sample.shape[0]
    rows = -(-(nbp + nbs) // 8) * 8
    c_all = jnp.concatenate([c_prompt, c_sample,
                             jnp.zeros((rows - nbp - nbs, c_prompt.shape[1]), F32)], axis=0)
    xp, xs = x_prompt, x_sample
    new_p, new_s = [], []
    for l in range(depth):
        mod = _adaln(c_all, w_ada[l], b_ada[l][None, :])
        wb = w_in[l].astype(BF16)
        wts = (g_mix[l][None, :], wb[:, :COL_Q_SA], wb[:, COL_Q_SA:COL_Q_IX],
               _pad_cols(wb[:, COL_Q_IX:], 2 * W_SB), w_gate[l].astype(BF16),
               w_br_sb[l].astype(BF16), w_br_sa[l].astype(BF16), w_out[l].astype(BF16),
               rel_table, g_ffn[l][None, :], w_up[l].astype(BF16), conv_w[l],
               conv_b[l][None, :], w_down[l].astype(BF16))
        gfin = g_final[None, :]
        past = (cache_sb_k[l], cache_sb_v[l], cache_sa_k[l], cache_sa_v[l], cache_idx_k[l],
                state_ffn_conv[l])
        xp, sp = _layer(xp, mod[:nbp], None, wts, last=l == depth - 1, g_final=gfin)
        xs, ss = _layer(xs, mod[nbp:nbp + nbs], past, wts, last=l == depth - 1, g_final=gfin)
        new_p.append(sp)
        new_s.append(ss)
    stack = lambda states, n: jnp.stack([s[n] for s in states])
    return ((xp, xs) + tuple(stack(new_p, n) for n in range(6))
            + tuple(stack(new_s, n) for n in range(6)))
```

```python
import functools

import numpy as np
import jax
import jax.numpy as jnp
from jax import lax
from jax.experimental import pallas as pl
from jax.experimental.pallas import tpu as pltpu

F32 = jnp.float32
BF16 = jnp.bfloat16
I32 = jnp.int32

CHUNK = 64
D_HEAD = 128
H_SB = 8
H_SA = 8
W_SB = H_SB * D_HEAD
W_SA = H_SA * D_HEAD
H_IDX = 16
D_IDX = 64
TOPK_MAX = 256
N_BUCKETS = 32
REL_MAX_DIST = 1024
CONV_W = 3
EPS = 1e-6

COL_Q_SB = 0
COL_K_SB = W_SB
COL_V_SB = 2 * W_SB
COL_Q_SA = 3 * W_SB
COL_K_SA = 3 * W_SB + W_SA
COL_V_SA = 3 * W_SB + 2 * W_SA
COL_Q_IX = 3 * W_SB + 3 * W_SA
COL_K_IX = COL_Q_IX + H_IDX * D_IDX
COL_W_IX = COL_K_IX + D_IDX
IN_COLS = COL_W_IX + H_IDX

LANE = 128
VMEM_LIMIT = 56 * 1024 * 1024

ATT_SCALE = D_HEAD ** -0.5
LOG2E = 1.4426950408889634
IDX_SCALE = (D_IDX ** -0.5) * (H_IDX ** -0.5)
NEG = -1e30
INT_MIN = -2 ** 31
SB_DEAD = -104.0


def _cparams(sem):
    return pltpu.CompilerParams(dimension_semantics=sem, vmem_limit_bytes=VMEM_LIMIT)


def _adaln_kernel(c_ref, w_ref, b_ref, o_ref):
    c = c_ref[...]
    a = c * jax.nn.sigmoid(c)
    o_ref[...] = jnp.dot(a.astype(BF16), w_ref[...].astype(BF16),
                         preferred_element_type=F32) + b_ref[...]


def _adaln(c, w, b):
    R, D = c.shape
    N = w.shape[1]
    tn = 1024
    return pl.pallas_call(
        _adaln_kernel,
        grid=(N // tn,),
        in_specs=[pl.BlockSpec((R, D), lambda n: (0, 0)),
                  pl.BlockSpec((D, tn), lambda n: (0, n)),
                  pl.BlockSpec((1, tn), lambda n: (0, n))],
        out_specs=pl.BlockSpec((R, tn), lambda n: (0, n)),
        out_shape=jax.ShapeDtypeStruct((R, N), F32),
        compiler_params=_cparams(("arbitrary",)),
        name="adaln",
    )(c, w, b)


def _norm_mod(x, g, sc, sh):
    ms = jnp.mean(x * x, axis=-1, keepdims=True)
    y = x * lax.rsqrt(ms + EPS) * g
    return y * (1.0 + sc) + sh


class _Out(tuple):
    __slots__ = ()

    def __new__(cls, tile, lo, hi, dtype, layout="row", rows=None):
        return tuple.__new__(cls, (tile, lo, hi, rows or (0, hi - lo), layout, dtype))


def _proj_kernel(x_ref, sc_ref, sh_ref, g_ref, w_ref, *rest, plan, sigmoid):
    out_refs, h_ref = rest[:-1], rest[-1]
    bt, lt, D = x_ref.shape
    n = pl.program_id(2)

    @pl.when(n == 0)
    def _():
        h = _norm_mod(x_ref[...], g_ref[...], sc_ref[...], sh_ref[...])
        h_ref[...] = h.reshape(bt * lt, D).astype(BF16)

    r = jnp.dot(h_ref[...], w_ref[...], preferred_element_type=F32)
    if sigmoid:
        r = jax.nn.sigmoid(r)
    for (tile, lo, hi, rows, layout, _), o_ref in zip(plan, out_refs):
        def emit(o_ref=o_ref, lo=lo, hi=hi, rows=rows, layout=layout):
            v = r[:, lo:hi]
            if layout == "feature":
                o_ref[...] = v.T[rows[0]:rows[1], :].astype(o_ref.dtype)
            elif layout == "heads":
                nh = (hi - lo) // D_HEAD
                v = pltpu.einshape("t(hd)->thd", v, h=nh)
                o_ref[...] = v.reshape(bt, lt, nh, D_HEAD).astype(o_ref.dtype)
            else:
                o_ref[...] = v.reshape(bt, lt, hi - lo).astype(o_ref.dtype)

        if tile is None:
            emit()
        else:
            pl.when(n == tile)(emit)


def _proj(x, sc, sh, g, w, plan, *, bt, lt, tn, sigmoid=False, name):
    B, L, D = x.shape
    N = w.shape[1]
    out_specs, out_shape = [], []
    for tile, lo, hi, rows, layout, dtype in plan:
        if tile is None:
            out_specs.append(pl.BlockSpec((bt, lt, tn), lambda b, t, n: (b, t, n)))
            out_shape.append(jax.ShapeDtypeStruct((B, L, N), dtype))
        elif layout == "heads":
            nh = (hi - lo) // D_HEAD
            out_specs.append(pl.BlockSpec((bt, lt, nh, D_HEAD), lambda b, t, n: (b, t, 0, 0)))
            out_shape.append(jax.ShapeDtypeStruct((B, L, nh, D_HEAD), dtype))
        elif layout == "feature":
            assert bt == 1
            width = rows[1] - rows[0]
            out_specs.append(pl.BlockSpec((None, width, lt), lambda b, t, n: (b, 0, t)))
            out_shape.append(jax.ShapeDtypeStruct((B, width, L), dtype))
        else:
            out_specs.append(pl.BlockSpec((bt, lt, hi - lo), lambda b, t, n: (b, t, 0)))
            out_shape.append(jax.ShapeDtypeStruct((B, L, hi - lo), dtype))
    return pl.pallas_call(
        functools.partial(_proj_kernel, plan=tuple(plan), sigmoid=sigmoid),
        grid=(B // bt, L // lt, N // tn),
        in_specs=[pl.BlockSpec((bt, lt, D), lambda b, t, n: (b, t, 0)),
                  pl.BlockSpec((bt, 1, D), lambda b, t, n: (b, 0, 0)),
                  pl.BlockSpec((bt, 1, D), lambda b, t, n: (b, 0, 0)),
                  pl.BlockSpec((1, D), lambda b, t, n: (0, 0)),
                  pl.BlockSpec((D, tn), lambda b, t, n: (0, n))],
        out_specs=out_specs,
        out_shape=out_shape,
        scratch_shapes=[pltpu.VMEM((bt * lt, D), BF16)],
        compiler_params=_cparams(("arbitrary", "arbitrary", "arbitrary")),
        name=name,
    )(x, sc, sh, g, w)


def _sb_kernel(q_ref, k_ref, v_ref, u_ref, o_ref, *, tq, tk, past_len, hp):
    i = pl.program_id(2)
    qpos0 = past_len + i * tq
    qpos = qpos0 + lax.broadcasted_iota(I32, (tq, 1), 0)
    j0 = (qpos0 + tq - 2) // tk
    u = u_ref[...]

    def cond(carry):
        j, go, _, _ = carry
        return jnp.logical_and(j >= 0, go)

    def body(carry):
        j, _, cums, accs = carry
        off = pl.multiple_of(j * tk, tk)
        kpos = j * tk + lax.broadcasted_iota(I32, (1, tk), 1)
        mask = kpos < qpos
        new_cums, new_accs = [], []
        for h in range(hp):
            cs = slice(h * D_HEAD, (h + 1) * D_HEAD)
            z = lax.dot_general(q_ref[:, cs], k_ref[pl.ds(off, tk), cs], _NT,
                                preferred_element_type=F32) * ATT_SCALE
            t = jnp.log(1.0 + jnp.exp(-jnp.abs(z)))
            lk = jnp.where(mask, -(jnp.maximum(z, 0.0) + t), 0.0)
            hi = lk.astype(BF16)
            lo = (lk - hi.astype(F32)).astype(BF16)
            after = cums[h] + (jnp.dot(hi, u, preferred_element_type=F32)
                               + jnp.dot(lo, u, preferred_element_type=F32))
            lsz = jnp.minimum(z, 0.0) - t
            a = jnp.where(mask, jnp.exp(lsz + after), 0.0)
            new_accs.append(accs[h] + jnp.dot(a.astype(BF16), v_ref[pl.ds(off, tk), cs],
                                              preferred_element_type=F32))
            new_cums.append(cums[h] + jnp.sum(lk, axis=1, keepdims=True))
        top = new_cums[0]
        for c in new_cums[1:]:
            top = jnp.maximum(top, c)
        return j - 1, jnp.max(top) > SB_DEAD, tuple(new_cums), tuple(new_accs)

    init = (j0, jnp.bool_(True), (jnp.zeros((tq, 1), F32),) * hp,
            (jnp.zeros((tq, D_HEAD), F32),) * hp)
    _, _, _, accs = lax.while_loop(cond, body, init)
    for h in range(hp):
        o_ref[:, h * D_HEAD:(h + 1) * D_HEAD] = accs[h].astype(o_ref.dtype)


def _sb_attention(q, k_all, v_all, *, tq, tk, past_len, hp=2):
    B, L, _ = q.shape
    Lkp = k_all.shape[1]
    r = np.arange(tk)
    u = jnp.asarray((r[:, None] > r[None, :]).astype(np.float32), BF16)
    wp = hp * D_HEAD
    return pl.pallas_call(
        functools.partial(_sb_kernel, tq=tq, tk=tk, past_len=past_len, hp=hp),
        grid=(B, H_SB // hp, L // tq),
        in_specs=[pl.BlockSpec((None, tq, wp), lambda b, h, i: (b, i, h)),
                  pl.BlockSpec((None, Lkp, wp), lambda b, h, i: (b, 0, h)),
                  pl.BlockSpec((None, Lkp, wp), lambda b, h, i: (b, 0, h)),
                  pl.BlockSpec((tk, tk), lambda b, h, i: (0, 0))],
        out_specs=pl.BlockSpec((None, tq, wp), lambda b, h, i: (b, i, h)),
        out_shape=jax.ShapeDtypeStruct((B, L, W_SB), BF16),
        compiler_params=_cparams(("arbitrary", "arbitrary", "arbitrary")),
        name="sb_attn",
    )(q, k_all, v_all, u)


def _bucket_edges():
    nb = N_BUCKETS // 2
    max_exact = nb // 2

    def bucket(rel):
        n = abs(rel)
        if n < max_exact:
            v = n
        else:
            v = max_exact + int(np.log(np.float32(n) / max_exact)
                                / np.log(REL_MAX_DIST / max_exact) * (nb - max_exact))
            v = min(v, nb - 1)
        return (nb if rel > 0 else 0) + v

    lo = -4 * REL_MAX_DIST
    assert bucket(lo) == nb - 1
    edges = []
    prev = bucket(lo)
    for rel in range(lo + 1, CHUNK):
        bk = bucket(rel)
        if bk != prev:
            edges.append((rel, prev))
            prev = bk
    edges.append((CHUNK, prev))
    return edges


_EDGES = _bucket_edges()
_FAR_BUCKET = _EDGES[0][1]
_NEAR_REL = _EDGES[0][0]


def _bias_kernel(d_ref, rel_ref, o_ref):
    tk, tq = o_ref.shape
    h = pl.program_id(1)
    rel = (lax.broadcasted_iota(I32, (tk, tq), 0) - lax.broadcasted_iota(I32, (tk, tq), 1)
           - d_ref[pl.program_id(0)])
    tile = jnp.full((tk, tq), rel_ref[_EDGES[-1][1], h], F32)
    for edge, bk in reversed(_EDGES[:-1]):
        tile = jnp.where(rel < edge, rel_ref[bk, h], tile)
    o_ref[...] = (tile - rel_ref[_FAR_BUCKET, h]) * LOG2E


def _bias_tiles(ds, rel_table, *, tq, tk):
    return pl.pallas_call(
        _bias_kernel,
        grid=(len(ds), H_SA),
        in_specs=[pl.BlockSpec(memory_space=pltpu.SMEM), pl.BlockSpec(memory_space=pltpu.SMEM)],
        out_specs=pl.BlockSpec((None, None, tk, tq), lambda n, h: (n, h, 0, 0)),
        out_shape=jax.ShapeDtypeStruct((len(ds), H_SA, tk, tq), F32),
        compiler_params=_cparams(("arbitrary", "arbitrary")),
        name="bias_tiles",
    )(jnp.asarray(np.asarray(ds, np.int32)), rel_table)


_NT = (((1,), (1,)), ((), ()))


def _dsa_kernel(itab, jtab, ftab, ltab, ntab, nbtab, fartab,
                qit_ref, wt_ref, qt_ref, ki_ref, k_ref, vt_ref, bias_ref, o_ref,
                keys_ref, half_ref, thr_ref, m_ref, l_ref, acc_ref, madd_ref, s_ref, p_ref,
                *, tq, tk, past_len, sub, csub):
    p = pl.program_id(1)
    i = itab[p]
    j = jtab[p]
    qpos = past_len + i * tq + lax.broadcasted_iota(I32, (1, tq), 1)
    lim = (qpos // CHUNK + 1) * CHUNK

    @pl.when(ftab[p] == 1)
    def _():
        wt = wt_ref[...] * IDX_SCALE

        def score_blk(c, carry):
            off = pl.multiple_of(c * sub, sub)
            kib = ki_ref[pl.ds(off, sub), :]
            s = jnp.zeros((sub, tq), F32)
            for h in range(H_IDX):
                sh = jnp.dot(kib, qit_ref[h * D_IDX:(h + 1) * D_IDX, :],
                             preferred_element_type=F32)
                s = s + wt[h:h + 1, :] * jnp.maximum(sh, 0.0)
            bits = pltpu.bitcast(s, I32)
            key = jnp.where(bits < 0, bits ^ jnp.int32(0x7FFFFFFF), bits)
            kpos = c * sub + lax.broadcasted_iota(I32, (sub, 1), 0)
            key = jnp.where(kpos < lim, key, jnp.int32(INT_MIN))
            keys_ref[pl.ds(off, sub), :] = key
            half_ref[pl.ds(off, sub), :] = (key >> 16).astype(jnp.int16)
            return carry

        per = tk // sub

        def score_grp(g, carry):
            for u in range(per):
                score_blk(g * per + u, carry)
            return carry

        lax.fori_loop(0, nbtab[p], score_grp, 0)

        need = jnp.minimum(TOPK_MAX, lim)
        nchunk = nbtab[p] * (tk // csub)
        i16_min = -2 ** 15

        def count_ge(cand):
            c16 = jnp.broadcast_to(cand, (16, tq)).astype(jnp.int16)

            def cnt_blk(c, acc):
                kb = half_ref[pl.ds(pl.multiple_of(c * csub, csub), csub), :]
                parts = [jnp.where(kb[16 * r:16 * (r + 1), :] >= c16, jnp.int16(1), jnp.int16(0))
                         for r in range(csub // 16)]
                while len(parts) > 1:
                    parts = [a + b for a, b in zip(parts[::2], parts[1::2])]
                return acc + parts[0]

            acc = lax.fori_loop(0, nchunk, cnt_blk, jnp.zeros((16, tq), jnp.int16))
            return jnp.sum(acc.astype(I32), axis=0, keepdims=True)

        def kth_half(want):
            def bit_step(it, tpre):
                bitv = lax.shift_left(jnp.int32(1), 15 - it)
                hit = count_ge((tpre | bitv) + i16_min) >= want
                return jnp.where(hit, tpre | bitv, tpre)

            return lax.fori_loop(0, 16, bit_step, jnp.zeros((1, tq), I32))

        hi = kth_half(need) + i16_min
        above = jnp.where(hi == 2 ** 15 - 1, 0, count_ge(jnp.minimum(hi + 1, 2 ** 15 - 1)))

        def low_halves(c, carry):
            rows = pl.ds(pl.multiple_of(c * csub, csub), csub)
            k32 = keys_ref[rows, :]
            lo = (k32 & 0xFFFF) + i16_min
            half_ref[rows, :] = jnp.where((k32 >> 16) == hi, lo, i16_min).astype(jnp.int16)
            return carry

        lax.fori_loop(0, nchunk, low_halves, 0)
        thr_ref[...] = lax.shift_left(hi, 16) | kth_half(need - above)
        m_ref[...] = jnp.full(m_ref.shape, NEG, F32)
        l_ref[...] = jnp.zeros(l_ref.shape, F32)
        acc_ref[...] = jnp.zeros(acc_ref.shape, F32)

    thr = thr_ref[...]
    for c in range(tk // sub):
        rows = pl.ds(pl.multiple_of(j * tk + c * sub, sub), sub)
        madd_ref[c * sub:(c + 1) * sub, :] = jnp.where(keys_ref[rows, :] >= thr, 0.0, NEG)

    def logits(h, near):
        cs = slice(h * D_HEAD, (h + 1) * D_HEAD)
        qh = qt_ref[cs, :]
        mx = jnp.full((8, tq), NEG, F32)
        for c in range(tk // sub):
            rs = slice(c * sub, (c + 1) * sub)
            s = jnp.dot(k_ref[rs, cs], qh, preferred_element_type=F32)
            extra = bias_ref[h, rs, :] + madd_ref[rs, :] if near else madd_ref[rs, :]
            s = s * (ATT_SCALE * LOG2E) + extra
            s_ref[h % 2, rs, :] = s
            mx = jnp.maximum(mx, jnp.max(s.reshape(sub // 8, 8, tq), axis=0))
        return jnp.max(mx, axis=0, keepdims=True)

    def attend(near):
        smax = logits(0, near)
        for h in range(H_SA):
            cs = slice(h * D_HEAD, (h + 1) * D_HEAD)
            m_prev = m_ref[h:h + 1, :]
            m_new = jnp.maximum(m_prev, smax)
            if h + 1 < H_SA:
                smax = logits(h + 1, near)
            alpha = jnp.exp2(m_prev - m_new)
            ps = jnp.zeros((8, tq), F32)
            for c in range(tk // sub):
                rs = slice(c * sub, (c + 1) * sub)
                pr = jnp.exp2(s_ref[h % 2, rs, :] - m_new)
                ps = ps + jnp.sum(pr.reshape(sub // 8, 8, tq), axis=0)
                p_ref[h % 2, rs, :] = pr.astype(BF16)
            l_ref[h:h + 1, :] = alpha * l_ref[h:h + 1, :] + jnp.sum(ps, axis=0, keepdims=True)
            acc_ref[cs, :] = alpha * acc_ref[cs, :] + jnp.dot(
                vt_ref[cs, :], p_ref[h % 2], preferred_element_type=F32)
            m_ref[h:h + 1, :] = m_new

    pl.when(fartab[p] == 0)(functools.partial(attend, True))
    pl.when(fartab[p] == 1)(functools.partial(attend, False))

    @pl.when(ltab[p] == 1)
    def _():
        for h in range(H_SA):
            cs = slice(h * D_HEAD, (h + 1) * D_HEAD)
            o_ref[cs, :] = (acc_ref[cs, :] / l_ref[h:h + 1, :]).astype(o_ref.dtype)


def _dsa_attention(qit, wt, qt, ki_all, k_all, vt_all, rel_table, *, tq, tk, past_len):
    B, _, L = qt.shape
    Lkp = k_all.shape[1]
    assert Lkp % tk == 0 and L % tq == 0
    nq = L // tq
    it, jt, ft, lt_, nt, nbt, fart = [], [], [], [], [], [], []
    near_ds = []
    for i in range(nq):
        qpos0 = past_len + i * tq
        lim_max = ((qpos0 + tq - 1) // CHUNK + 1) * CHUNK
        jmax = (lim_max - 1) // tk
        for j in range(jmax + 1):
            d = qpos0 - j * tk
            if (tk - 1) - d >= _NEAR_REL:
                if d not in near_ds:
                    near_ds.append(d)
                nt.append(near_ds.index(d))
                fart.append(0)
            else:
                nt.append(nt[-1] if nt else 0)
                fart.append(1)
            it.append(i); jt.append(j); ft.append(int(j == 0)); lt_.append(int(j == jmax))
            nbt.append(jmax + 1)
    bias = _bias_tiles(near_ds, rel_table, tq=tq, tk=tk)
    tabs = [jnp.asarray(np.asarray(t, np.int32)) for t in (it, jt, ft, lt_, nt, nbt, fart)]
    grid_spec = pltpu.PrefetchScalarGridSpec(
        num_scalar_prefetch=7,
        grid=(B, len(it)),
        in_specs=[
            pl.BlockSpec((None, H_IDX * D_IDX, tq), lambda b, p, it, jt, *_: (b, 0, it[p])),
            pl.BlockSpec((None, H_IDX, tq), lambda b, p, it, jt, *_: (b, 0, it[p])),
            pl.BlockSpec((None, W_SA, tq), lambda b, p, it, jt, *_: (b, 0, it[p])),
            pl.BlockSpec((None, Lkp, D_IDX), lambda b, p, it, jt, *_: (b, 0, 0),
                         pipeline_mode=pl.Buffered(1)),
            pl.BlockSpec((None, tk, W_SA), lambda b, p, it, jt, *_: (b, jt[p], 0)),
            pl.BlockSpec((None, W_SA, tk), lambda b, p, it, jt, *_: (b, 0, jt[p])),
            pl.BlockSpec((None, H_SA, tk, tq),
                         lambda b, p, it, jt, ft, lt, nt, *_: (nt[p], 0, 0, 0)),
        ],
        out_specs=pl.BlockSpec((None, W_SA, tq), lambda b, p, it, jt, *_: (b, 0, it[p])),
        scratch_shapes=[
            pltpu.VMEM((Lkp, tq), I32),
            pltpu.VMEM((Lkp, tq), jnp.int16),
            pltpu.VMEM((1, tq), I32),
            pltpu.VMEM((H_SA, tq), F32),
            pltpu.VMEM((H_SA, tq), F32),
            pltpu.VMEM((W_SA, tq), F32),
            pltpu.VMEM((tk, tq), F32),
            pltpu.VMEM((2, tk, tq), F32),
            pltpu.VMEM((2, tk, tq), BF16),
        ],
    )
    return pl.pallas_call(
        functools.partial(_dsa_kernel, tq=tq, tk=tk, past_len=past_len,
                          sub=min(tk, 128), csub=min(tk, 512)),
        grid_spec=grid_spec,
        out_shape=jax.ShapeDtypeStruct((B, W_SA, L), BF16),
        compiler_params=_cparams(("arbitrary", "arbitrary")),
        name="dsa_attn",
    )(*tabs, qit, wt, qt, ki_all, k_all, vt_all, bias)


def _merge_kernel(osb_ref, osa_ref, g_ref, x_ref, gt_ref, wsb_ref, wsa_ref, wo_ref, o_ref):
    bt, lt, D = x_ref.shape
    tm = bt * lt
    a = jnp.dot(osb_ref[...].reshape(tm, W_SB), wsb_ref[...], preferred_element_type=F32)
    c = jnp.dot(osa_ref[...].reshape(tm, W_SA), wsa_ref[...], preferred_element_type=F32)
    g = g_ref[...].reshape(tm, 2 * D)
    merged = g[:, :D] * a + g[:, D:] * c
    y = jnp.dot(merged.astype(BF16), wo_ref[...], preferred_element_type=F32)
    o_ref[...] = x_ref[...] + gt_ref[...] * y.reshape(bt, lt, D)


def _merge(osb, osa, g, x, gt, wsb, wsa, wo, *, bt, lt):
    B, L, D = x.shape
    row = lambda w: pl.BlockSpec((bt, lt, w), lambda b, t: (b, t, 0))
    full = lambda a: pl.BlockSpec(a.shape, lambda b, t: (0, 0), pipeline_mode=pl.Buffered(1))
    return pl.pallas_call(
        _merge_kernel,
        grid=(B // bt, L // lt),
        in_specs=[row(W_SB), row(W_SA), row(2 * D), row(D),
                  pl.BlockSpec((bt, 1, D), lambda b, t: (b, 0, 0)),
                  full(wsb), full(wsa), full(wo)],
        out_specs=row(D),
        out_shape=jax.ShapeDtypeStruct((B, L, D), F32),
        compiler_params=_cparams(("arbitrary", "arbitrary")),
        name="merge_out",
    )(osb, osa, g, x, gt, wsb, wsa, wo)


HALO = 8


def _ffn_kernel(x_ref, sc_ref, sh_ref, gt_ref, g_ref, gfin_ref, prev_ref, wg_ref, wv_ref,
                cw_ref, cb_ref, wd_ref, x2_ref, st_ref,
                h_ref, acc_ref, ext_ref, carry_ref, *, final):
    bt, lt, D = x_ref.shape
    tm = bt * lt
    fc = wg_ref.shape[1]
    t = pl.program_id(1)
    f = pl.program_id(2)

    @pl.when(f == 0)
    def _():
        h = _norm_mod(x_ref[...], g_ref[...], sc_ref[...], sh_ref[...])
        h_ref[...] = h.reshape(tm, D).astype(BF16)
        acc_ref[...] = jnp.zeros(acc_ref.shape, F32)

    h = h_ref[...]
    ug = jnp.dot(h, wg_ref[...], preferred_element_type=F32).reshape(bt, lt, fc)
    uv = jnp.dot(h, wv_ref[...], preferred_element_type=F32).reshape(bt, lt, fc)

    @pl.when(t == 0)
    def _():
        ext_ref[:, HALO - 2:HALO, :] = prev_ref[...]

    @pl.when(t > 0)
    def _():
        ext_ref[:, HALO - 2:HALO, :] = carry_ref[f]

    ext_ref[:, HALO:HALO + lt, :] = ug
    tail = ug[:, lt - 2:lt, :]
    carry_ref[f] = tail
    st_ref[:, f] = tail
    cw = cw_ref[...]
    conv = (cb_ref[...] + cw[0:1, :] * ext_ref[:, HALO - 2:HALO - 2 + lt, :]
            + cw[1:2, :] * ext_ref[:, HALO - 1:HALO - 1 + lt, :] + cw[2:3, :] * ug)
    act = conv * jax.nn.sigmoid(conv) * uv
    acc_ref[...] += jnp.dot(act.reshape(tm, fc).astype(BF16), wd_ref[...],
                            preferred_element_type=F32)

    @pl.when(f == pl.num_programs(2) - 1)
    def _():
        x2 = x_ref[...] + gt_ref[...] * acc_ref[...].reshape(bt, lt, D)
        if final:
            ms = jnp.mean(x2 * x2, axis=-1, keepdims=True)
            x2 = x2 * lax.rsqrt(ms + EPS) * gfin_ref[...]
        x2_ref[...] = x2


def _ffn(x, sc, sh, gt, g, gfin, prev, w_up, cw, cb, w_down, *, bt, lt, fc, final):
    B, L, D = x.shape
    F = w_down.shape[0]
    nf = F // fc
    row = pl.BlockSpec((bt, lt, D), lambda b, t, f: (b, t, 0))
    mod = pl.BlockSpec((bt, 1, D), lambda b, t, f: (b, 0, 0))
    vec = pl.BlockSpec((1, D), lambda b, t, f: (0, 0))
    return pl.pallas_call(
        functools.partial(_ffn_kernel, final=final),
        grid=(B // bt, L // lt, nf),
        in_specs=[row, mod, mod, mod, vec, vec,
                  pl.BlockSpec((bt, CONV_W - 1, fc), lambda b, t, f: (b, 0, f)),
                  pl.BlockSpec((D, fc), lambda b, t, f: (0, f)),
                  pl.BlockSpec((D, fc), lambda b, t, f: (0, nf + f)),
                  pl.BlockSpec((CONV_W, fc), lambda b, t, f: (0, f)),
                  pl.BlockSpec((1, fc), lambda b, t, f: (0, f)),
                  pl.BlockSpec((fc, D), lambda b, t, f: (f, 0))],
        out_specs=[row, pl.BlockSpec((bt, nf, CONV_W - 1, fc), lambda b, t, f: (b, 0, 0, 0))],
        out_shape=[jax.ShapeDtypeStruct((B, L, D), F32),
                   jax.ShapeDtypeStruct((B, nf, CONV_W - 1, fc), F32)],
        scratch_shapes=[pltpu.VMEM((bt * lt, D), BF16),
                        pltpu.VMEM((bt * lt, D), F32),
                        pltpu.VMEM((bt, HALO + lt, fc), F32),
                        pltpu.VMEM((nf, bt, CONV_W - 1, fc), F32)],
        compiler_params=_cparams(("arbitrary", "arbitrary", "arbitrary")),
        name="conv_ffn",
    )(x, sc, sh, gt, g, gfin, prev, w_up, w_up, cw, cb, w_down)


CACHE_BLOCK = 256


def _cache_kernel(c_ref, n_ref, o_ref, *, nc, feature_major):
    r = pl.program_id(1)
    blk = CACHE_BLOCK

    def put(v):
        o_ref[...] = (v.T if feature_major else v).astype(o_ref.dtype)

    @pl.when(r < nc)
    def _():
        c = c_ref[...]
        put(pltpu.einshape("thd->t(hd)", c) if c.ndim == 3 else c)

    @pl.when(r == nc)
    def _():
        new = n_ref[...].astype(F32)
        put(jnp.concatenate([new, jnp.zeros((blk - new.shape[0], new.shape[1]), F32)], axis=0))

    @pl.when(r > nc)
    def _():
        o_ref[...] = jnp.zeros(o_ref.shape, o_ref.dtype)


def _cache_keys(cache, new, lkp, feature_major):
    B, P = cache.shape[:2]
    L, W = new.shape[1:]
    blk = CACHE_BLOCK
    assert P % blk == 0 and lkp % blk == 0 and L <= blk and lkp >= P + L
    nc = P // blk
    cblock = (None, blk) + cache.shape[2:]
    cmap = ((lambda b, r: (b, jnp.minimum(r, nc - 1), 0, 0)) if cache.ndim == 4
            else (lambda b, r: (b, jnp.minimum(r, nc - 1), 0)))
    return pl.pallas_call(
        functools.partial(_cache_kernel, nc=nc, feature_major=feature_major),
        grid=(B, lkp // blk),
        in_specs=[pl.BlockSpec(cblock, cmap),
                  pl.BlockSpec((None, L, W), lambda b, r: (b, 0, 0))],
        out_specs=(pl.BlockSpec((None, W, blk), lambda b, r: (b, 0, r)) if feature_major
                   else pl.BlockSpec((None, blk, W), lambda b, r: (b, r, 0))),
        out_shape=jax.ShapeDtypeStruct((B, W, lkp) if feature_major else (B, lkp, W), BF16),
        compiler_params=_cparams(("arbitrary", "arbitrary")),
        name="cache_keys",
    )(cache, new)


def _pad_cols(w, mult):
    n = w.shape[1]
    npad = -(-n // mult) * mult
    return jnp.pad(w, ((0, 0), (0, npad - n)))


def _tiles(B, L, rows):
    if L >= rows:
        assert L % rows == 0
        return 1, rows
    assert rows % L == 0 and B % (rows // L) == 0
    return rows // L, L


def _layer(x, mod, past, wts, *, last, g_final):
    (g_mix, w_in_sb, w_in_sa, w_in_ix, w_gate, w_br_sb, w_br_sa, w_out, rel_table, g_ffn, w_up,
     conv_w, conv_b, w_down) = wts
    B, L, D = x.shape
    sh1, sc1, gt1, sh2, sc2, gt2 = (m[:, None, :] for m in jnp.split(mod, 6, axis=-1))
    past_len = 0 if past is None else past[0].shape[1]

    wn = W_SB
    bt, lt = _tiles(B, L, 512)
    fm = bt == 1
    qlay = "feature" if fm else "row"
    call = functools.partial(_proj, x, sc1, sh1, g_mix, bt=bt, lt=lt)
    q_sb, k_sb, kb_sb, v_sb, vb_sb = call(
        w_in_sb, [_Out(0, 0, wn, BF16), _Out(1, 0, wn, F32, "heads"), _Out(1, 0, wn, BF16),
                  _Out(2, 0, wn, F32, "heads"), _Out(2, 0, wn, BF16)], tn=wn, name="proj_sb")
    q_sa, k_sa, kb_sa, v_sa, vb_sa = call(
        w_in_sa, [_Out(0, 0, wn, BF16, qlay), _Out(1, 0, wn, F32, "heads"), _Out(1, 0, wn, BF16),
                  _Out(2, 0, wn, F32, "heads"), _Out(2, 0, wn, BF16, qlay)],
        tn=wn, name="proj_sa")
    w_ix_out = (_Out(1, 0, LANE, F32, "feature", (D_IDX, D_IDX + H_IDX)) if fm
                else _Out(1, D_IDX, D_IDX + H_IDX, F32))
    q_ix, k_ix, kb_ix, w_ix = call(
        w_in_ix, [_Out(0, 0, wn, BF16, qlay), _Out(1, 0, D_IDX, F32), _Out(1, 0, D_IDX, BF16),
                  w_ix_out], tn=wn, name="proj_ix")
    bt, lt = _tiles(B, L, 1024)
    gate, = _proj(x, sc1, sh1, g_mix, w_gate, [_Out(None, 0, 512, F32)], bt=bt, lt=lt, tn=512,
                  sigmoid=True, name="proj_gate")
    if not fm:
        q_sa, q_ix, w_ix = (jnp.swapaxes(a, 1, 2) for a in (q_sa, q_ix, w_ix))

    tq = min(256, L)
    tk_sb = 256
    tk_sa = 512 if L >= 512 else 256
    l_keys = past_len + L

    def keys_of(new, old, tk, feature_major=False):
        if old is None:
            assert l_keys % tk == 0
            return new
        assert not fm
        return _cache_keys(old, new, -(-l_keys // tk) * tk, feature_major)

    old = (None,) * 5 if past is None else past[:5]
    o_sb = _sb_attention(q_sb, keys_of(kb_sb, old[0], tk_sb), keys_of(vb_sb, old[1], tk_sb),
                         tq=tq, tk=tk_sb, past_len=past_len)
    assert fm or past is not None
    vt = keys_of(vb_sa, old[3], tk_sa, True)
    o_sa_t = _dsa_attention(q_ix, w_ix, q_sa, keys_of(kb_ix, old[4], tk_sa),
                            keys_of(kb_sa, old[2], tk_sa), vt, rel_table,
                            tq=tq, tk=tk_sa, past_len=past_len)
    o_sa = jnp.swapaxes(o_sa_t, 1, 2)

    bt, lt = _tiles(B, L, 256)
    x1 = _merge(o_sb, o_sa, gate, x, gt1, w_br_sb, w_br_sa, w_out, bt=bt, lt=lt)

    prev = jnp.zeros((B, CONV_W - 1, w_down.shape[0]), F32) if past is None else past[5]
    bt, lt = _tiles(B, L, 512)
    x2, conv_state = _ffn(x1, sc2, sh2, gt2, g_ffn, g_final, prev, w_up, conv_w, conv_b,
                          w_down, bt=bt, lt=lt, fc=512, final=last)
    conv_state = jnp.swapaxes(conv_state, 1, 2).reshape(B, CONV_W - 1, -1)
    return x2, (k_sb, v_sb, k_sa, v_sa, k_ix, conv_state)


def kernel(x_prompt, x_sample, cache_sb_k, cache_sb_v, cache_sa_k, cache_sa_v, cache_idx_k,
           state_ffn_conv, c_prompt, c_sample, w_ada, b_ada, g_mix, w_in, w_gate, w_br_sb,
           w_br_sa, w_out, rel_table, g_ffn, w_up, conv_w, conv_b, w_down, g_final):
    depth = w_ada.shape[0]
    nbp = c_prompt.shape[0]
    nbs = c_sample.shape[0]
    rows = -(-(nbp + nbs) // 8) * 8
    c_all = jnp.concatenate([c_prompt, c_sample,
                             jnp.zeros((rows - nbp - nbs, c_prompt.shape[1]), F32)], axis=0)
    xp, xs = x_prompt, x_sample
    new_p, new_s = [], []
    for l in range(depth):
        mod = _adaln(c_all, w_ada[l], b_ada[l][None, :])
        wb = w_in[l].astype(BF16)
        wts = (g_mix[l][None, :], wb[:, :COL_Q_SA], wb[:, COL_Q_SA:COL_Q_IX],
               _pad_cols(wb[:, COL_Q_IX:], 2 * W_SB), w_gate[l].astype(BF16),
               w_br_sb[l].astype(BF16), w_br_sa[l].astype(BF16), w_out[l].astype(BF16),
               rel_table, g_ffn[l][None, :], w_up[l].astype(BF16), conv_w[l],
               conv_b[l][None, :], w_down[l].astype(BF16))
        gfin = g_final[None, :]
        past = (cache_sb_k[l], cache_sb_v[l], cache_sa_k[l], cache_sa_v[l], cache_idx_k[l],
                state_ffn_conv[l])
        xp, sp = _layer(xp, mod[:nbp], None, wts, last=l == depth - 1, g_final=gfin)
        xs, ss = _layer(xs, mod[nbp:nbp + nbs], past, wts, last=l == depth - 1, g_final=gfin)
        new_p.append(sp)
        new_s.append(ss)
    stack = lambda states, n: jnp.stack([s[n] for s in states])
    return ((xp, xs) + tuple(stack(new_p, n) for n in range(6))
            + tuple(stack(new_s, n) for n in range(6)))
```

```python
import functools

import numpy as np
import jax
import jax.numpy as jnp
from jax import lax
from jax.experimental import pallas as pl
from jax.experimental.pallas import tpu as pltpu

F32 = jnp.float32
BF16 = jnp.bfloat16
I32 = jnp.int32

CHUNK = 64
D_HEAD = 128
H_SB = 8
H_SA = 8
W_SB = H_SB * D_HEAD
W_SA = H_SA * D_HEAD
H_IDX = 16
D_IDX = 64
TOPK_MAX = 256
N_BUCKETS = 32
REL_MAX_DIST = 1024
CONV_W = 3
EPS = 1e-6

COL_Q_SB = 0
COL_K_SB = W_SB
COL_V_SB = 2 * W_SB
COL_Q_SA = 3 * W_SB
COL_K_SA = 3 * W_SB + W_SA
COL_V_SA = 3 * W_SB + 2 * W_SA
COL_Q_IX = 3 * W_SB + 3 * W_SA
COL_K_IX = COL_Q_IX + H_IDX * D_IDX
COL_W_IX = COL_K_IX + D_IDX
IN_COLS = COL_W_IX + H_IDX

LANE = 128
VMEM_LIMIT = 56 * 1024 * 1024

ATT_SCALE = D_HEAD ** -0.5
LOG2E = 1.4426950408889634
IDX_SCALE = (D_IDX ** -0.5) * (H_IDX ** -0.5)
NEG = -1e30
INT_MIN = -2 ** 31
SB_DEAD = -104.0


def _cparams(sem):
    return pltpu.CompilerParams(dimension_semantics=sem, vmem_limit_bytes=VMEM_LIMIT)


def _adaln_kernel(c_ref, w_ref, b_ref, o_ref):
    c = c_ref[...]
    a = c * jax.nn.sigmoid(c)
    o_ref[...] = jnp.dot(a.astype(BF16), w_ref[...].astype(BF16),
                         preferred_element_type=F32) + b_ref[...]


def _adaln(c, w, b):
    R, D = c.shape
    N = w.shape[1]
    tn = 1024
    return pl.pallas_call(
        _adaln_kernel,
        grid=(N // tn,),
        in_specs=[pl.BlockSpec((R, D), lambda n: (0, 0)),
                  pl.BlockSpec((D, tn), lambda n: (0, n)),
                  pl.BlockSpec((1, tn), lambda n: (0, n))],
        out_specs=pl.BlockSpec((R, tn), lambda n: (0, n)),
        out_shape=jax.ShapeDtypeStruct((R, N), F32),
        compiler_params=_cparams(("arbitrary",)),
        name="adaln",
    )(c, w, b)


def _norm_mod(x, g, sc, sh):
    ms = jnp.mean(x * x, axis=-1, keepdims=True)
    y = x * lax.rsqrt(ms + EPS) * g
    return y * (1.0 + sc) + sh


class _Out(tuple):
    __slots__ = ()

    def __new__(cls, tile, lo, hi, dtype, layout="row", rows=None, scale=None):
        return tuple.__new__(cls, (tile, lo, hi, rows or (0, hi - lo), layout, dtype, scale))


def _proj_kernel(x_ref, sc_ref, sh_ref, g_ref, w_ref, *rest, plan, sigmoid):
    out_refs, h_ref = rest[:-1], rest[-1]
    bt, lt, D = x_ref.shape
    n = pl.program_id(2)

    @pl.when(n == 0)
    def _():
        h = _norm_mod(x_ref[...], g_ref[...], sc_ref[...], sh_ref[...])
        h_ref[...] = h.reshape(bt * lt, D).astype(BF16)

    r = jnp.dot(h_ref[...], w_ref[...], preferred_element_type=F32)
    if sigmoid:
        r = jax.nn.sigmoid(r)
    for (tile, lo, hi, rows, layout, _, scale), o_ref in zip(plan, out_refs):
        def emit(o_ref=o_ref, lo=lo, hi=hi, rows=rows, layout=layout, scale=scale):
            v = r[:, lo:hi]
            if scale is not None:
                v = v * scale
            if layout == "feature":
                o_ref[...] = v.T[rows[0]:rows[1], :].astype(o_ref.dtype)
            elif layout == "heads":
                nh = (hi - lo) // D_HEAD
                v = pltpu.einshape("t(hd)->thd", v, h=nh)
                o_ref[...] = v.reshape(bt, lt, nh, D_HEAD).astype(o_ref.dtype)
            else:
                o_ref[...] = v.reshape(bt, lt, hi - lo).astype(o_ref.dtype)

        if tile is None:
            emit()
        else:
            pl.when(n == tile)(emit)


def _proj(x, sc, sh, g, w, plan, *, bt, lt, tn, sigmoid=False, name):
    B, L, D = x.shape
    N = w.shape[1]
    out_specs, out_shape = [], []
    for tile, lo, hi, rows, layout, dtype, _ in plan:
        if tile is None:
            out_specs.append(pl.BlockSpec((bt, lt, tn), lambda b, t, n: (b, t, n)))
            out_shape.append(jax.ShapeDtypeStruct((B, L, N), dtype))
        elif layout == "heads":
            nh = (hi - lo) // D_HEAD
            out_specs.append(pl.BlockSpec((bt, lt, nh, D_HEAD), lambda b, t, n: (b, t, 0, 0)))
            out_shape.append(jax.ShapeDtypeStruct((B, L, nh, D_HEAD), dtype))
        elif layout == "feature":
            assert bt == 1
            width = rows[1] - rows[0]
            out_specs.append(pl.BlockSpec((None, width, lt), lambda b, t, n: (b, 0, t)))
            out_shape.append(jax.ShapeDtypeStruct((B, width, L), dtype))
        else:
            out_specs.append(pl.BlockSpec((bt, lt, hi - lo), lambda b, t, n: (b, t, 0)))
            out_shape.append(jax.ShapeDtypeStruct((B, L, hi - lo), dtype))
    return pl.pallas_call(
        functools.partial(_proj_kernel, plan=tuple(plan), sigmoid=sigmoid),
        grid=(B // bt, L // lt, N // tn),
        in_specs=[pl.BlockSpec((bt, lt, D), lambda b, t, n: (b, t, 0)),
                  pl.BlockSpec((bt, 1, D), lambda b, t, n: (b, 0, 0)),
                  pl.BlockSpec((bt, 1, D), lambda b, t, n: (b, 0, 0)),
                  pl.BlockSpec((1, D), lambda b, t, n: (0, 0)),
                  pl.BlockSpec((D, tn), lambda b, t, n: (0, n))],
        out_specs=out_specs,
        out_shape=out_shape,
        scratch_shapes=[pltpu.VMEM((bt * lt, D), BF16)],
        compiler_params=_cparams(("arbitrary", "arbitrary", "arbitrary")),
        name=name,
    )(x, sc, sh, g, w)


def _sb_kernel(q_ref, k_ref, v_ref, u_ref, o_ref, *, tq, tk, past_len, hp):
    i = pl.program_id(2)
    qpos0 = past_len + i * tq
    qpos = qpos0 + lax.broadcasted_iota(I32, (tq, 1), 0)
    j0 = (qpos0 + tq - 2) // tk
    u = u_ref[...]

    def cond(carry):
        j, go, _, _ = carry
        return jnp.logical_and(j >= 0, go)

    def body(carry):
        j, _, cums, accs = carry
        off = pl.multiple_of(j * tk, tk)
        kpos = j * tk + lax.broadcasted_iota(I32, (1, tk), 1)
        mask = kpos < qpos
        new_cums, new_accs = [], []
        for h in range(hp):
            cs = slice(h * D_HEAD, (h + 1) * D_HEAD)
            z = lax.dot_general(q_ref[:, cs], k_ref[pl.ds(off, tk), cs], _NT,
                                preferred_element_type=F32) * ATT_SCALE
            t = jnp.log(1.0 + jnp.exp(-jnp.abs(z)))
            lk = jnp.where(mask, -(jnp.maximum(z, 0.0) + t), 0.0)
            hi = lk.astype(BF16)
            lo = (lk - hi.astype(F32)).astype(BF16)
            after = cums[h] + (jnp.dot(hi, u, preferred_element_type=F32)
                               + jnp.dot(lo, u, preferred_element_type=F32))
            lsz = jnp.minimum(z, 0.0) - t
            a = jnp.where(mask, jnp.exp(lsz + after), 0.0)
            new_accs.append(accs[h] + jnp.dot(a.astype(BF16), v_ref[pl.ds(off, tk), cs],
                                              preferred_element_type=F32))
            new_cums.append(cums[h] + jnp.sum(lk, axis=1, keepdims=True))
        top = new_cums[0]
        for c in new_cums[1:]:
            top = jnp.maximum(top, c)
        return j - 1, jnp.max(top) > SB_DEAD, tuple(new_cums), tuple(new_accs)

    init = (j0, jnp.bool_(True), (jnp.zeros((tq, 1), F32),) * hp,
            (jnp.zeros((tq, D_HEAD), F32),) * hp)
    _, _, _, accs = lax.while_loop(cond, body, init)
    for h in range(hp):
        o_ref[:, h * D_HEAD:(h + 1) * D_HEAD] = accs[h].astype(o_ref.dtype)


def _sb_attention(q, k_all, v_all, *, tq, tk, past_len, hp=2):
    B, L, _ = q.shape
    Lkp = k_all.shape[1]
    r = np.arange(tk)
    u = jnp.asarray((r[:, None] > r[None, :]).astype(np.float32), BF16)
    wp = hp * D_HEAD
    return pl.pallas_call(
        functools.partial(_sb_kernel, tq=tq, tk=tk, past_len=past_len, hp=hp),
        grid=(B, H_SB // hp, L // tq),
        in_specs=[pl.BlockSpec((None, tq, wp), lambda b, h, i: (b, i, h)),
                  pl.BlockSpec((None, Lkp, wp), lambda b, h, i: (b, 0, h)),
                  pl.BlockSpec((None, Lkp, wp), lambda b, h, i: (b, 0, h)),
                  pl.BlockSpec((tk, tk), lambda b, h, i: (0, 0))],
        out_specs=pl.BlockSpec((None, tq, wp), lambda b, h, i: (b, i, h)),
        out_shape=jax.ShapeDtypeStruct((B, L, W_SB), BF16),
        compiler_params=_cparams(("arbitrary", "arbitrary", "arbitrary")),
        name="sb_attn",
    )(q, k_all, v_all, u)


def _bucket_edges():
    nb = N_BUCKETS // 2
    max_exact = nb // 2

    def bucket(rel):
        n = abs(rel)
        if n < max_exact:
            v = n
        else:
            v = max_exact + int(np.log(np.float32(n) / max_exact)
                                / np.log(REL_MAX_DIST / max_exact) * (nb - max_exact))
            v = min(v, nb - 1)
        return (nb if rel > 0 else 0) + v

    lo = -4 * REL_MAX_DIST
    assert bucket(lo) == nb - 1
    edges = []
    prev = bucket(lo)
    for rel in range(lo + 1, CHUNK):
        bk = bucket(rel)
        if bk != prev:
            edges.append((rel, prev))
            prev = bk
    edges.append((CHUNK, prev))
    return edges


_EDGES = _bucket_edges()
_FAR_BUCKET = _EDGES[0][1]
_NEAR_REL = _EDGES[0][0]


def _bias_kernel(d_ref, rel_ref, o_ref):
    tk, tq = o_ref.shape
    h = pl.program_id(1)
    rel = (lax.broadcasted_iota(I32, (tk, tq), 0) - lax.broadcasted_iota(I32, (tk, tq), 1)
           - d_ref[pl.program_id(0)])
    tile = jnp.full((tk, tq), rel_ref[_EDGES[-1][1], h], F32)
    for edge, bk in reversed(_EDGES[:-1]):
        tile = jnp.where(rel < edge, rel_ref[bk, h], tile)
    o_ref[...] = (tile - rel_ref[_FAR_BUCKET, h]) * LOG2E


def _bias_tiles(ds, rel_table, *, tq, tk):
    return pl.pallas_call(
        _bias_kernel,
        grid=(len(ds), H_SA),
        in_specs=[pl.BlockSpec(memory_space=pltpu.SMEM), pl.BlockSpec(memory_space=pltpu.SMEM)],
        out_specs=pl.BlockSpec((None, None, tk, tq), lambda n, h: (n, h, 0, 0)),
        out_shape=jax.ShapeDtypeStruct((len(ds), H_SA, tk, tq), F32),
        compiler_params=_cparams(("arbitrary", "arbitrary")),
        name="bias_tiles",
    )(jnp.asarray(np.asarray(ds, np.int32)), rel_table)


_NT = (((1,), (1,)), ((), ()))


def _dsa_kernel(itab, jtab, ftab, ltab, ntab, nbtab, fartab,
                qit_ref, wt_ref, qt_ref, ki_ref, k_ref, vt_ref, bias_ref, o_ref,
                keys_ref, half_ref, thr_ref, m_ref, l_ref, acc_ref, madd_ref, s_ref, p_ref,
                *, tq, tk, past_len, sub, csub):
    p = pl.program_id(1)
    i = itab[p]
    j = jtab[p]
    qpos = past_len + i * tq + lax.broadcasted_iota(I32, (1, tq), 1)
    lim = (qpos // CHUNK + 1) * CHUNK

    @pl.when(ftab[p] == 1)
    def _():
        wt = wt_ref[...] * IDX_SCALE

        def score_blk(c, carry):
            off = pl.multiple_of(c * sub, sub)
            kib = ki_ref[pl.ds(off, sub), :]
            s = jnp.zeros((sub, tq), F32)
            for h in range(H_IDX):
                sh = jnp.dot(kib, qit_ref[h * D_IDX:(h + 1) * D_IDX, :],
                             preferred_element_type=F32)
                s = s + wt[h:h + 1, :] * jnp.maximum(sh, 0.0)
            bits = pltpu.bitcast(s, I32)
            key = jnp.where(bits < 0, bits ^ jnp.int32(0x7FFFFFFF), bits)
            kpos = c * sub + lax.broadcasted_iota(I32, (sub, 1), 0)
            key = jnp.where(kpos < lim, key, jnp.int32(INT_MIN))
            keys_ref[pl.ds(off, sub), :] = key
            half_ref[pl.ds(off, sub), :] = (key >> 16).astype(jnp.int16)
            return carry

        per = tk // sub

        def score_grp(g, carry):
            for u in range(per):
                score_blk(g * per + u, carry)
            return carry

        lax.fori_loop(0, nbtab[p], score_grp, 0)

        need = jnp.minimum(TOPK_MAX, lim)
        nchunk = nbtab[p] * (tk // csub)
        i16_min = -2 ** 15

        def count_ge(cand):
            c16 = jnp.broadcast_to(cand, (16, tq)).astype(jnp.int16)

            def cnt_blk(c, acc):
                kb = half_ref[pl.ds(pl.multiple_of(c * csub, csub), csub), :]
                parts = [jnp.where(kb[16 * r:16 * (r + 1), :] >= c16, jnp.int16(1), jnp.int16(0))
                         for r in range(csub // 16)]
                while len(parts) > 1:
                    parts = [a + b for a, b in zip(parts[::2], parts[1::2])]
                return acc + parts[0]

            acc = lax.fori_loop(0, nchunk, cnt_blk, jnp.zeros((16, tq), jnp.int16))
            return jnp.sum(acc.astype(I32), axis=0, keepdims=True)

        def kth_half(want):
            def bit_step(it, tpre):
                bitv = lax.shift_left(jnp.int32(1), 15 - it)
                hit = count_ge((tpre | bitv) + i16_min) >= want
                return jnp.where(hit, tpre | bitv, tpre)

            return lax.fori_loop(0, 16, bit_step, jnp.zeros((1, tq), I32))

        hi = kth_half(need) + i16_min
        above = jnp.where(hi == 2 ** 15 - 1, 0, count_ge(jnp.minimum(hi + 1, 2 ** 15 - 1)))

        def low_halves(c, carry):
            rows = pl.ds(pl.multiple_of(c * csub, csub), csub)
            k32 = keys_ref[rows, :]
            lo = (k32 & 0xFFFF) + i16_min
            half_ref[rows, :] = jnp.where((k32 >> 16) == hi, lo, i16_min).astype(jnp.int16)
            return carry

        lax.fori_loop(0, nchunk, low_halves, 0)
        thr_ref[...] = lax.shift_left(hi, 16) | kth_half(need - above)
        m_ref[...] = jnp.full(m_ref.shape, NEG, F32)
        l_ref[...] = jnp.zeros(l_ref.shape, F32)
        acc_ref[...] = jnp.zeros(acc_ref.shape, F32)

    thr = thr_ref[...]
    for c in range(tk // sub):
        rows = pl.ds(pl.multiple_of(j * tk + c * sub, sub), sub)
        madd_ref[c * sub:(c + 1) * sub, :] = jnp.where(keys_ref[rows, :] >= thr, 0.0, NEG)

    def logits(h, near):
        cs = slice(h * D_HEAD, (h + 1) * D_HEAD)
        qh = qt_ref[cs, :]
        mx = jnp.full((8, tq), NEG, F32)
        for c in range(tk // sub):
            rs = slice(c * sub, (c + 1) * sub)
            s = jnp.dot(k_ref[rs, cs], qh, preferred_element_type=F32)
            extra = bias_ref[h, rs, :] + madd_ref[rs, :] if near else madd_ref[rs, :]
            s = s + extra
            s_ref[h % 2, rs, :] = s
            mx = jnp.maximum(mx, jnp.max(s.reshape(sub // 8, 8, tq), axis=0))
        return jnp.max(mx, axis=0, keepdims=True)

    def attend(near):
        smax = logits(0, near)
        for h in range(H_SA):
            cs = slice(h * D_HEAD, (h + 1) * D_HEAD)
            m_prev = m_ref[h:h + 1, :]
            m_new = jnp.maximum(m_prev, smax)
            if h + 1 < H_SA:
                smax = logits(h + 1, near)
            alpha = jnp.exp2(m_prev - m_new)
            for c in range(tk // sub):
                rs = slice(c * sub, (c + 1) * sub)
                p_ref[h % 2, rs, :] = jnp.exp2(s_ref[h % 2, rs, :] - m_new).astype(BF16)
            vt_ones = jnp.concatenate([vt_ref[cs, :], jnp.ones((16, tk), BF16)], axis=0)
            pv = jnp.dot(vt_ones, p_ref[h % 2], preferred_element_type=F32)
            l_ref[h:h + 1, :] = alpha * l_ref[h:h + 1, :] + pv[D_HEAD:D_HEAD + 1, :]
            acc_ref[cs, :] = alpha * acc_ref[cs, :] + pv[:D_HEAD, :]
            m_ref[h:h + 1, :] = m_new

    pl.when(fartab[p] == 0)(functools.partial(attend, True))
    pl.when(fartab[p] == 1)(functools.partial(attend, False))

    @pl.when(ltab[p] == 1)
    def _():
        for h in range(H_SA):
            cs = slice(h * D_HEAD, (h + 1) * D_HEAD)
            o_ref[cs, :] = (acc_ref[cs, :] / l_ref[h:h + 1, :]).astype(o_ref.dtype)


def _dsa_attention(qit, wt, qt, ki_all, k_all, vt_all, rel_table, *, tq, tk, past_len):
    B, _, L = qt.shape
    Lkp = k_all.shape[1]
    assert Lkp % tk == 0 and L % tq == 0
    nq = L // tq
    it, jt, ft, lt_, nt, nbt, fart = [], [], [], [], [], [], []
    near_ds = []
    for i in range(nq):
        qpos0 = past_len + i * tq
        lim_max = ((qpos0 + tq - 1) // CHUNK + 1) * CHUNK
        jmax = (lim_max - 1) // tk
        for j in range(jmax + 1):
            d = qpos0 - j * tk
            if (tk - 1) - d >= _NEAR_REL:
                if d not in near_ds:
                    near_ds.append(d)
                nt.append(near_ds.index(d))
                fart.append(0)
            else:
                nt.append(nt[-1] if nt else 0)
                fart.append(1)
            it.append(i); jt.append(j); ft.append(int(j == 0)); lt_.append(int(j == jmax))
            nbt.append(jmax + 1)
    bias = _bias_tiles(near_ds, rel_table, tq=tq, tk=tk)
    tabs = [jnp.asarray(np.asarray(t, np.int32)) for t in (it, jt, ft, lt_, nt, nbt, fart)]
    grid_spec = pltpu.PrefetchScalarGridSpec(
        num_scalar_prefetch=7,
        grid=(B, len(it)),
        in_specs=[
            pl.BlockSpec((None, H_IDX * D_IDX, tq), lambda b, p, it, jt, *_: (b, 0, it[p])),
            pl.BlockSpec((None, H_IDX, tq), lambda b, p, it, jt, *_: (b, 0, it[p])),
            pl.BlockSpec((None, W_SA, tq), lambda b, p, it, jt, *_: (b, 0, it[p])),
            pl.BlockSpec((None, Lkp, D_IDX), lambda b, p, it, jt, *_: (b, 0, 0),
                         pipeline_mode=pl.Buffered(1)),
            pl.BlockSpec((None, tk, W_SA), lambda b, p, it, jt, *_: (b, jt[p], 0)),
            pl.BlockSpec((None, W_SA, tk), lambda b, p, it, jt, *_: (b, 0, jt[p])),
            pl.BlockSpec((None, H_SA, tk, tq),
                         lambda b, p, it, jt, ft, lt, nt, *_: (nt[p], 0, 0, 0)),
        ],
        out_specs=pl.BlockSpec((None, W_SA, tq), lambda b, p, it, jt, *_: (b, 0, it[p])),
        scratch_shapes=[
            pltpu.VMEM((Lkp, tq), I32),
            pltpu.VMEM((Lkp, tq), jnp.int16),
            pltpu.VMEM((1, tq), I32),
            pltpu.VMEM((H_SA, tq), F32),
            pltpu.VMEM((H_SA, tq), F32),
            pltpu.VMEM((W_SA, tq), F32),
            pltpu.VMEM((tk, tq), F32),
            pltpu.VMEM((2, tk, tq), F32),
            pltpu.VMEM((2, tk, tq), BF16),
        ],
    )
    return pl.pallas_call(
        functools.partial(_dsa_kernel, tq=tq, tk=tk, past_len=past_len,
                          sub=min(tk, 128), csub=min(tk, 512)),
        grid_spec=grid_spec,
        out_shape=jax.ShapeDtypeStruct((B, W_SA, L), BF16),
        compiler_params=_cparams(("arbitrary", "arbitrary")),
        name="dsa_attn",
    )(*tabs, qit, wt, qt, ki_all, k_all, vt_all, bias)


def _merge_kernel(osb_ref, osa_ref, g_ref, x_ref, gt_ref, wsb_ref, wsa_ref, wo_ref, o_ref):
    bt, lt, D = x_ref.shape
    tm = bt * lt
    a = jnp.dot(osb_ref[...].reshape(tm, W_SB), wsb_ref[...], preferred_element_type=F32)
    c = jnp.dot(osa_ref[...].reshape(tm, W_SA), wsa_ref[...], preferred_element_type=F32)
    g = g_ref[...].reshape(tm, 2 * D)
    merged = g[:, :D] * a + g[:, D:] * c
    y = jnp.dot(merged.astype(BF16), wo_ref[...], preferred_element_type=F32)
    o_ref[...] = x_ref[...] + gt_ref[...] * y.reshape(bt, lt, D)


def _merge(osb, osa, g, x, gt, wsb, wsa, wo, *, bt, lt):
    B, L, D = x.shape
    row = lambda w: pl.BlockSpec((bt, lt, w), lambda b, t: (b, t, 0))
    full = lambda a: pl.BlockSpec(a.shape, lambda b, t: (0, 0), pipeline_mode=pl.Buffered(1))
    return pl.pallas_call(
        _merge_kernel,
        grid=(B // bt, L // lt),
        in_specs=[row(W_SB), row(W_SA), row(2 * D), row(D),
                  pl.BlockSpec((bt, 1, D), lambda b, t: (b, 0, 0)),
                  full(wsb), full(wsa), full(wo)],
        out_specs=row(D),
        out_shape=jax.ShapeDtypeStruct((B, L, D), F32),
        compiler_params=_cparams(("arbitrary", "arbitrary")),
        name="merge_out",
    )(osb, osa, g, x, gt, wsb, wsa, wo)


HALO = 8


def _ffn_kernel(x_ref, sc_ref, sh_ref, gt_ref, g_ref, gfin_ref, prev_ref, wg_ref, wv_ref,
                cw_ref, cb_ref, wd_ref, x2_ref, st_ref,
                h_ref, acc_ref, ext_ref, carry_ref, *, final):
    bt, lt, D = x_ref.shape
    tm = bt * lt
    fc = wg_ref.shape[1]
    t = pl.program_id(1)
    f = pl.program_id(2)

    @pl.when(f == 0)
    def _():
        h = _norm_mod(x_ref[...], g_ref[...], sc_ref[...], sh_ref[...])
        h_ref[...] = h.reshape(tm, D).astype(BF16)
        acc_ref[...] = jnp.zeros(acc_ref.shape, F32)

    h = h_ref[...]
    ug = jnp.dot(h, wg_ref[...], preferred_element_type=F32).reshape(bt, lt, fc)
    uv = jnp.dot(h, wv_ref[...], preferred_element_type=F32).reshape(bt, lt, fc)

    @pl.when(t == 0)
    def _():
        ext_ref[:, HALO - 2:HALO, :] = prev_ref[...]

    @pl.when(t > 0)
    def _():
        ext_ref[:, HALO - 2:HALO, :] = carry_ref[f]

    ext_ref[:, HALO:HALO + lt, :] = ug
    tail = ug[:, lt - 2:lt, :]
    carry_ref[f] = tail
    st_ref[:, f] = tail
    cw = cw_ref[...]
    conv = (cb_ref[...] + cw[0:1, :] * ext_ref[:, HALO - 2:HALO - 2 + lt, :]
            + cw[1:2, :] * ext_ref[:, HALO - 1:HALO - 1 + lt, :] + cw[2:3, :] * ug)
    act = conv * jax.nn.sigmoid(conv) * uv
    acc_ref[...] += jnp.dot(act.reshape(tm, fc).astype(BF16), wd_ref[...],
                            preferred_element_type=F32)

    @pl.when(f == pl.num_programs(2) - 1)
    def _():
        x2 = x_ref[...] + gt_ref[...] * acc_ref[...].reshape(bt, lt, D)
        if final:
            ms = jnp.mean(x2 * x2, axis=-1, keepdims=True)
            x2 = x2 * lax.rsqrt(ms + EPS) * gfin_ref[...]
        x2_ref[...] = x2


def _ffn(x, sc, sh, gt, g, gfin, prev, w_up, cw, cb, w_down, *, bt, lt, fc, final):
    B, L, D = x.shape
    F = w_down.shape[0]
    nf = F // fc
    row = pl.BlockSpec((bt, lt, D), lambda b, t, f: (b, t, 0))
    mod = pl.BlockSpec((bt, 1, D), lambda b, t, f: (b, 0, 0))
    vec = pl.BlockSpec((1, D), lambda b, t, f: (0, 0))
    return pl.pallas_call(
        functools.partial(_ffn_kernel, final=final),
        grid=(B // bt, L // lt, nf),
        in_specs=[row, mod, mod, mod, vec, vec,
                  pl.BlockSpec((bt, CONV_W - 1, fc), lambda b, t, f: (b, 0, f)),
                  pl.BlockSpec((D, fc), lambda b, t, f: (0, f)),
                  pl.BlockSpec((D, fc), lambda b, t, f: (0, nf + f)),
                  pl.BlockSpec((CONV_W, fc), lambda b, t, f: (0, f)),
                  pl.BlockSpec((1, fc), lambda b, t, f: (0, f)),
                  pl.BlockSpec((fc, D), lambda b, t, f: (f, 0))],
        out_specs=[row, pl.BlockSpec((bt, nf, CONV_W - 1, fc), lambda b, t, f: (b, 0, 0, 0))],
        out_shape=[jax.ShapeDtypeStruct((B, L, D), F32),
                   jax.ShapeDtypeStruct((B, nf, CONV_W - 1, fc), F32)],
        scratch_shapes=[pltpu.VMEM((bt * lt, D), BF16),
                        pltpu.VMEM((bt * lt, D), F32),
                        pltpu.VMEM((bt, HALO + lt, fc), F32),
                        pltpu.VMEM((nf, bt, CONV_W - 1, fc), F32)],
        compiler_params=_cparams(("arbitrary", "arbitrary", "arbitrary")),
        name="conv_ffn",
    )(x, sc, sh, gt, g, gfin, prev, w_up, w_up, cw, cb, w_down)


CACHE_BLOCK = 256


def _cache_kernel(c_ref, n_ref, o_ref, *, feature_major):
    blk = CACHE_BLOCK
    P = c_ref.shape[0]
    lkp = o_ref.shape[1] if feature_major else o_ref.shape[0]

    def put(r, v):
        rows = slice(r * blk, (r + 1) * blk)
        if feature_major:
            o_ref[:, rows] = v.T.astype(o_ref.dtype)
        else:
            o_ref[rows, :] = v.astype(o_ref.dtype)

    for r in range(P // blk):
        c = c_ref[r * blk:(r + 1) * blk]
        put(r, pltpu.einshape("thd->t(hd)", c) if c.ndim == 3 else c)
    new = n_ref[...].astype(F32)
    put(P // blk, jnp.concatenate(
        [new, jnp.zeros((blk - new.shape[0], new.shape[1]), F32)], axis=0))
    for r in range(P // blk + 1, lkp // blk):
        put(r, jnp.zeros((blk, new.shape[1]), F32))


def _cache_keys(cache, new, lkp, feature_major):
    B, P = cache.shape[:2]
    L, W = new.shape[1:]
    blk = CACHE_BLOCK
    assert P % blk == 0 and lkp % blk == 0 and L <= blk and lkp >= P + L
    cblock = (None,) + cache.shape[1:]
    cmap = (lambda b: (b, 0, 0, 0)) if cache.ndim == 4 else (lambda b: (b, 0, 0))
    oshape = (W, lkp) if feature_major else (lkp, W)
    return pl.pallas_call(
        functools.partial(_cache_kernel, feature_major=feature_major),
        grid=(B,),
        in_specs=[pl.BlockSpec(cblock, cmap),
                  pl.BlockSpec((None, L, W), lambda b: (b, 0, 0))],
        out_specs=pl.BlockSpec((None,) + oshape, lambda b: (b, 0, 0)),
        out_shape=jax.ShapeDtypeStruct((B,) + oshape, BF16),
        compiler_params=_cparams(("arbitrary",)),
        name="cache_keys",
    )(cache, new)


def _pad_cols(w, mult):
    n = w.shape[1]
    npad = -(-n // mult) * mult
    return jnp.pad(w, ((0, 0), (0, npad - n)))


def _tiles(B, L, rows):
    if L >= rows:
        assert L % rows == 0
        return 1, rows
    assert rows % L == 0 and B % (rows // L) == 0
    return rows // L, L


def _layer(x, mod, past, wts, *, last, g_final):
    (g_mix, w_in_sb, w_in_sa, w_in_ix, w_gate, w_br_sb, w_br_sa, w_out, rel_table, g_ffn, w_up,
     conv_w, conv_b, w_down) = wts
    B, L, D = x.shape
    sh1, sc1, gt1, sh2, sc2, gt2 = (m[:, None, :] for m in jnp.split(mod, 6, axis=-1))
    past_len = 0 if past is None else past[0].shape[1]

    wn = W_SB
    bt, lt = _tiles(B, L, 512)
    fm = bt == 1
    qlay = "feature" if fm else "row"
    call = functools.partial(_proj, x, sc1, sh1, g_mix, bt=bt, lt=lt)
    q_sb, k_sb, kb_sb, v_sb, vb_sb = call(
        w_in_sb, [_Out(0, 0, wn, BF16), _Out(1, 0, wn, F32, "heads"), _Out(1, 0, wn, BF16),
                  _Out(2, 0, wn, F32, "heads"), _Out(2, 0, wn, BF16)], tn=wn, name="proj_sb")
    q_sa, k_sa, kb_sa, v_sa, vb_sa = call(
        w_in_sa, [_Out(0, 0, wn, BF16, qlay, scale=ATT_SCALE * LOG2E),
                  _Out(1, 0, wn, F32, "heads"), _Out(1, 0, wn, BF16),
                  _Out(2, 0, wn, F32, "heads"), _Out(2, 0, wn, BF16, qlay)],
        tn=wn, name="proj_sa")
    w_ix_out = (_Out(1, 0, LANE, F32, "feature", (D_IDX, D_IDX + H_IDX)) if fm
                else _Out(1, D_IDX, D_IDX + H_IDX, F32))
    q_ix, k_ix, kb_ix, w_ix = call(
        w_in_ix, [_Out(0, 0, wn, BF16, qlay), _Out(1, 0, D_IDX, F32), _Out(1, 0, D_IDX, BF16),
                  w_ix_out], tn=wn, name="proj_ix")
    bt, lt = _tiles(B, L, 1024)
    gate, = _proj(x, sc1, sh1, g_mix, w_gate, [_Out(None, 0, 512, F32)], bt=bt, lt=lt, tn=512,
                  sigmoid=True, name="proj_gate")
    if not fm:
        q_sa, q_ix, w_ix = (jnp.swapaxes(a, 1, 2) for a in (q_sa, q_ix, w_ix))

    tq = min(256, L)
    tk_sb = 256
    tk_sa = 512 if L >= 512 else 256
    l_keys = past_len + L

    def keys_of(new, old, tk, feature_major=False):
        if old is None:
            assert l_keys % tk == 0
            return new
        assert not fm
        return _cache_keys(old, new, -(-l_keys // tk) * tk, feature_major)

    old = (None,) * 5 if past is None else past[:5]
    o_sb = _sb_attention(q_sb, keys_of(kb_sb, old[0], tk_sb), keys_of(vb_sb, old[1], tk_sb),
                         tq=tq, tk=tk_sb, past_len=past_len)
    assert fm or past is not None
    vt = keys_of(vb_sa, old[3], tk_sa, True)
    o_sa_t = _dsa_attention(q_ix, w_ix, q_sa, keys_of(kb_ix, old[4], tk_sa),
                            keys_of(kb_sa, old[2], tk_sa), vt, rel_table,
                            tq=tq, tk=tk_sa, past_len=past_len)
    o_sa = jnp.swapaxes(o_sa_t, 1, 2)

    bt, lt = _tiles(B, L, 256)
    x1 = _merge(o_sb, o_sa, gate, x, gt1, w_br_sb, w_br_sa, w_out, bt=bt, lt=lt)

    prev = jnp.zeros((B, CONV_W - 1, w_down.shape[0]), F32) if past is None else past[5]
    bt, lt = _tiles(B, L, 512)
    x2, conv_state = _ffn(x1, sc2, sh2, gt2, g_ffn, g_final, prev, w_up, conv_w, conv_b,
                          w_down, bt=bt, lt=lt, fc=512, final=last)
    conv_state = jnp.swapaxes(conv_state, 1, 2).reshape(B, CONV_W - 1, -1)
    return x2, (k_sb, v_sb, k_sa, v_sa, k_ix, conv_state)


def kernel(x_prompt, x_sample, cache_sb_k, cache_sb_v, cache_sa_k, cache_sa_v, cache_idx_k,
           state_ffn_conv, c_prompt, c_sample, w_ada, b_ada, g_mix, w_in, w_gate, w_br_sb,
           w_br_sa, w_out, rel_table, g_ffn, w_up, conv_w, conv_b, w_down, g_final):
    depth = w_ada.shape[0]
    nbp = c_prompt.shape[0]
    nbs = c_sample.shape[0]
    rows = -(-(nbp + nbs) // 8) * 8
    c_all = jnp.concatenate([c_prompt, c_sample,
                             jnp.zeros((rows - nbp - nbs, c_prompt.shape[1]), F32)], axis=0)
    xp, xs = x_prompt, x_sample
    new_p, new_s = [], []
    for l in range(depth):
        mod = _adaln(c_all, w_ada[l], b_ada[l][None, :])
        wb = w_in[l].astype(BF16)
        wts = (g_mix[l][None, :], wb[:, :COL_Q_SA], wb[:, COL_Q_SA:COL_Q_IX],
               _pad_cols(wb[:, COL_Q_IX:], 2 * W_SB), w_gate[l].astype(BF16),
               w_br_sb[l].astype(BF16), w_br_sa[l].astype(BF16), w_out[l].astype(BF16),
               rel_table, g_ffn[l][None, :], w_up[l].astype(BF16), conv_w[l],
               conv_b[l][None, :], w_down[l].astype(BF16))
        gfin = g_final[None, :]
        past = (cache_sb_k[l], cache_sb_v[l], cache_sa_k[l], cache_sa_v[l], cache_idx_k[l],
                state_ffn_conv[l])
        xp, sp = _layer(xp, mod[:nbp], None, wts, last=l == depth - 1, g_final=gfin)
        xs, ss = _layer(xs, mod[nbp:nbp + nbs], past, wts, last=l == depth - 1, g_final=gfin)
        new_p.append(sp)
        new_s.append(ss)
    stack = lambda states, n: jnp.stack([s[n] for s in states])
    return ((xp, xs) + tuple(stack(new_p, n) for n in range(6))
            + tuple(stack(new_s, n) for n in range(6)))
```

```python
import functools

import numpy as np
import jax
import jax.numpy as jnp
from jax import lax
from jax.experimental import pallas as pl
from jax.experimental.pallas import tpu as pltpu

F32 = jnp.float32
BF16 = jnp.bfloat16
I32 = jnp.int32

CHUNK = 64
D_HEAD = 128
H_SB = 8
H_SA = 8
W_SB = H_SB * D_HEAD
W_SA = H_SA * D_HEAD
H_IDX = 16
D_IDX = 64
TOPK_MAX = 256
N_BUCKETS = 32
REL_MAX_DIST = 1024
CONV_W = 3
EPS = 1e-6

COL_Q_SB = 0
COL_K_SB = W_SB
COL_V_SB = 2 * W_SB
COL_Q_SA = 3 * W_SB
COL_K_SA = 3 * W_SB + W_SA
COL_V_SA = 3 * W_SB + 2 * W_SA
COL_Q_IX = 3 * W_SB + 3 * W_SA
COL_K_IX = COL_Q_IX + H_IDX * D_IDX
COL_W_IX = COL_K_IX + D_IDX
IN_COLS = COL_W_IX + H_IDX

LANE = 128
VMEM_LIMIT = 56 * 1024 * 1024

ATT_SCALE = D_HEAD ** -0.5
LOG2E = 1.4426950408889634
IDX_SCALE = (D_IDX ** -0.5) * (H_IDX ** -0.5)
NEG = -1e30
INT_MIN = -2 ** 31
SB_DEAD = -104.0


def _cparams(sem):
    return pltpu.CompilerParams(dimension_semantics=sem, vmem_limit_bytes=VMEM_LIMIT)


def _adaln_kernel(c_ref, w_ref, b_ref, o_ref):
    c = c_ref[...]
    a = c * jax.nn.sigmoid(c)
    o_ref[...] = jnp.dot(a.astype(BF16), w_ref[...].astype(BF16),
                         preferred_element_type=F32) + b_ref[...]


def _adaln(c, w, b):
    R, D = c.shape
    N = w.shape[1]
    tn = 1024
    return pl.pallas_call(
        _adaln_kernel,
        grid=(N // tn,),
        in_specs=[pl.BlockSpec((R, D), lambda n: (0, 0)),
                  pl.BlockSpec((D, tn), lambda n: (0, n)),
                  pl.BlockSpec((1, tn), lambda n: (0, n))],
        out_specs=pl.BlockSpec((R, tn), lambda n: (0, n)),
        out_shape=jax.ShapeDtypeStruct((R, N), F32),
        compiler_params=_cparams(("arbitrary",)),
        name="adaln",
    )(c, w, b)


def _norm_mod(x, g, sc, sh):
    ms = jnp.mean(x * x, axis=-1, keepdims=True)
    y = x * lax.rsqrt(ms + EPS) * g
    return y * (1.0 + sc) + sh


class _Out(tuple):
    __slots__ = ()

    def __new__(cls, tile, lo, hi, dtype, layout="row", rows=None, scale=None):
        return tuple.__new__(cls, (tile, lo, hi, rows or (0, hi - lo), layout, dtype, scale))


def _proj_kernel(x_ref, sc_ref, sh_ref, g_ref, w_ref, *rest, plan, sigmoid):
    out_refs, h_ref = rest[:-1], rest[-1]
    bt, lt, D = x_ref.shape
    n = pl.program_id(2)

    @pl.when(n == 0)
    def _():
        h = _norm_mod(x_ref[...], g_ref[...], sc_ref[...], sh_ref[...])
        h_ref[...] = h.reshape(bt * lt, D).astype(BF16)

    r = jnp.dot(h_ref[...], w_ref[...], preferred_element_type=F32)
    if sigmoid:
        r = jax.nn.sigmoid(r)
    for (tile, lo, hi, rows, layout, _, scale), o_ref in zip(plan, out_refs):
        def emit(o_ref=o_ref, lo=lo, hi=hi, rows=rows, layout=layout, scale=scale):
            v = r[:, lo:hi]
            if scale is not None:
                v = v * scale
            if layout == "feature":
                o_ref[...] = v.T[rows[0]:rows[1], :].astype(o_ref.dtype)
            elif layout == "heads":
                nh = (hi - lo) // D_HEAD
                v = pltpu.einshape("t(hd)->thd", v, h=nh)
                o_ref[...] = v.reshape(bt, lt, nh, D_HEAD).astype(o_ref.dtype)
            else:
                o_ref[...] = v.reshape(bt, lt, hi - lo).astype(o_ref.dtype)

        if tile is None:
            emit()
        else:
            pl.when(n == tile)(emit)


def _proj(x, sc, sh, g, w, plan, *, bt, lt, tn, sigmoid=False, name):
    B, L, D = x.shape
    N = w.shape[1]
    out_specs, out_shape = [], []
    for tile, lo, hi, rows, layout, dtype, _ in plan:
        if tile is None:
            out_specs.append(pl.BlockSpec((bt, lt, tn), lambda b, t, n: (b, t, n)))
            out_shape.append(jax.ShapeDtypeStruct((B, L, N), dtype))
        elif layout == "heads":
            nh = (hi - lo) // D_HEAD
            out_specs.append(pl.BlockSpec((bt, lt, nh, D_HEAD), lambda b, t, n: (b, t, 0, 0)))
            out_shape.append(jax.ShapeDtypeStruct((B, L, nh, D_HEAD), dtype))
        elif layout == "feature":
            assert bt == 1
            width = rows[1] - rows[0]
            out_specs.append(pl.BlockSpec((None, width, lt), lambda b, t, n: (b, 0, t)))
            out_shape.append(jax.ShapeDtypeStruct((B, width, L), dtype))
        else:
            out_specs.append(pl.BlockSpec((bt, lt, hi - lo), lambda b, t, n: (b, t, 0)))
            out_shape.append(jax.ShapeDtypeStruct((B, L, hi - lo), dtype))
    return pl.pallas_call(
        functools.partial(_proj_kernel, plan=tuple(plan), sigmoid=sigmoid),
        grid=(B // bt, L // lt, N // tn),
        in_specs=[pl.BlockSpec((bt, lt, D), lambda b, t, n: (b, t, 0)),
                  pl.BlockSpec((bt, 1, D), lambda b, t, n: (b, 0, 0)),
                  pl.BlockSpec((bt, 1, D), lambda b, t, n: (b, 0, 0)),
                  pl.BlockSpec((1, D), lambda b, t, n: (0, 0)),
                  pl.BlockSpec((D, tn), lambda b, t, n: (0, n))],
        out_specs=out_specs,
        out_shape=out_shape,
        scratch_shapes=[pltpu.VMEM((bt * lt, D), BF16)],
        compiler_params=_cparams(("arbitrary", "arbitrary", "arbitrary")),
        name=name,
    )(x, sc, sh, g, w)


def _sb_kernel(q_ref, k_ref, v_ref, u_ref, o_ref, *, tq, tk, past_len, hp):
    i = pl.program_id(2)
    qpos0 = past_len + i * tq
    qpos = qpos0 + lax.broadcasted_iota(I32, (tq, 1), 0)
    j0 = (qpos0 + tq - 2) // tk
    u = u_ref[...]

    def cond(carry):
        j, go, _, _ = carry
        return jnp.logical_and(j >= 0, go)

    def body(carry):
        j, _, cums, accs = carry
        off = pl.multiple_of(j * tk, tk)
        kpos = j * tk + lax.broadcasted_iota(I32, (1, tk), 1)
        mask = kpos < qpos
        new_cums, new_accs = [], []
        for h in range(hp):
            cs = slice(h * D_HEAD, (h + 1) * D_HEAD)
            z = lax.dot_general(q_ref[:, cs], k_ref[pl.ds(off, tk), cs], _NT,
                                preferred_element_type=F32) * ATT_SCALE
            t = jnp.log(1.0 + jnp.exp(-jnp.abs(z)))
            lk = jnp.where(mask, -(jnp.maximum(z, 0.0) + t), 0.0)
            hi = lk.astype(BF16)
            lo = (lk - hi.astype(F32)).astype(BF16)
            after = cums[h] + (jnp.dot(hi, u, preferred_element_type=F32)
                               + jnp.dot(lo, u, preferred_element_type=F32))
            lsz = jnp.minimum(z, 0.0) - t
            a = jnp.where(mask, jnp.exp(lsz + after), 0.0)
            new_accs.append(accs[h] + jnp.dot(a.astype(BF16), v_ref[pl.ds(off, tk), cs],
                                              preferred_element_type=F32))
            new_cums.append(cums[h] + jnp.sum(lk, axis=1, keepdims=True))
        top = new_cums[0]
        for c in new_cums[1:]:
            top = jnp.maximum(top, c)
        return j - 1, jnp.max(top) > SB_DEAD, tuple(new_cums), tuple(new_accs)

    init = (j0, jnp.bool_(True), (jnp.zeros((tq, 1), F32),) * hp,
            (jnp.zeros((tq, D_HEAD), F32),) * hp)
    _, _, _, accs = lax.while_loop(cond, body, init)
    for h in range(hp):
        o_ref[:, h * D_HEAD:(h + 1) * D_HEAD] = accs[h].astype(o_ref.dtype)


def _sb_attention(q, k_all, v_all, *, tq, tk, past_len, hp=4):
    B, L, _ = q.shape
    Lkp = k_all.shape[1]
    r = np.arange(tk)
    u = jnp.asarray((r[:, None] > r[None, :]).astype(np.float32), BF16)
    wp = hp * D_HEAD
    return pl.pallas_call(
        functools.partial(_sb_kernel, tq=tq, tk=tk, past_len=past_len, hp=hp),
        grid=(B, H_SB // hp, L // tq),
        in_specs=[pl.BlockSpec((None, tq, wp), lambda b, h, i: (b, i, h)),
                  pl.BlockSpec((None, Lkp, wp), lambda b, h, i: (b, 0, h),
                               pipeline_mode=pl.Buffered(1)),
                  pl.BlockSpec((None, Lkp, wp), lambda b, h, i: (b, 0, h),
                               pipeline_mode=pl.Buffered(1)),
                  pl.BlockSpec((tk, tk), lambda b, h, i: (0, 0))],
        out_specs=pl.BlockSpec((None, tq, wp), lambda b, h, i: (b, i, h)),
        out_shape=jax.ShapeDtypeStruct((B, L, W_SB), BF16),
        compiler_params=_cparams(("arbitrary", "arbitrary", "arbitrary")),
        name="sb_attn",
    )(q, k_all, v_all, u)


def _bucket_edges():
    nb = N_BUCKETS // 2
    max_exact = nb // 2

    def bucket(rel):
        n = abs(rel)
        if n < max_exact:
            v = n
        else:
            v = max_exact + int(np.log(np.float32(n) / max_exact)
                                / np.log(REL_MAX_DIST / max_exact) * (nb - max_exact))
            v = min(v, nb - 1)
        return (nb if rel > 0 else 0) + v

    lo = -4 * REL_MAX_DIST
    assert bucket(lo) == nb - 1
    edges = []
    prev = bucket(lo)
    for rel in range(lo + 1, CHUNK):
        bk = bucket(rel)
        if bk != prev:
            edges.append((rel, prev))
            prev = bk
    edges.append((CHUNK, prev))
    return edges


_EDGES = _bucket_edges()
_FAR_BUCKET = _EDGES[0][1]
_NEAR_REL = _EDGES[0][0]


def _bias_kernel(d_ref, rel_ref, o_ref):
    tk, tq = o_ref.shape
    h = pl.program_id(1)
    rel = (lax.broadcasted_iota(I32, (tk, tq), 0) - lax.broadcasted_iota(I32, (tk, tq), 1)
           - d_ref[pl.program_id(0)])
    tile = jnp.full((tk, tq), rel_ref[_EDGES[-1][1], h], F32)
    for edge, bk in reversed(_EDGES[:-1]):
        tile = jnp.where(rel < edge, rel_ref[bk, h], tile)
    o_ref[...] = (tile - rel_ref[_FAR_BUCKET, h]) * LOG2E


def _bias_tiles(ds, rel_table, *, tq, tk):
    return pl.pallas_call(
        _bias_kernel,
        grid=(len(ds), H_SA),
        in_specs=[pl.BlockSpec(memory_space=pltpu.SMEM), pl.BlockSpec(memory_space=pltpu.SMEM)],
        out_specs=pl.BlockSpec((None, None, tk, tq), lambda n, h: (n, h, 0, 0)),
        out_shape=jax.ShapeDtypeStruct((len(ds), H_SA, tk, tq), F32),
        compiler_params=_cparams(("arbitrary", "arbitrary")),
        name="bias_tiles",
    )(jnp.asarray(np.asarray(ds, np.int32)), rel_table)


_NT = (((1,), (1,)), ((), ()))


def _dsa_kernel(itab, jtab, ftab, ltab, ntab, nbtab, fartab,
                qit_ref, wt_ref, qt_ref, ki_ref, k_ref, vt_ref, bias_ref, o_ref,
                keys_ref, half_ref, thr_ref, m_ref, l_ref, acc_ref, madd_ref, s_ref, p_ref,
                *, tq, tk, past_len, sub, csub):
    p = pl.program_id(1)
    i = itab[p]
    j = jtab[p]
    qpos = past_len + i * tq + lax.broadcasted_iota(I32, (1, tq), 1)
    lim = (qpos // CHUNK + 1) * CHUNK

    @pl.when(ftab[p] == 1)
    def _():
        wt = wt_ref[...] * IDX_SCALE

        def score_blk(c, carry):
            off = pl.multiple_of(c * sub, sub)
            kib = ki_ref[pl.ds(off, sub), :]
            s = jnp.zeros((sub, tq), F32)
            for h in range(H_IDX):
                sh = jnp.dot(kib, qit_ref[h * D_IDX:(h + 1) * D_IDX, :],
                             preferred_element_type=F32)
                s = s + wt[h:h + 1, :] * jnp.maximum(sh, 0.0)
            bits = pltpu.bitcast(s, I32)
            key = jnp.where(bits < 0, bits ^ jnp.int32(0x7FFFFFFF), bits)
            kpos = c * sub + lax.broadcasted_iota(I32, (sub, 1), 0)
            key = jnp.where(kpos < lim, key, jnp.int32(INT_MIN))
            keys_ref[pl.ds(off, sub), :] = key
            half_ref[pl.ds(off, sub), :] = (key >> 16).astype(jnp.int16)
            return carry

        per = tk // sub

        def score_grp(g, carry):
            for u in range(per):
                score_blk(g * per + u, carry)
            return carry

        lax.fori_loop(0, nbtab[p], score_grp, 0)

        need = jnp.minimum(TOPK_MAX, lim)
        nchunk = nbtab[p] * (tk // csub)
        i16_min = -2 ** 15

        def count_ge(cand):
            c16 = jnp.broadcast_to(cand, (16, tq)).astype(jnp.int16)

            def cnt_blk(c, acc):
                kb = half_ref[pl.ds(pl.multiple_of(c * csub, csub), csub), :]
                parts = [jnp.where(kb[16 * r:16 * (r + 1), :] >= c16, jnp.int16(1), jnp.int16(0))
                         for r in range(csub // 16)]
                while len(parts) > 1:
                    parts = [a + b for a, b in zip(parts[::2], parts[1::2])]
                return acc + parts[0]

            acc = lax.fori_loop(0, nchunk, cnt_blk, jnp.zeros((16, tq), jnp.int16))
            return jnp.sum(acc.astype(I32), axis=0, keepdims=True)

        def kth_half(want):
            def bit_step(it, tpre):
                bitv = lax.shift_left(jnp.int32(1), 15 - it)
                hit = count_ge((tpre | bitv) + i16_min) >= want
                return jnp.where(hit, tpre | bitv, tpre)

            return lax.fori_loop(0, 16, bit_step, jnp.zeros((1, tq), I32))

        hi = kth_half(need) + i16_min
        above = jnp.where(hi == 2 ** 15 - 1, 0, count_ge(jnp.minimum(hi + 1, 2 ** 15 - 1)))

        def low_halves(c, carry):
            rows = pl.ds(pl.multiple_of(c * csub, csub), csub)
            k32 = keys_ref[rows, :]
            lo = (k32 & 0xFFFF) + i16_min
            half_ref[rows, :] = jnp.where((k32 >> 16) == hi, lo, i16_min).astype(jnp.int16)
            return carry

        lax.fori_loop(0, nchunk, low_halves, 0)
        thr_ref[...] = lax.shift_left(hi, 16) | kth_half(need - above)
        m_ref[...] = jnp.full(m_ref.shape, NEG, F32)
        l_ref[...] = jnp.zeros(l_ref.shape, F32)
        acc_ref[...] = jnp.zeros(acc_ref.shape, F32)

    thr = thr_ref[...]
    for c in range(tk // sub):
        rows = pl.ds(pl.multiple_of(j * tk + c * sub, sub), sub)
        madd_ref[c * sub:(c + 1) * sub, :] = jnp.where(keys_ref[rows, :] >= thr, 0.0, NEG)

    def logits(h, near):
        cs = slice(h * D_HEAD, (h + 1) * D_HEAD)
        qh = qt_ref[cs, :]
        mx = jnp.full((8, tq), NEG, F32)
        for c in range(tk // sub):
            rs = slice(c * sub, (c + 1) * sub)
            s = jnp.dot(k_ref[rs, cs], qh, preferred_element_type=F32)
            extra = bias_ref[h, rs, :] + madd_ref[rs, :] if near else madd_ref[rs, :]
            s = s + extra
            s_ref[h % s_ref.shape[0], rs, :] = s
            mx = jnp.maximum(mx, jnp.max(s.reshape(sub // 8, 8, tq), axis=0))
        return jnp.max(mx, axis=0, keepdims=True)

    def attend(near):
        def weighted_values(h, alpha):
            cs = slice(h * D_HEAD, (h + 1) * D_HEAD)
            vt_ones = jnp.concatenate([vt_ref[cs, :], jnp.ones((16, tk), BF16)], axis=0)
            pv = jnp.dot(vt_ones, p_ref[h % 2], preferred_element_type=F32)
            l_ref[h:h + 1, :] = alpha * l_ref[h:h + 1, :] + pv[D_HEAD:D_HEAD + 1, :]
            acc_ref[cs, :] = alpha * acc_ref[cs, :] + pv[:D_HEAD, :]

        ahead = s_ref.shape[0] - 1
        smaxes = [logits(h, near) for h in range(ahead)]
        alpha_prev = None
        for h in range(H_SA):
            m_prev = m_ref[h:h + 1, :]
            m_new = jnp.maximum(m_prev, smaxes[h])
            if h + ahead < H_SA:
                smaxes.append(logits(h + ahead, near))
            alpha = jnp.exp2(m_prev - m_new)
            for c in range(tk // sub):
                rs = slice(c * sub, (c + 1) * sub)
                p_ref[h % 2, rs, :] = jnp.exp2(
                    s_ref[h % (ahead + 1), rs, :] - m_new).astype(BF16)
            m_ref[h:h + 1, :] = m_new
            if h > 0:
                weighted_values(h - 1, alpha_prev)
            alpha_prev = alpha
        weighted_values(H_SA - 1, alpha_prev)

    pl.when(fartab[p] == 0)(functools.partial(attend, True))
    pl.when(fartab[p] == 1)(functools.partial(attend, False))

    @pl.when(ltab[p] == 1)
    def _():
        for h in range(H_SA):
            cs = slice(h * D_HEAD, (h + 1) * D_HEAD)
            o_ref[cs, :] = (acc_ref[cs, :] / l_ref[h:h + 1, :]).astype(o_ref.dtype)


def _dsa_attention(qit, wt, qt, ki_all, k_all, vt_all, rel_table, *, tq, tk, past_len):
    B, _, L = qt.shape
    Lkp = k_all.shape[1]
    assert Lkp % tk == 0 and L % tq == 0
    nq = L // tq
    it, jt, ft, lt_, nt, nbt, fart = [], [], [], [], [], [], []
    near_ds = []
    for i in range(nq):
        qpos0 = past_len + i * tq
        lim_max = ((qpos0 + tq - 1) // CHUNK + 1) * CHUNK
        jmax = (lim_max - 1) // tk
        for j in range(jmax + 1):
            d = qpos0 - j * tk
            if (tk - 1) - d >= _NEAR_REL:
                if d not in near_ds:
                    near_ds.append(d)
                nt.append(near_ds.index(d))
                fart.append(0)
            else:
                nt.append(nt[-1] if nt else 0)
                fart.append(1)
            it.append(i); jt.append(j); ft.append(int(j == 0)); lt_.append(int(j == jmax))
            nbt.append(jmax + 1)
    bias = _bias_tiles(near_ds, rel_table, tq=tq, tk=tk)
    tabs = [jnp.asarray(np.asarray(t, np.int32)) for t in (it, jt, ft, lt_, nt, nbt, fart)]
    grid_spec = pltpu.PrefetchScalarGridSpec(
        num_scalar_prefetch=7,
        grid=(B, len(it)),
        in_specs=[
            pl.BlockSpec((None, H_IDX * D_IDX, tq), lambda b, p, it, jt, *_: (b, 0, it[p])),
            pl.BlockSpec((None, H_IDX, tq), lambda b, p, it, jt, *_: (b, 0, it[p])),
            pl.BlockSpec((None, W_SA, tq), lambda b, p, it, jt, *_: (b, 0, it[p])),
            pl.BlockSpec((None, Lkp, D_IDX), lambda b, p, it, jt, *_: (b, 0, 0),
                         pipeline_mode=pl.Buffered(1)),
            pl.BlockSpec((None, tk, W_SA), lambda b, p, it, jt, *_: (b, jt[p], 0)),
            pl.BlockSpec((None, W_SA, tk), lambda b, p, it, jt, *_: (b, 0, jt[p])),
            pl.BlockSpec((None, H_SA, tk, tq),
                         lambda b, p, it, jt, ft, lt, nt, *_: (nt[p], 0, 0, 0)),
        ],
        out_specs=pl.BlockSpec((None, W_SA, tq), lambda b, p, it, jt, *_: (b, 0, it[p])),
        scratch_shapes=[
            pltpu.VMEM((Lkp, tq), I32),
            pltpu.VMEM((Lkp, tq), jnp.int16),
            pltpu.VMEM((1, tq), I32),
            pltpu.VMEM((H_SA, tq), F32),
            pltpu.VMEM((H_SA, tq), F32),
            pltpu.VMEM((W_SA, tq), F32),
            pltpu.VMEM((tk, tq), F32),
            pltpu.VMEM((3, tk, tq), F32),
            pltpu.VMEM((2, tk, tq), BF16),
        ],
    )
    return pl.pallas_call(
        functools.partial(_dsa_kernel, tq=tq, tk=tk, past_len=past_len,
                          sub=min(tk, 128), csub=min(tk, 512)),
        grid_spec=grid_spec,
        out_shape=jax.ShapeDtypeStruct((B, W_SA, L), BF16),
        compiler_params=_cparams(("arbitrary", "arbitrary")),
        name="dsa_attn",
    )(*tabs, qit, wt, qt, ki_all, k_all, vt_all, bias)


def _merge_kernel(osb_ref, osa_ref, g_ref, x_ref, gt_ref, wsb_ref, wsa_ref, wo_ref, o_ref):
    bt, lt, D = x_ref.shape
    tm = bt * lt
    a = jnp.dot(osb_ref[...].reshape(tm, W_SB), wsb_ref[...], preferred_element_type=F32)
    c = jnp.dot(osa_ref[...].reshape(tm, W_SA), wsa_ref[...], preferred_element_type=F32)
    g = g_ref[...].reshape(tm, 2 * D)
    merged = g[:, :D] * a + g[:, D:] * c
    y = jnp.dot(merged.astype(BF16), wo_ref[...], preferred_element_type=F32)
    o_ref[...] = x_ref[...] + gt_ref[...] * y.reshape(bt, lt, D)


def _merge(osb, osa, g, x, gt, wsb, wsa, wo, *, bt, lt):
    B, L, D = x.shape
    row = lambda w: pl.BlockSpec((bt, lt, w), lambda b, t: (b, t, 0))
    full = lambda a: pl.BlockSpec(a.shape, lambda b, t: (0, 0), pipeline_mode=pl.Buffered(1))
    return pl.pallas_call(
        _merge_kernel,
        grid=(B // bt, L // lt),
        in_specs=[row(W_SB), row(W_SA), row(2 * D), row(D),
                  pl.BlockSpec((bt, 1, D), lambda b, t: (b, 0, 0)),
                  full(wsb), full(wsa), full(wo)],
        out_specs=row(D),
        out_shape=jax.ShapeDtypeStruct((B, L, D), F32),
        compiler_params=_cparams(("arbitrary", "arbitrary")),
        name="merge_out",
    )(osb, osa, g, x, gt, wsb, wsa, wo)


HALO = 8


def _ffn_kernel(x_ref, sc_ref, sh_ref, gt_ref, g_ref, gfin_ref, prev_ref, wg_ref, wv_ref,
                cw_ref, cb_ref, wd_ref, x2_ref, st_ref,
                h_ref, acc_ref, ext_ref, carry_ref, *, final):
    bt, lt, D = x_ref.shape
    tm = bt * lt
    fc = wg_ref.shape[1]
    t = pl.program_id(1)
    f = pl.program_id(2)

    @pl.when(f == 0)
    def _():
        h = _norm_mod(x_ref[...], g_ref[...], sc_ref[...], sh_ref[...])
        h_ref[...] = h.reshape(tm, D).astype(BF16)
        acc_ref[...] = jnp.zeros(acc_ref.shape, F32)

    h = h_ref[...]
    ug = jnp.dot(h, wg_ref[...], preferred_element_type=F32).reshape(bt, lt, fc)
    uv = jnp.dot(h, wv_ref[...], preferred_element_type=F32).reshape(bt, lt, fc)

    @pl.when(t == 0)
    def _():
        ext_ref[:, HALO - 2:HALO, :] = prev_ref[...]

    @pl.when(t > 0)
    def _():
        ext_ref[:, HALO - 2:HALO, :] = carry_ref[f]

    ext_ref[:, HALO:HALO + lt, :] = ug
    tail = ug[:, lt - 2:lt, :]
    carry_ref[f] = tail
    st_ref[:, f] = tail
    cw = cw_ref[...]
    conv = (cb_ref[...] + cw[0:1, :] * ext_ref[:, HALO - 2:HALO - 2 + lt, :]
            + cw[1:2, :] * ext_ref[:, HALO - 1:HALO - 1 + lt, :] + cw[2:3, :] * ug)
    act = conv * jax.nn.sigmoid(conv) * uv
    acc_ref[...] += jnp.dot(act.reshape(tm, fc).astype(BF16), wd_ref[...],
                            preferred_element_type=F32)

    @pl.when(f == pl.num_programs(2) - 1)
    def _():
        x2 = x_ref[...] + gt_ref[...] * acc_ref[...].reshape(bt, lt, D)
        if final:
            ms = jnp.mean(x2 * x2, axis=-1, keepdims=True)
            x2 = x2 * lax.rsqrt(ms + EPS) * gfin_ref[...]
        x2_ref[...] = x2


def _ffn(x, sc, sh, gt, g, gfin, prev, w_up, cw, cb, w_down, *, bt, lt, fc, final):
    B, L, D = x.shape
    F = w_down.shape[0]
    nf = F // fc
    row = pl.BlockSpec((bt, lt, D), lambda b, t, f: (b, t, 0))
    mod = pl.BlockSpec((bt, 1, D), lambda b, t, f: (b, 0, 0))
    vec = pl.BlockSpec((1, D), lambda b, t, f: (0, 0))
    return pl.pallas_call(
        functools.partial(_ffn_kernel, final=final),
        grid=(B // bt, L // lt, nf),
        in_specs=[row, mod, mod, mod, vec, vec,
                  pl.BlockSpec((bt, CONV_W - 1, fc), lambda b, t, f: (b, 0, f)),
                  pl.BlockSpec((D, fc), lambda b, t, f: (0, f)),
                  pl.BlockSpec((D, fc), lambda b, t, f: (0, nf + f)),
                  pl.BlockSpec((CONV_W, fc), lambda b, t, f: (0, f)),
                  pl.BlockSpec((1, fc), lambda b, t, f: (0, f)),
                  pl.BlockSpec((fc, D), lambda b, t, f: (f, 0))],
        out_specs=[row, pl.BlockSpec((bt, nf, CONV_W - 1, fc), lambda b, t, f: (b, 0, 0, 0))],
        out_shape=[jax.ShapeDtypeStruct((B, L, D), F32),
                   jax.ShapeDtypeStruct((B, nf, CONV_W - 1, fc), F32)],
        scratch_shapes=[pltpu.VMEM((bt * lt, D), BF16),
                        pltpu.VMEM((bt * lt, D), F32),
                        pltpu.VMEM((bt, HALO + lt, fc), F32),
                        pltpu.VMEM((nf, bt, CONV_W - 1, fc), F32)],
        compiler_params=_cparams(("arbitrary", "arbitrary", "arbitrary")),
        name="conv_ffn",
    )(x, sc, sh, gt, g, gfin, prev, w_up, w_up, cw, cb, w_down)


CACHE_BLOCK = 256


def _cache_kernel(c_ref, n_ref, o_ref, *, feature_major):
    blk = CACHE_BLOCK
    P = c_ref.shape[0]
    lkp = o_ref.shape[1] if feature_major else o_ref.shape[0]

    def put(r, v):
        rows = slice(r * blk, (r + 1) * blk)
        if feature_major:
            o_ref[:, rows] = v.T.astype(o_ref.dtype)
        else:
            o_ref[rows, :] = v.astype(o_ref.dtype)

    for r in range(P // blk):
        c = c_ref[r * blk:(r + 1) * blk]
        put(r, pltpu.einshape("thd->t(hd)", c) if c.ndim == 3 else c)
    new = n_ref[...].astype(F32)
    put(P // blk, jnp.concatenate(
        [new, jnp.zeros((blk - new.shape[0], new.shape[1]), F32)], axis=0))
    for r in range(P // blk + 1, lkp // blk):
        put(r, jnp.zeros((blk, new.shape[1]), F32))


def _cache_keys(cache, new, lkp, feature_major):
    B, P = cache.shape[:2]
    L, W = new.shape[1:]
    blk = CACHE_BLOCK
    assert P % blk == 0 and lkp % blk == 0 and L <= blk and lkp >= P + L
    cblock = (None,) + cache.shape[1:]
    cmap = (lambda b: (b, 0, 0, 0)) if cache.ndim == 4 else (lambda b: (b, 0, 0))
    oshape = (W, lkp) if feature_major else (lkp, W)
    return pl.pallas_call(
        functools.partial(_cache_kernel, feature_major=feature_major),
        grid=(B,),
        in_specs=[pl.BlockSpec(cblock, cmap),
                  pl.BlockSpec((None, L, W), lambda b: (b, 0, 0))],
        out_specs=pl.BlockSpec((None,) + oshape, lambda b: (b, 0, 0)),
        out_shape=jax.ShapeDtypeStruct((B,) + oshape, BF16),
        compiler_params=_cparams(("arbitrary",)),
        name="cache_keys",
    )(cache, new)


def _pad_cols(w, mult):
    n = w.shape[1]
    npad = -(-n // mult) * mult
    return jnp.pad(w, ((0, 0), (0, npad - n)))


def _tiles(B, L, rows):
    if L >= rows:
        assert L % rows == 0
        return 1, rows
    assert rows % L == 0 and B % (rows // L) == 0
    return rows // L, L


def _layer(x, mod, past, wts, *, last, g_final):
    (g_mix, w_in_sb, w_in_sa, w_in_ix, w_gate, w_br_sb, w_br_sa, w_out, rel_table, g_ffn, w_up,
     conv_w, conv_b, w_down) = wts
    B, L, D = x.shape
    sh1, sc1, gt1, sh2, sc2, gt2 = (m[:, None, :] for m in jnp.split(mod, 6, axis=-1))
    past_len = 0 if past is None else past[0].shape[1]

    wn = W_SB
    bt, lt = _tiles(B, L, 512)
    fm = bt == 1
    qlay = "feature" if fm else "row"
    call = functools.partial(_proj, x, sc1, sh1, g_mix, bt=bt, lt=lt)
    q_sb, k_sb, kb_sb, v_sb, vb_sb = call(
        w_in_sb, [_Out(0, 0, wn, BF16), _Out(1, 0, wn, F32, "heads"), _Out(1, 0, wn, BF16),
                  _Out(2, 0, wn, F32, "heads"), _Out(2, 0, wn, BF16)], tn=wn, name="proj_sb")
    q_sa, k_sa, kb_sa, v_sa, vb_sa = call(
        w_in_sa, [_Out(0, 0, wn, BF16, qlay, scale=ATT_SCALE * LOG2E),
                  _Out(1, 0, wn, F32, "heads"), _Out(1, 0, wn, BF16),
                  _Out(2, 0, wn, F32, "heads"), _Out(2, 0, wn, BF16, qlay)],
        tn=wn, name="proj_sa")
    w_ix_out = (_Out(1, 0, LANE, F32, "feature", (D_IDX, D_IDX + H_IDX)) if fm
                else _Out(1, D_IDX, D_IDX + H_IDX, F32))
    q_ix, k_ix, kb_ix, w_ix = call(
        w_in_ix, [_Out(0, 0, wn, BF16, qlay), _Out(1, 0, D_IDX, F32), _Out(1, 0, D_IDX, BF16),
                  w_ix_out], tn=wn, name="proj_ix")
    bt, lt = _tiles(B, L, 1024)
    gate, = _proj(x, sc1, sh1, g_mix, w_gate, [_Out(None, 0, 512, F32)], bt=bt, lt=lt, tn=512,
                  sigmoid=True, name="proj_gate")
    if not fm:
        q_sa, q_ix, w_ix = (jnp.swapaxes(a, 1, 2) for a in (q_sa, q_ix, w_ix))

    tq = min(256, L)
    tk_sb = 256
    tk_sa = 512 if L >= 512 else 256
    l_keys = past_len + L

    def keys_of(new, old, tk, feature_major=False):
        if old is None:
            assert l_keys % tk == 0
            return new
        assert not fm
        return _cache_keys(old, new, -(-l_keys // tk) * tk, feature_major)

    old = (None,) * 5 if past is None else past[:5]
    o_sb = _sb_attention(q_sb, keys_of(kb_sb, old[0], tk_sb), keys_of(vb_sb, old[1], tk_sb),
                         tq=tq, tk=tk_sb, past_len=past_len)
    assert fm or past is not None
    vt = keys_of(vb_sa, old[3], tk_sa, True)
    o_sa_t = _dsa_attention(q_ix, w_ix, q_sa, keys_of(kb_ix, old[4], tk_sa),
                            keys_of(kb_sa, old[2], tk_sa), vt, rel_table,
                            tq=tq, tk=tk_sa, past_len=past_len)
    o_sa = jnp.swapaxes(o_sa_t, 1, 2)

    bt, lt = _tiles(B, L, 256)
    x1 = _merge(o_sb, o_sa, gate, x, gt1, w_br_sb, w_br_sa, w_out, bt=bt, lt=lt)

    prev = jnp.zeros((B, CONV_W - 1, w_down.shape[0]), F32) if past is None else past[5]
    bt, lt = _tiles(B, L, 512)
    x2, conv_state = _ffn(x1, sc2, sh2, gt2, g_ffn, g_final, prev, w_up, conv_w, conv_b,
                          w_down, bt=bt, lt=lt, fc=512, final=last)
    conv_state = jnp.swapaxes(conv_state, 1, 2).reshape(B, CONV_W - 1, -1)
    return x2, (k_sb, v_sb, k_sa, v_sa, k_ix, conv_state)


def kernel(x_prompt, x_sample, cache_sb_k, cache_sb_v, cache_sa_k, cache_sa_v, cache_idx_k,
           state_ffn_conv, c_prompt, c_sample, w_ada, b_ada, g_mix, w_in, w_gate, w_br_sb,
           w_br_sa, w_out, rel_table, g_ffn, w_up, conv_w, conv_b, w_down, g_final):
    depth = w_ada.shape[0]
    nbp = c_prompt.shape[0]
    nbs = c_sample.shape[0]
    rows = -(-(nbp + nbs) // 8) * 8
    c_all = jnp.concatenate([c_prompt, c_sample,
                             jnp.zeros((rows - nbp - nbs, c_prompt.shape[1]), F32)], axis=0)
    xp, xs = x_prompt, x_sample
    new_p, new_s = [], []
    for l in range(depth):
        mod = _adaln(c_all, w_ada[l], b_ada[l][None, :])
        wb = w_in[l].astype(BF16)
        wts = (g_mix[l][None, :], wb[:, :COL_Q_SA], wb[:, COL_Q_SA:COL_Q_IX],
               _pad_cols(wb[:, COL_Q_IX:], 2 * W_SB), w_gate[l].astype(BF16),
               w_br_sb[l].astype(BF16), w_br_sa[l].astype(BF16), w_out[l].astype(BF16),
               rel_table, g_ffn[l][None, :], w_up[l].astype(BF16), conv_w[l],
               conv_b[l][None, :], w_down[l].astype(BF16))
        gfin = g_final[None, :]
        past = (cache_sb_k[l], cache_sb_v[l], cache_sa_k[l], cache_sa_v[l], cache_idx_k[l],
                state_ffn_conv[l])
        xp, sp = _layer(xp, mod[:nbp], None, wts, last=l == depth - 1, g_final=gfin)
        xs, ss = _layer(xs, mod[nbp:nbp + nbs], past, wts, last=l == depth - 1, g_final=gfin)
        new_p.append(sp)
        new_s.append(ss)
    stack = lambda states, n: jnp.stack([s[n] for s in states])
    return ((xp, xs) + tuple(stack(new_p, n) for n in range(6))
            + tuple(stack(new_s, n) for n in range(6)))
```

```python
import functools

import numpy as np
import jax
import jax.numpy as jnp
from jax import lax
from jax.experimental import pallas as pl
from jax.experimental.pallas import tpu as pltpu

F32 = jnp.float32
BF16 = jnp.bfloat16
I32 = jnp.int32

CHUNK = 64
D_HEAD = 128
H_SB = 8
H_SA = 8
W_SB = H_SB * D_HEAD
W_SA = H_SA * D_HEAD
H_IDX = 16
D_IDX = 64
TOPK_MAX = 256
N_BUCKETS = 32
REL_MAX_DIST = 1024
CONV_W = 3
EPS = 1e-6

COL_Q_SB = 0
COL_K_SB = W_SB
COL_V_SB = 2 * W_SB
COL_Q_SA = 3 * W_SB
COL_K_SA = 3 * W_SB + W_SA
COL_V_SA = 3 * W_SB + 2 * W_SA
COL_Q_IX = 3 * W_SB + 3 * W_SA
COL_K_IX = COL_Q_IX + H_IDX * D_IDX
COL_W_IX = COL_K_IX + D_IDX
IN_COLS = COL_W_IX + H_IDX

LANE = 128
VMEM_LIMIT = 56 * 1024 * 1024

ATT_SCALE = D_HEAD ** -0.5
LOG2E = 1.4426950408889634
IDX_SCALE = (D_IDX ** -0.5) * (H_IDX ** -0.5)
NEG = -1e30
INT_MIN = -2 ** 31
SB_DEAD = -104.0


def _cparams(sem):
    return pltpu.CompilerParams(dimension_semantics=sem, vmem_limit_bytes=VMEM_LIMIT)


def _adaln_kernel(c_ref, w_ref, b_ref, o_ref):
    c = c_ref[...]
    a = c * jax.nn.sigmoid(c)
    o_ref[...] = jnp.dot(a.astype(BF16), w_ref[...].astype(BF16),
                         preferred_element_type=F32) + b_ref[...]


def _adaln(c, w, b):
    R, D = c.shape
    N = w.shape[1]
    tn = 1024
    return pl.pallas_call(
        _adaln_kernel,
        grid=(N // tn,),
        in_specs=[pl.BlockSpec((R, D), lambda n: (0, 0)),
                  pl.BlockSpec((D, tn), lambda n: (0, n)),
                  pl.BlockSpec((1, tn), lambda n: (0, n))],
        out_specs=pl.BlockSpec((R, tn), lambda n: (0, n)),
        out_shape=jax.ShapeDtypeStruct((R, N), F32),
        compiler_params=_cparams(("arbitrary",)),
        name="adaln",
    )(c, w, b)


NORM_ROWS = 16


def _norm_mod_store(x_ref, g_ref, sc_ref, sh_ref, h_ref):
    bt, lt, _ = x_ref.shape
    g = g_ref[...]
    for b in range(bt):
        sc1 = 1.0 + sc_ref[b]
        sh = sh_ref[b]
        for r0 in range(0, lt, NORM_ROWS):
            r1 = min(r0 + NORM_ROWS, lt)
            x = x_ref[b, r0:r1, :]
            ms = jnp.mean(x * x, axis=-1, keepdims=True)
            y = x * lax.rsqrt(ms + EPS) * g
            h_ref[b * lt + r0:b * lt + r1, :] = (y * sc1 + sh).astype(h_ref.dtype)


class _Out(tuple):
    __slots__ = ()

    def __new__(cls, tile, lo, hi, dtype, layout="row", rows=None, scale=None):
        return tuple.__new__(cls, (tile, lo, hi, rows or (0, hi - lo), layout, dtype, scale))


def _proj_kernel(x_ref, sc_ref, sh_ref, g_ref, w_ref, *rest, plan, sigmoid):
    out_refs, h_ref = rest[:-1], rest[-1]
    bt, lt, D = x_ref.shape
    n = pl.program_id(2)

    @pl.when(n == 0)
    def _():
        _norm_mod_store(x_ref, g_ref, sc_ref, sh_ref, h_ref)

    r = jnp.dot(h_ref[...], w_ref[...], preferred_element_type=F32)
    if sigmoid:
        r = jnp.concatenate([jax.nn.sigmoid(r[r0:r0 + NORM_ROWS]) for r0 in
                             range(0, bt * lt, NORM_ROWS)], axis=0)
    for (tile, lo, hi, rows, layout, _, scale), o_ref in zip(plan, out_refs):
        def emit(o_ref=o_ref, lo=lo, hi=hi, rows=rows, layout=layout, scale=scale):
            v = r[:, lo:hi]
            if scale is not None:
                v = v * scale
            if layout == "feature":
                o_ref[...] = v.T[rows[0]:rows[1], :].astype(o_ref.dtype)
            elif layout == "heads":
                nh = (hi - lo) // D_HEAD
                v = pltpu.einshape("t(hd)->thd", v, h=nh)
                o_ref[...] = v.reshape(bt, lt, nh, D_HEAD).astype(o_ref.dtype)
            else:
                o_ref[...] = v.reshape(bt, lt, hi - lo).astype(o_ref.dtype)

        if tile is None:
            emit()
        else:
            pl.when(n == tile)(emit)


def _proj(x, sc, sh, g, w, plan, *, bt, lt, tn, sigmoid=False, name):
    B, L, D = x.shape
    N = w.shape[1]
    out_specs, out_shape = [], []
    for tile, lo, hi, rows, layout, dtype, _ in plan:
        if tile is None:
            out_specs.append(pl.BlockSpec((bt, lt, tn), lambda b, t, n: (b, t, n)))
            out_shape.append(jax.ShapeDtypeStruct((B, L, N), dtype))
        elif layout == "heads":
            nh = (hi - lo) // D_HEAD
            out_specs.append(pl.BlockSpec((bt, lt, nh, D_HEAD), lambda b, t, n: (b, t, 0, 0)))
            out_shape.append(jax.ShapeDtypeStruct((B, L, nh, D_HEAD), dtype))
        elif layout == "feature":
            assert bt == 1
            width = rows[1] - rows[0]
            out_specs.append(pl.BlockSpec((None, width, lt), lambda b, t, n: (b, 0, t)))
            out_shape.append(jax.ShapeDtypeStruct((B, width, L), dtype))
        else:
            out_specs.append(pl.BlockSpec((bt, lt, hi - lo), lambda b, t, n: (b, t, 0)))
            out_shape.append(jax.ShapeDtypeStruct((B, L, hi - lo), dtype))
    return pl.pallas_call(
        functools.partial(_proj_kernel, plan=tuple(plan), sigmoid=sigmoid),
        grid=(B // bt, L // lt, N // tn),
        in_specs=[pl.BlockSpec((bt, lt, D), lambda b, t, n: (b, t, 0)),
                  pl.BlockSpec((bt, 1, D), lambda b, t, n: (b, 0, 0)),
                  pl.BlockSpec((bt, 1, D), lambda b, t, n: (b, 0, 0)),
                  pl.BlockSpec((1, D), lambda b, t, n: (0, 0)),
                  pl.BlockSpec((D, tn), lambda b, t, n: (0, n))],
        out_specs=out_specs,
        out_shape=out_shape,
        scratch_shapes=[pltpu.VMEM((bt * lt, D), BF16)],
        compiler_params=_cparams(("arbitrary", "arbitrary", "arbitrary")),
        name=name,
    )(x, sc, sh, g, w)


def _sb_kernel(q_ref, k_ref, v_ref, u_ref, o_ref, *, tq, tk, past_len, hp):
    i = pl.program_id(2)
    qpos0 = past_len + i * tq
    qpos = qpos0 + lax.broadcasted_iota(I32, (tq, 1), 0)
    j0 = (qpos0 + tq - 2) // tk
    u = u_ref[...]

    def cond(carry):
        j, go, _, _ = carry
        return jnp.logical_and(j >= 0, go)

    def body(carry):
        j, _, cums, accs = carry
        off = pl.multiple_of(j * tk, tk)
        kpos = j * tk + lax.broadcasted_iota(I32, (1, tk), 1)
        mask = kpos < qpos
        new_cums, new_accs = [], []
        for h in range(hp):
            cs = slice(h * D_HEAD, (h + 1) * D_HEAD)
            z = lax.dot_general(q_ref[:, cs], k_ref[pl.ds(off, tk), cs], _NT,
                                preferred_element_type=F32) * ATT_SCALE
            t = jnp.log(1.0 + jnp.exp(-jnp.abs(z)))
            lk = jnp.where(mask, -(jnp.maximum(z, 0.0) + t), 0.0)
            hi = lk.astype(BF16)
            lo = (lk - hi.astype(F32)).astype(BF16)
            after = cums[h] + (jnp.dot(hi, u, preferred_element_type=F32)
                               + jnp.dot(lo, u, preferred_element_type=F32))
            lsz = jnp.minimum(z, 0.0) - t
            a = jnp.where(mask, jnp.exp(lsz + after), 0.0)
            new_accs.append(accs[h] + jnp.dot(a.astype(BF16), v_ref[pl.ds(off, tk), cs],
                                              preferred_element_type=F32))
            new_cums.append(cums[h] + jnp.sum(lk, axis=1, keepdims=True))
        top = new_cums[0]
        for c in new_cums[1:]:
            top = jnp.maximum(top, c)
        return j - 1, jnp.max(top) > SB_DEAD, tuple(new_cums), tuple(new_accs)

    init = (j0, jnp.bool_(True), (jnp.zeros((tq, 1), F32),) * hp,
            (jnp.zeros((tq, D_HEAD), F32),) * hp)
    _, _, _, accs = lax.while_loop(cond, body, init)
    for h in range(hp):
        o_ref[:, h * D_HEAD:(h + 1) * D_HEAD] = accs[h].astype(o_ref.dtype)


def _sb_attention(q, k_all, v_all, *, tq, tk, past_len, hp=4):
    B, L, _ = q.shape
    Lkp = k_all.shape[1]
    r = np.arange(tk)
    u = jnp.asarray((r[:, None] > r[None, :]).astype(np.float32), BF16)
    wp = hp * D_HEAD
    kv_mode = dict(pipeline_mode=pl.Buffered(1)) if L // tq > 1 else {}
    return pl.pallas_call(
        functools.partial(_sb_kernel, tq=tq, tk=tk, past_len=past_len, hp=hp),
        grid=(B, H_SB // hp, L // tq),
        in_specs=[pl.BlockSpec((None, tq, wp), lambda b, h, i: (b, i, h)),
                  pl.BlockSpec((None, Lkp, wp), lambda b, h, i: (b, 0, h), **kv_mode),
                  pl.BlockSpec((None, Lkp, wp), lambda b, h, i: (b, 0, h), **kv_mode),
                  pl.BlockSpec((tk, tk), lambda b, h, i: (0, 0))],
        out_specs=pl.BlockSpec((None, tq, wp), lambda b, h, i: (b, i, h)),
        out_shape=jax.ShapeDtypeStruct((B, L, W_SB), BF16),
        compiler_params=_cparams(("arbitrary", "arbitrary", "arbitrary")),
        name="sb_attn",
    )(q, k_all, v_all, u)


def _bucket_edges():
    nb = N_BUCKETS // 2
    max_exact = nb // 2

    def bucket(rel):
        n = abs(rel)
        if n < max_exact:
            v = n
        else:
            v = max_exact + int(np.log(np.float32(n) / max_exact)
                                / np.log(REL_MAX_DIST / max_exact) * (nb - max_exact))
            v = min(v, nb - 1)
        return (nb if rel > 0 else 0) + v

    lo = -4 * REL_MAX_DIST
    assert bucket(lo) == nb - 1
    edges = []
    prev = bucket(lo)
    for rel in range(lo + 1, CHUNK):
        bk = bucket(rel)
        if bk != prev:
            edges.append((rel, prev))
            prev = bk
    edges.append((CHUNK, prev))
    return edges


_EDGES = _bucket_edges()
_FAR_BUCKET = _EDGES[0][1]
_NEAR_REL = _EDGES[0][0]


def _bias_kernel(d_ref, rel_ref, o_ref):
    tk, tq = o_ref.shape
    h = pl.program_id(1)
    rel = (lax.broadcasted_iota(I32, (tk, tq), 0) - lax.broadcasted_iota(I32, (tk, tq), 1)
           - d_ref[pl.program_id(0)])
    tile = jnp.full((tk, tq), rel_ref[_EDGES[-1][1], h], F32)
    for edge, bk in reversed(_EDGES[:-1]):
        tile = jnp.where(rel < edge, rel_ref[bk, h], tile)
    o_ref[...] = (tile - rel_ref[_FAR_BUCKET, h]) * LOG2E


def _bias_tiles(ds, rel_table, *, tq, tk):
    return pl.pallas_call(
        _bias_kernel,
        grid=(len(ds), H_SA),
        in_specs=[pl.BlockSpec(memory_space=pltpu.SMEM), pl.BlockSpec(memory_space=pltpu.SMEM)],
        out_specs=pl.BlockSpec((None, None, tk, tq), lambda n, h: (n, h, 0, 0)),
        out_shape=jax.ShapeDtypeStruct((len(ds), H_SA, tk, tq), F32),
        compiler_params=_cparams(("arbitrary", "arbitrary")),
        name="bias_tiles",
    )(jnp.asarray(np.asarray(ds, np.int32)), rel_table)


_NT = (((1,), (1,)), ((), ()))


def _dsa_kernel(itab, jtab, ftab, ltab, ntab, nbtab, fartab,
                qit_ref, wt_ref, qt_ref, ki_ref, k_ref, vt_ref, bias_ref, o_ref,
                keys_ref, half_ref, thr_ref, m_ref, l_ref, acc_ref, madd_ref, s_ref, p_ref,
                *, tq, tk, past_len, sub, csub):
    p = pl.program_id(1)
    i = itab[p]
    j = jtab[p]
    qpos = past_len + i * tq + lax.broadcasted_iota(I32, (1, tq), 1)
    lim = (qpos // CHUNK + 1) * CHUNK

    @pl.when(ftab[p] == 1)
    def _():
        wt = wt_ref[...] * IDX_SCALE

        def score_blk(c, carry):
            off = pl.multiple_of(c * sub, sub)
            kib = ki_ref[pl.ds(off, sub), :]
            s = jnp.zeros((sub, tq), F32)
            for h in range(H_IDX):
                sh = jnp.dot(kib, qit_ref[h * D_IDX:(h + 1) * D_IDX, :],
                             preferred_element_type=F32)
                s = s + wt[h:h + 1, :] * jnp.maximum(sh, 0.0)
            bits = pltpu.bitcast(s, I32)
            key = jnp.where(bits < 0, bits ^ jnp.int32(0x7FFFFFFF), bits)
            kpos = c * sub + lax.broadcasted_iota(I32, (sub, 1), 0)
            key = jnp.where(kpos < lim, key, jnp.int32(INT_MIN))
            keys_ref[pl.ds(off, sub), :] = key
            half_ref[pl.ds(off, sub), :] = (key >> 16).astype(jnp.int16)
            return carry

        per = tk // sub

        def score_grp(g, carry):
            for u in range(per):
                score_blk(g * per + u, carry)
            return carry

        lax.fori_loop(0, nbtab[p], score_grp, 0)

        need = jnp.minimum(TOPK_MAX, lim)
        nchunk = nbtab[p] * (tk // csub)
        i16_min = -2 ** 15

        def count_ge(cand):
            c16 = jnp.broadcast_to(cand, (16, tq)).astype(jnp.int16)

            def cnt_blk(c, acc):
                kb = half_ref[pl.ds(pl.multiple_of(c * csub, csub), csub), :]
                parts = [jnp.where(kb[16 * r:16 * (r + 1), :] >= c16, jnp.int16(1), jnp.int16(0))
                         for r in range(csub // 16)]
                while len(parts) > 1:
                    parts = [a + b for a, b in zip(parts[::2], parts[1::2])]
                return acc + parts[0]

            acc = lax.fori_loop(0, nchunk, cnt_blk, jnp.zeros((16, tq), jnp.int16))
            return jnp.sum(acc.astype(I32), axis=0, keepdims=True)

        def kth_half(want):
            def bit_step(it, tpre):
                bitv = lax.shift_left(jnp.int32(1), 15 - it)
                hit = count_ge((tpre | bitv) + i16_min) >= want
                return jnp.where(hit, tpre | bitv, tpre)

            return lax.fori_loop(0, 16, bit_step, jnp.zeros((1, tq), I32))

        hi = kth_half(need) + i16_min
        above = jnp.where(hi == 2 ** 15 - 1, 0, count_ge(jnp.minimum(hi + 1, 2 ** 15 - 1)))

        def low_halves(c, carry):
            rows = pl.ds(pl.multiple_of(c * csub, csub), csub)
            k32 = keys_ref[rows, :]
            lo = (k32 & 0xFFFF) + i16_min
            half_ref[rows, :] = jnp.where((k32 >> 16) == hi, lo, i16_min).astype(jnp.int16)
            return carry

        lax.fori_loop(0, nchunk, low_halves, 0)
        thr_ref[...] = lax.shift_left(hi, 16) | kth_half(need - above)
        m_ref[...] = jnp.full(m_ref.shape, NEG, F32)
        l_ref[...] = jnp.zeros(l_ref.shape, F32)
        acc_ref[...] = jnp.zeros(acc_ref.shape, F32)

    thr = thr_ref[...]
    for c in range(tk // sub):
        rows = pl.ds(pl.multiple_of(j * tk + c * sub, sub), sub)
        madd_ref[c * sub:(c + 1) * sub, :] = jnp.where(keys_ref[rows, :] >= thr, 0.0, NEG)

    def logits(h, near):
        cs = slice(h * D_HEAD, (h + 1) * D_HEAD)
        qh = qt_ref[cs, :]
        mx = jnp.full((8, tq), NEG, F32)
        for c in range(tk // sub):
            rs = slice(c * sub, (c + 1) * sub)
            s = jnp.dot(k_ref[rs, cs], qh, preferred_element_type=F32)
            extra = bias_ref[h, rs, :] + madd_ref[rs, :] if near else madd_ref[rs, :]
            s = s + extra
            s_ref[h % s_ref.shape[0], rs, :] = s
            mx = jnp.maximum(mx, jnp.max(s.reshape(sub // 8, 8, tq), axis=0))
        return jnp.max(mx, axis=0, keepdims=True)

    def attend(near):
        def weighted_values(h, alpha):
            cs = slice(h * D_HEAD, (h + 1) * D_HEAD)
            vt_ones = jnp.concatenate([vt_ref[cs, :], jnp.ones((16, tk), BF16)], axis=0)
            pv = jnp.dot(vt_ones, p_ref[h % 2], preferred_element_type=F32)
            l_ref[h:h + 1, :] = alpha * l_ref[h:h + 1, :] + pv[D_HEAD:D_HEAD + 1, :]
            acc_ref[cs, :] = alpha * acc_ref[cs, :] + pv[:D_HEAD, :]

        ahead = s_ref.shape[0] - 1
        smaxes = [logits(h, near) for h in range(ahead)]
        alpha_prev = None
        for h in range(H_SA):
            m_prev = m_ref[h:h + 1, :]
            m_new = jnp.maximum(m_prev, smaxes[h])
            if h + ahead < H_SA:
                smaxes.append(logits(h + ahead, near))
            alpha = jnp.exp2(m_prev - m_new)
            for c in range(tk // sub):
                rs = slice(c * sub, (c + 1) * sub)
                p_ref[h % 2, rs, :] = jnp.exp2(
                    s_ref[h % (ahead + 1), rs, :] - m_new).astype(BF16)
            m_ref[h:h + 1, :] = m_new
            if h > 0:
                weighted_values(h - 1, alpha_prev)
            alpha_prev = alpha
        weighted_values(H_SA - 1, alpha_prev)

    pl.when(fartab[p] == 0)(functools.partial(attend, True))
    pl.when(fartab[p] == 1)(functools.partial(attend, False))

    @pl.when(ltab[p] == 1)
    def _():
        for h in range(H_SA):
            cs = slice(h * D_HEAD, (h + 1) * D_HEAD)
            o_ref[cs, :] = (acc_ref[cs, :] / l_ref[h:h + 1, :]).astype(o_ref.dtype)


def _dsa_attention(qit, wt, qt, ki_all, k_all, vt_all, rel_table, *, tq, tk, past_len):
    B, _, L = qt.shape
    Lkp = k_all.shape[1]
    assert Lkp % tk == 0 and L % tq == 0
    nq = L // tq
    it, jt, ft, lt_, nt, nbt, fart = [], [], [], [], [], [], []
    near_ds = []
    for i in range(nq):
        qpos0 = past_len + i * tq
        lim_max = ((qpos0 + tq - 1) // CHUNK + 1) * CHUNK
        jmax = (lim_max - 1) // tk
        for j in range(jmax + 1):
            d = qpos0 - j * tk
            if (tk - 1) - d >= _NEAR_REL:
                if d not in near_ds:
                    near_ds.append(d)
                nt.append(near_ds.index(d))
                fart.append(0)
            else:
                nt.append(nt[-1] if nt else 0)
                fart.append(1)
            it.append(i); jt.append(j); ft.append(int(j == 0)); lt_.append(int(j == jmax))
            nbt.append(jmax + 1)
    bias = _bias_tiles(near_ds, rel_table, tq=tq, tk=tk)
    tabs = [jnp.asarray(np.asarray(t, np.int32)) for t in (it, jt, ft, lt_, nt, nbt, fart)]
    grid_spec = pltpu.PrefetchScalarGridSpec(
        num_scalar_prefetch=7,
        grid=(B, len(it)),
        in_specs=[
            pl.BlockSpec((None, H_IDX * D_IDX, tq), lambda b, p, it, jt, *_: (b, 0, it[p])),
            pl.BlockSpec((None, H_IDX, tq), lambda b, p, it, jt, *_: (b, 0, it[p])),
            pl.BlockSpec((None, W_SA, tq), lambda b, p, it, jt, *_: (b, 0, it[p])),
            pl.BlockSpec((None, Lkp, D_IDX), lambda b, p, it, jt, *_: (b, 0, 0),
                         pipeline_mode=pl.Buffered(1)),
            pl.BlockSpec((None, tk, W_SA), lambda b, p, it, jt, *_: (b, jt[p], 0)),
            pl.BlockSpec((None, W_SA, tk), lambda b, p, it, jt, *_: (b, 0, jt[p])),
            pl.BlockSpec((None, H_SA, tk, tq),
                         lambda b, p, it, jt, ft, lt, nt, *_: (nt[p], 0, 0, 0)),
        ],
        out_specs=pl.BlockSpec((None, W_SA, tq), lambda b, p, it, jt, *_: (b, 0, it[p])),
        scratch_shapes=[
            pltpu.VMEM((Lkp, tq), I32),
            pltpu.VMEM((Lkp, tq), jnp.int16),
            pltpu.VMEM((1, tq), I32),
            pltpu.VMEM((H_SA, tq), F32),
            pltpu.VMEM((H_SA, tq), F32),
            pltpu.VMEM((W_SA, tq), F32),
            pltpu.VMEM((tk, tq), F32),
            pltpu.VMEM((3, tk, tq), F32),
            pltpu.VMEM((2, tk, tq), BF16),
        ],
    )
    return pl.pallas_call(
        functools.partial(_dsa_kernel, tq=tq, tk=tk, past_len=past_len,
                          sub=min(tk, 128), csub=min(tk, 512)),
        grid_spec=grid_spec,
        out_shape=jax.ShapeDtypeStruct((B, W_SA, L), BF16),
        compiler_params=_cparams(("arbitrary", "arbitrary")),
        name="dsa_attn",
    )(*tabs, qit, wt, qt, ki_all, k_all, vt_all, bias)


def _merge_kernel(osb_ref, osa_ref, g_ref, x_ref, gt_ref, wsb_ref, wsa_ref, wo_ref, o_ref):
    bt, lt, D = x_ref.shape
    tm = bt * lt
    a = jnp.dot(osb_ref[...].reshape(tm, W_SB), wsb_ref[...], preferred_element_type=F32)
    c = jnp.dot(osa_ref[...].reshape(tm, W_SA), wsa_ref[...], preferred_element_type=F32)
    g = g_ref[...].reshape(tm, 2 * D)
    merged = g[:, :D] * a + g[:, D:] * c
    y = jnp.dot(merged.astype(BF16), wo_ref[...], preferred_element_type=F32)
    o_ref[...] = x_ref[...] + gt_ref[...] * y.reshape(bt, lt, D)


def _merge(osb, osa, g, x, gt, wsb, wsa, wo, *, bt, lt):
    B, L, D = x.shape
    row = lambda w: pl.BlockSpec((bt, lt, w), lambda b, t: (b, t, 0))
    full = lambda a: pl.BlockSpec(a.shape, lambda b, t: (0, 0), pipeline_mode=pl.Buffered(1))
    return pl.pallas_call(
        _merge_kernel,
        grid=(B // bt, L // lt),
        in_specs=[row(W_SB), row(W_SA), row(2 * D), row(D),
                  pl.BlockSpec((bt, 1, D), lambda b, t: (b, 0, 0)),
                  full(wsb), full(wsa), full(wo)],
        out_specs=row(D),
        out_shape=jax.ShapeDtypeStruct((B, L, D), F32),
        compiler_params=_cparams(("arbitrary", "arbitrary")),
        name="merge_out",
    )(osb, osa, g, x, gt, wsb, wsa, wo)


HALO = 8
FFN_ROWS = 32


def _ffn_kernel(x_ref, sc_ref, sh_ref, gt_ref, g_ref, gfin_ref, prev_ref, wg_ref, wv_ref,
                cw_ref, cb_ref, wd_ref, x2_ref, st_ref,
                h_ref, acc_ref, ext_ref, carry_ref, uv_ref, act_ref, *, final):
    bt, lt, D = x_ref.shape
    tm = bt * lt
    fc = wg_ref.shape[1]
    t = pl.program_id(1)
    f = pl.program_id(2)

    @pl.when(f == 0)
    def _():
        _norm_mod_store(x_ref, g_ref, sc_ref, sh_ref, h_ref)
        acc_ref[...] = jnp.zeros(acc_ref.shape, F32)

    h = h_ref[...]

    @pl.when(t == 0)
    def _():
        ext_ref[:, HALO - 2:HALO, :] = prev_ref[...]

    @pl.when(t > 0)
    def _():
        ext_ref[:, HALO - 2:HALO, :] = carry_ref[f]

    ext_ref[:, HALO:HALO + lt, :] = jnp.dot(
        h, wg_ref[...], preferred_element_type=F32).reshape(bt, lt, fc)
    uv_ref[...] = jnp.dot(h, wv_ref[...], preferred_element_type=F32).reshape(bt, lt, fc)
    tail = ext_ref[:, HALO + lt - 2:HALO + lt, :]
    carry_ref[f] = tail
    st_ref[:, f] = tail
    cw = cw_ref[...]
    cb = cb_ref[...]
    for b in range(bt):
        for r0 in range(0, lt, FFN_ROWS):
            r1 = min(r0 + FFN_ROWS, lt)
            e = [ext_ref[b, HALO - 2 + j + r0:HALO - 2 + j + r1, :] for j in range(CONV_W)]
            conv = cb + cw[0:1, :] * e[0] + cw[1:2, :] * e[1] + cw[2:3, :] * e[2]
            act = conv * jax.nn.sigmoid(conv) * uv_ref[b, r0:r1, :]
            act_ref[b * lt + r0:b * lt + r1, :] = act.astype(BF16)
    acc_ref[...] += jnp.dot(act_ref[...], wd_ref[...], preferred_element_type=F32)

    @pl.when(f == pl.num_programs(2) - 1)
    def _():
        gfin = gfin_ref[...]
        for b in range(bt):
            gt = gt_ref[b]
            for r0 in range(0, lt, NORM_ROWS):
                r1 = min(r0 + NORM_ROWS, lt)
                x2 = x_ref[b, r0:r1, :] + gt * acc_ref[b * lt + r0:b * lt + r1, :]
                if final:
                    ms = jnp.mean(x2 * x2, axis=-1, keepdims=True)
                    x2 = x2 * lax.rsqrt(ms + EPS) * gfin
                x2_ref[b, r0:r1, :] = x2


def _ffn(x, sc, sh, gt, g, gfin, prev, w_up, cw, cb, w_down, *, bt, lt, fc, final):
    B, L, D = x.shape
    F = w_down.shape[0]
    nf = F // fc
    row = pl.BlockSpec((bt, lt, D), lambda b, t, f: (b, t, 0))
    mod = pl.BlockSpec((bt, 1, D), lambda b, t, f: (b, 0, 0))
    vec = pl.BlockSpec((1, D), lambda b, t, f: (0, 0))
    return pl.pallas_call(
        functools.partial(_ffn_kernel, final=final),
        grid=(B // bt, L // lt, nf),
        in_specs=[row, mod, mod, mod, vec, vec,
                  pl.BlockSpec((bt, CONV_W - 1, fc), lambda b, t, f: (b, 0, f)),
                  pl.BlockSpec((D, fc), lambda b, t, f: (0, f)),
                  pl.BlockSpec((D, fc), lambda b, t, f: (0, nf + f)),
                  pl.BlockSpec((CONV_W, fc), lambda b, t, f: (0, f)),
                  pl.BlockSpec((1, fc), lambda b, t, f: (0, f)),
                  pl.BlockSpec((fc, D), lambda b, t, f: (f, 0))],
        out_specs=[row, pl.BlockSpec((bt, nf, CONV_W - 1, fc), lambda b, t, f: (b, 0, 0, 0))],
        out_shape=[jax.ShapeDtypeStruct((B, L, D), F32),
                   jax.ShapeDtypeStruct((B, nf, CONV_W - 1, fc), F32)],
        scratch_shapes=[pltpu.VMEM((bt * lt, D), BF16),
                        pltpu.VMEM((bt * lt, D), F32),
                        pltpu.VMEM((bt, HALO + lt, fc), F32),
                        pltpu.VMEM((nf, bt, CONV_W - 1, fc), F32),
                        pltpu.VMEM((bt, lt, fc), F32),
                        pltpu.VMEM((bt * lt, fc), BF16)],
        compiler_params=_cparams(("arbitrary", "arbitrary", "arbitrary")),
        name="conv_ffn",
    )(x, sc, sh, gt, g, gfin, prev, w_up, w_up, cw, cb, w_down)


CACHE_BLOCK = 256


def _cache_kernel(c_ref, n_ref, o_ref, *, feature_major):
    blk = CACHE_BLOCK
    P = c_ref.shape[0]
    lkp = o_ref.shape[1] if feature_major else o_ref.shape[0]

    def put(r, v):
        rows = slice(r * blk, (r + 1) * blk)
        if feature_major:
            o_ref[:, rows] = v.T.astype(o_ref.dtype)
        else:
            o_ref[rows, :] = v.astype(o_ref.dtype)

    for r in range(P // blk):
        c = c_ref[r * blk:(r + 1) * blk]
        put(r, pltpu.einshape("thd->t(hd)", c) if c.ndim == 3 else c)
    new = n_ref[...].astype(F32)
    put(P // blk, jnp.concatenate(
        [new, jnp.zeros((blk - new.shape[0], new.shape[1]), F32)], axis=0))
    for r in range(P // blk + 1, lkp // blk):
        put(r, jnp.zeros((blk, new.shape[1]), F32))


def _cache_keys(cache, new, lkp, feature_major):
    B, P = cache.shape[:2]
    L, W = new.shape[1:]
    blk = CACHE_BLOCK
    assert P % blk == 0 and lkp % blk == 0 and L <= blk and lkp >= P + L
    cblock = (None,) + cache.shape[1:]
    cmap = (lambda b: (b, 0, 0, 0)) if cache.ndim == 4 else (lambda b: (b, 0, 0))
    oshape = (W, lkp) if feature_major else (lkp, W)
    return pl.pallas_call(
        functools.partial(_cache_kernel, feature_major=feature_major),
        grid=(B,),
        in_specs=[pl.BlockSpec(cblock, cmap),
                  pl.BlockSpec((None, L, W), lambda b: (b, 0, 0))],
        out_specs=pl.BlockSpec((None,) + oshape, lambda b: (b, 0, 0)),
        out_shape=jax.ShapeDtypeStruct((B,) + oshape, BF16),
        compiler_params=_cparams(("arbitrary",)),
        name="cache_keys",
    )(cache, new)


def _pad_cols(w, mult):
    n = w.shape[1]
    npad = -(-n // mult) * mult
    return jnp.pad(w, ((0, 0), (0, npad - n)))


def _tiles(B, L, rows):
    if L >= rows:
        assert L % rows == 0
        return 1, rows
    assert rows % L == 0 and B % (rows // L) == 0
    return rows // L, L


def _layer(x, mod, past, wts, *, last, g_final):
    (g_mix, w_in_sb, w_in_sa, w_in_ix, w_gate, w_br_sb, w_br_sa, w_out, rel_table, g_ffn, w_up,
     conv_w, conv_b, w_down) = wts
    B, L, D = x.shape
    sh1, sc1, gt1, sh2, sc2, gt2 = (m[:, None, :] for m in jnp.split(mod, 6, axis=-1))
    past_len = 0 if past is None else past[0].shape[1]

    wn = W_SB
    bt, lt = _tiles(B, L, 512)
    fm = bt == 1
    qlay = "feature" if fm else "row"
    call = functools.partial(_proj, x, sc1, sh1, g_mix, bt=bt, lt=lt)
    q_sb, k_sb, kb_sb, v_sb, vb_sb = call(
        w_in_sb, [_Out(0, 0, wn, BF16), _Out(1, 0, wn, F32, "heads"), _Out(1, 0, wn, BF16),
                  _Out(2, 0, wn, F32, "heads"), _Out(2, 0, wn, BF16)], tn=wn, name="proj_sb")
    q_sa, k_sa, kb_sa, v_sa, vb_sa = call(
        w_in_sa, [_Out(0, 0, wn, BF16, qlay, scale=ATT_SCALE * LOG2E),
                  _Out(1, 0, wn, F32, "heads"), _Out(1, 0, wn, BF16),
                  _Out(2, 0, wn, F32, "heads"), _Out(2, 0, wn, BF16, qlay)],
        tn=wn, name="proj_sa")
    w_ix_out = (_Out(1, 0, LANE, F32, "feature", (D_IDX, D_IDX + H_IDX)) if fm
                else _Out(1, D_IDX, D_IDX + H_IDX, F32))
    q_ix, k_ix, kb_ix, w_ix = call(
        w_in_ix, [_Out(0, 0, wn, BF16, qlay), _Out(1, 0, D_IDX, F32), _Out(1, 0, D_IDX, BF16),
                  w_ix_out], tn=wn, name="proj_ix")
    bt, lt = _tiles(B, L, 1024)
    gate, = _proj(x, sc1, sh1, g_mix, w_gate, [_Out(None, 0, 512, F32)], bt=bt, lt=lt, tn=512,
                  sigmoid=True, name="proj_gate")
    if not fm:
        q_sa, q_ix, w_ix = (jnp.swapaxes(a, 1, 2) for a in (q_sa, q_ix, w_ix))

    tq = min(256, L)
    tk_sb = 256
    tk_sa = 512 if L >= 512 else 256
    l_keys = past_len + L

    def keys_of(new, old, tk, feature_major=False):
        if old is None:
            assert l_keys % tk == 0
            return new
        assert not fm
        return _cache_keys(old, new, -(-l_keys // tk) * tk, feature_major)

    old = (None,) * 5 if past is None else past[:5]
    o_sb = _sb_attention(q_sb, keys_of(kb_sb, old[0], tk_sb), keys_of(vb_sb, old[1], tk_sb),
                         tq=tq, tk=tk_sb, past_len=past_len)
    assert fm or past is not None
    vt = keys_of(vb_sa, old[3], tk_sa, True)
    o_sa_t = _dsa_attention(q_ix, w_ix, q_sa, keys_of(kb_ix, old[4], tk_sa),
                            keys_of(kb_sa, old[2], tk_sa), vt, rel_table,
                            tq=tq, tk=tk_sa, past_len=past_len)
    o_sa = jnp.swapaxes(o_sa_t, 1, 2)

    bt, lt = _tiles(B, L, 256)
    x1 = _merge(o_sb, o_sa, gate, x, gt1, w_br_sb, w_br_sa, w_out, bt=bt, lt=lt)

    prev = jnp.zeros((B, CONV_W - 1, w_down.shape[0]), F32) if past is None else past[5]
    bt, lt = _tiles(B, L, 512)
    x2, conv_state = _ffn(x1, sc2, sh2, gt2, g_ffn, g_final, prev, w_up, conv_w, conv_b,
                          w_down, bt=bt, lt=lt, fc=512, final=last)
    conv_state = jnp.swapaxes(conv_state, 1, 2).reshape(B, CONV_W - 1, -1)
    return x2, (k_sb, v_sb, k_sa, v_sa, k_ix, conv_state)


def kernel(x_prompt, x_sample, cache_sb_k, cache_sb_v, cache_sa_k, cache_sa_v, cache_idx_k,
           state_ffn_conv, c_prompt, c_sample, w_ada, b_ada, g_mix, w_in, w_gate, w_br_sb,
           w_br_sa, w_out, rel_table, g_ffn, w_up, conv_w, conv_b, w_down, g_final):
    depth = w_ada.shape[0]
    nbp = c_prompt.shape[0]
    nbs = c_sample.shape[0]
    rows = -(-(nbp + nbs) // 8) * 8
    c_all = jnp.concatenate([c_prompt, c_sample,
                             jnp.zeros((rows - nbp - nbs, c_prompt.shape[1]), F32)], axis=0)
    xp, xs = x_prompt, x_sample
    new_p, new_s = [], []
    for l in range(depth):
        mod = _adaln(c_all, w_ada[l], b_ada[l][None, :])
        wb = w_in[l].astype(BF16)
        wts = (g_mix[l][None, :], wb[:, :COL_Q_SA], wb[:, COL_Q_SA:COL_Q_IX],
               _pad_cols(wb[:, COL_Q_IX:], 2 * W_SB), w_gate[l].astype(BF16),
               w_br_sb[l].astype(BF16), w_br_sa[l].astype(BF16), w_out[l].astype(BF16),
               rel_table, g_ffn[l][None, :], w_up[l].astype(BF16), conv_w[l],
               conv_b[l][None, :], w_down[l].astype(BF16))
        gfin = g_final[None, :]
        past = (cache_sb_k[l], cache_sb_v[l], cache_sa_k[l], cache_sa_v[l], cache_idx_k[l],
                state_ffn_conv[l])
        xp, sp = _layer(xp, mod[:nbp], None, wts, last=l == depth - 1, g_final=gfin)
        xs, ss = _layer(xs, mod[nbp:nbp + nbs], past, wts, last=l == depth - 1, g_final=gfin)
        new_p.append(sp)
        new_s.append(ss)
    stack = lambda states, n: jnp.stack([s[n] for s in states])
    return ((xp, xs) + tuple(stack(new_p, n) for n in range(6))
            + tuple(stack(new_s, n) for n in range(6)))
```

```python
import functools

import numpy as np
import jax
import jax.numpy as jnp
from jax import lax
from jax.experimental import pallas as pl
from jax.experimental.pallas import tpu as pltpu

F32 = jnp.float32
BF16 = jnp.bfloat16
I32 = jnp.int32

CHUNK = 64
D_HEAD = 128
H_SB = 8
H_SA = 8
W_SB = H_SB * D_HEAD
W_SA = H_SA * D_HEAD
H_IDX = 16
D_IDX = 64
TOPK_MAX = 256
N_BUCKETS = 32
REL_MAX_DIST = 1024
CONV_W = 3
EPS = 1e-6

COL_Q_SB = 0
COL_K_SB = W_SB
COL_V_SB = 2 * W_SB
COL_Q_SA = 3 * W_SB
COL_K_SA = 3 * W_SB + W_SA
COL_V_SA = 3 * W_SB + 2 * W_SA
COL_Q_IX = 3 * W_SB + 3 * W_SA
COL_K_IX = COL_Q_IX + H_IDX * D_IDX
COL_W_IX = COL_K_IX + D_IDX
IN_COLS = COL_W_IX + H_IDX

LANE = 128
VMEM_LIMIT = 56 * 1024 * 1024

ATT_SCALE = D_HEAD ** -0.5
LOG2E = 1.4426950408889634
IDX_SCALE = (D_IDX ** -0.5) * (H_IDX ** -0.5)
NEG = -1e30
INT_MIN = -2 ** 31
SB_DEAD = -104.0


def _cparams(sem):
    return pltpu.CompilerParams(dimension_semantics=sem, vmem_limit_bytes=VMEM_LIMIT)


def _adaln_kernel(c_ref, w_ref, b_ref, o_ref):
    c = c_ref[...]
    a = c * jax.nn.sigmoid(c)
    o_ref[...] = jnp.dot(a.astype(BF16), w_ref[...].astype(BF16),
                         preferred_element_type=F32) + b_ref[...]


def _adaln(c, w, b):
    R, D = c.shape
    N = w.shape[1]
    tn = 1024
    return pl.pallas_call(
        _adaln_kernel,
        grid=(N // tn,),
        in_specs=[pl.BlockSpec((R, D), lambda n: (0, 0)),
                  pl.BlockSpec((D, tn), lambda n: (0, n)),
                  pl.BlockSpec((1, tn), lambda n: (0, n))],
        out_specs=pl.BlockSpec((R, tn), lambda n: (0, n)),
        out_shape=jax.ShapeDtypeStruct((R, N), F32),
        compiler_params=_cparams(("arbitrary",)),
        name="adaln",
    )(c, w, b)


NORM_ROWS = 16


def _norm_mod_store(x_ref, g_ref, sc_ref, sh_ref, h_ref):
    bt, lt, _ = x_ref.shape
    g = g_ref[...]
    for b in range(bt):
        sc1 = 1.0 + sc_ref[b]
        sh = sh_ref[b]
        for r0 in range(0, lt, NORM_ROWS):
            r1 = min(r0 + NORM_ROWS, lt)
            x = x_ref[b, r0:r1, :]
            ms = jnp.mean(x * x, axis=-1, keepdims=True)
            y = x * lax.rsqrt(ms + EPS) * g
            h_ref[b * lt + r0:b * lt + r1, :] = (y * sc1 + sh).astype(h_ref.dtype)


class _Out(tuple):
    __slots__ = ()

    def __new__(cls, tile, lo, hi, dtype, layout="row", rows=None, scale=None):
        return tuple.__new__(cls, (tile, lo, hi, rows or (0, hi - lo), layout, dtype, scale))


def _proj_kernel(x_ref, sc_ref, sh_ref, g_ref, w_ref, *rest, plan, sigmoid):
    out_refs, h_ref = rest[:-1], rest[-1]
    bt, lt, D = x_ref.shape
    n = pl.program_id(2)

    @pl.when(n == 0)
    def _():
        _norm_mod_store(x_ref, g_ref, sc_ref, sh_ref, h_ref)

    r = jnp.dot(h_ref[...], w_ref[...], preferred_element_type=F32)
    if sigmoid:
        r = jnp.concatenate([jax.nn.sigmoid(r[r0:r0 + NORM_ROWS]) for r0 in
                             range(0, bt * lt, NORM_ROWS)], axis=0)
    for (tile, lo, hi, rows, layout, _, scale), o_ref in zip(plan, out_refs):
        def emit(o_ref=o_ref, lo=lo, hi=hi, rows=rows, layout=layout, scale=scale):
            v = r[:, lo:hi]
            if scale is not None:
                v = v * scale
            if layout == "feature":
                o_ref[...] = v.T[rows[0]:rows[1], :].astype(o_ref.dtype)
            elif layout == "heads":
                nh = (hi - lo) // D_HEAD
                v = pltpu.einshape("t(hd)->thd", v, h=nh)
                o_ref[...] = v.reshape(bt, lt, nh, D_HEAD).astype(o_ref.dtype)
            else:
                o_ref[...] = v.reshape(bt, lt, hi - lo).astype(o_ref.dtype)

        if tile is None:
            emit()
        else:
            pl.when(n == tile)(emit)


def _proj(x, sc, sh, g, w, plan, *, bt, lt, tn, sigmoid=False, name):
    B, L, D = x.shape
    N = w.shape[1]
    out_specs, out_shape = [], []
    for tile, lo, hi, rows, layout, dtype, _ in plan:
        if tile is None:
            out_specs.append(pl.BlockSpec((bt, lt, tn), lambda b, t, n: (b, t, n)))
            out_shape.append(jax.ShapeDtypeStruct((B, L, N), dtype))
        elif layout == "heads":
            nh = (hi - lo) // D_HEAD
            out_specs.append(pl.BlockSpec((bt, lt, nh, D_HEAD), lambda b, t, n: (b, t, 0, 0)))
            out_shape.append(jax.ShapeDtypeStruct((B, L, nh, D_HEAD), dtype))
        elif layout == "feature":
            assert bt == 1
            width = rows[1] - rows[0]
            out_specs.append(pl.BlockSpec((None, width, lt), lambda b, t, n: (b, 0, t)))
            out_shape.append(jax.ShapeDtypeStruct((B, width, L), dtype))
        else:
            out_specs.append(pl.BlockSpec((bt, lt, hi - lo), lambda b, t, n: (b, t, 0)))
            out_shape.append(jax.ShapeDtypeStruct((B, L, hi - lo), dtype))
    return pl.pallas_call(
        functools.partial(_proj_kernel, plan=tuple(plan), sigmoid=sigmoid),
        grid=(B // bt, L // lt, N // tn),
        in_specs=[pl.BlockSpec((bt, lt, D), lambda b, t, n: (b, t, 0)),
                  pl.BlockSpec((bt, 1, D), lambda b, t, n: (b, 0, 0)),
                  pl.BlockSpec((bt, 1, D), lambda b, t, n: (b, 0, 0)),
                  pl.BlockSpec((1, D), lambda b, t, n: (0, 0)),
                  pl.BlockSpec((D, tn), lambda b, t, n: (0, n))],
        out_specs=out_specs,
        out_shape=out_shape,
        scratch_shapes=[pltpu.VMEM((bt * lt, D), BF16)],
        compiler_params=_cparams(("arbitrary", "arbitrary", "arbitrary")),
        name=name,
    )(x, sc, sh, g, w)


def _sb_kernel(q_ref, k_ref, v_ref, u_ref, o_ref, *, tq, tk, past_len, hp):
    i = pl.program_id(2)
    qpos0 = past_len + i * tq
    qpos = qpos0 + lax.broadcasted_iota(I32, (tq, 1), 0)
    j0 = (qpos0 + tq - 2) // tk
    u = u_ref[...]

    def cond(carry):
        j, go, _, _ = carry
        return jnp.logical_and(j >= 0, go)

    def body(carry):
        j, _, cums, accs = carry
        off = pl.multiple_of(j * tk, tk)
        kpos = j * tk + lax.broadcasted_iota(I32, (1, tk), 1)
        mask = kpos < qpos
        new_cums, new_accs = [], []
        for h in range(hp):
            cs = slice(h * D_HEAD, (h + 1) * D_HEAD)
            z = lax.dot_general(q_ref[:, cs], k_ref[pl.ds(off, tk), cs], _NT,
                                preferred_element_type=F32) * (ATT_SCALE * LOG2E)
            t = jnp.log2(1.0 + jnp.exp2(-jnp.abs(z)))
            lk = jnp.where(mask, -(jnp.maximum(z, 0.0) + t), 0.0)
            hi = lk.astype(BF16)
            lo = (lk - hi.astype(F32)).astype(BF16)
            after = cums[h] + (jnp.dot(hi, u, preferred_element_type=F32)
                               + jnp.dot(lo, u, preferred_element_type=F32))
            lsz = jnp.minimum(z, 0.0) - t
            a = jnp.where(mask, jnp.exp2(lsz + after), 0.0)
            new_accs.append(accs[h] + jnp.dot(a.astype(BF16), v_ref[pl.ds(off, tk), cs],
                                              preferred_element_type=F32))
            new_cums.append(cums[h] + jnp.sum(lk, axis=1, keepdims=True))
        top = new_cums[0]
        for c in new_cums[1:]:
            top = jnp.maximum(top, c)
        return j - 1, jnp.max(top) > SB_DEAD * LOG2E, tuple(new_cums), tuple(new_accs)

    init = (j0, jnp.bool_(True), (jnp.zeros((tq, 1), F32),) * hp,
            (jnp.zeros((tq, D_HEAD), F32),) * hp)
    _, _, _, accs = lax.while_loop(cond, body, init)
    for h in range(hp):
        o_ref[:, h * D_HEAD:(h + 1) * D_HEAD] = accs[h].astype(o_ref.dtype)


def _sb_attention(q, k_all, v_all, *, tq, tk, past_len, hp=4):
    B, L, _ = q.shape
    Lkp = k_all.shape[1]
    r = np.arange(tk)
    u = jnp.asarray((r[:, None] > r[None, :]).astype(np.float32), BF16)
    wp = hp * D_HEAD
    kv_mode = dict(pipeline_mode=pl.Buffered(1)) if L // tq > 1 else {}
    return pl.pallas_call(
        functools.partial(_sb_kernel, tq=tq, tk=tk, past_len=past_len, hp=hp),
        grid=(B, H_SB // hp, L // tq),
        in_specs=[pl.BlockSpec((None, tq, wp), lambda b, h, i: (b, i, h)),
                  pl.BlockSpec((None, Lkp, wp), lambda b, h, i: (b, 0, h), **kv_mode),
                  pl.BlockSpec((None, Lkp, wp), lambda b, h, i: (b, 0, h), **kv_mode),
                  pl.BlockSpec((tk, tk), lambda b, h, i: (0, 0))],
        out_specs=pl.BlockSpec((None, tq, wp), lambda b, h, i: (b, i, h)),
        out_shape=jax.ShapeDtypeStruct((B, L, W_SB), BF16),
        compiler_params=_cparams(("arbitrary", "arbitrary", "arbitrary")),
        name="sb_attn",
    )(q, k_all, v_all, u)


def _bucket_edges():
    nb = N_BUCKETS // 2
    max_exact = nb // 2

    def bucket(rel):
        n = abs(rel)
        if n < max_exact:
            v = n
        else:
            v = max_exact + int(np.log(np.float32(n) / max_exact)
                                / np.log(REL_MAX_DIST / max_exact) * (nb - max_exact))
            v = min(v, nb - 1)
        return (nb if rel > 0 else 0) + v

    lo = -4 * REL_MAX_DIST
    assert bucket(lo) == nb - 1
    edges = []
    prev = bucket(lo)
    for rel in range(lo + 1, CHUNK):
        bk = bucket(rel)
        if bk != prev:
            edges.append((rel, prev))
            prev = bk
    edges.append((CHUNK, prev))
    return edges


_EDGES = _bucket_edges()
_FAR_BUCKET = _EDGES[0][1]
_NEAR_REL = _EDGES[0][0]


def _bias_kernel(d_ref, rel_ref, o_ref):
    tk, tq = o_ref.shape
    h = pl.program_id(1)
    rel = (lax.broadcasted_iota(I32, (tk, tq), 0) - lax.broadcasted_iota(I32, (tk, tq), 1)
           - d_ref[pl.program_id(0)])
    tile = jnp.full((tk, tq), rel_ref[_EDGES[-1][1], h], F32)
    for edge, bk in reversed(_EDGES[:-1]):
        tile = jnp.where(rel < edge, rel_ref[bk, h], tile)
    o_ref[...] = (tile - rel_ref[_FAR_BUCKET, h]) * LOG2E


def _bias_tiles(ds, rel_table, *, tq, tk):
    return pl.pallas_call(
        _bias_kernel,
        grid=(len(ds), H_SA),
        in_specs=[pl.BlockSpec(memory_space=pltpu.SMEM), pl.BlockSpec(memory_space=pltpu.SMEM)],
        out_specs=pl.BlockSpec((None, None, tk, tq), lambda n, h: (n, h, 0, 0)),
        out_shape=jax.ShapeDtypeStruct((len(ds), H_SA, tk, tq), F32),
        compiler_params=_cparams(("arbitrary", "arbitrary")),
        name="bias_tiles",
    )(jnp.asarray(np.asarray(ds, np.int32)), rel_table)


_NT = (((1,), (1,)), ((), ()))


def _dsa_kernel(itab, jtab, ftab, ltab, ntab, nbtab, fartab,
                qit_ref, wt_ref, qt_ref, ki_ref, k_ref, vt_ref, bias_ref, o_ref,
                keys_ref, half_ref, thr_ref, m_ref, l_ref, acc_ref, madd_ref, s_ref, p_ref,
                *, tq, tk, past_len, sub, csub):
    p = pl.program_id(1)
    i = itab[p]
    j = jtab[p]
    qpos = past_len + i * tq + lax.broadcasted_iota(I32, (1, tq), 1)
    lim = (qpos // CHUNK + 1) * CHUNK

    @pl.when(ftab[p] == 1)
    def _():
        wt = wt_ref[...] * IDX_SCALE

        def score_blk(c, carry):
            off = pl.multiple_of(c * sub, sub)
            kib = ki_ref[pl.ds(off, sub), :]
            s = jnp.zeros((sub, tq), F32)
            for h in range(H_IDX):
                sh = jnp.dot(kib, qit_ref[h * D_IDX:(h + 1) * D_IDX, :],
                             preferred_element_type=F32)
                s = s + wt[h:h + 1, :] * jnp.maximum(sh, 0.0)
            bits = pltpu.bitcast(s, I32)
            key = jnp.where(bits < 0, bits ^ jnp.int32(0x7FFFFFFF), bits)
            kpos = c * sub + lax.broadcasted_iota(I32, (sub, 1), 0)
            key = jnp.where(kpos < lim, key, jnp.int32(INT_MIN))
            keys_ref[pl.ds(off, sub), :] = key
            half_ref[pl.ds(off, sub), :] = (key >> 16).astype(jnp.int16)
            return carry

        per = tk // sub

        def score_grp(g, carry):
            for u in range(per):
                score_blk(g * per + u, carry)
            return carry

        lax.fori_loop(0, nbtab[p], score_grp, 0)

        need = jnp.minimum(TOPK_MAX, lim)
        nchunk = nbtab[p] * (tk // csub)
        i16_min = -2 ** 15

        def count_ge(cand):
            c16 = jnp.broadcast_to(cand, (16, tq)).astype(jnp.int16)

            def cnt_blk(c, acc):
                kb = half_ref[pl.ds(pl.multiple_of(c * csub, csub), csub), :]
                parts = [jnp.where(kb[16 * r:16 * (r + 1), :] >= c16, jnp.int16(1), jnp.int16(0))
                         for r in range(csub // 16)]
                while len(parts) > 1:
                    parts = [a + b for a, b in zip(parts[::2], parts[1::2])]
                return acc + parts[0]

            acc = lax.fori_loop(0, nchunk, cnt_blk, jnp.zeros((16, tq), jnp.int16))
            return jnp.sum(acc.astype(I32), axis=0, keepdims=True)

        def kth_half(want):
            def bit_step(it, tpre):
                bitv = lax.shift_left(jnp.int32(1), 15 - it)
                hit = count_ge((tpre | bitv) + i16_min) >= want
                return jnp.where(hit, tpre | bitv, tpre)

            return lax.fori_loop(0, 16, bit_step, jnp.zeros((1, tq), I32))

        hi = kth_half(need) + i16_min
        above = jnp.where(hi == 2 ** 15 - 1, 0, count_ge(jnp.minimum(hi + 1, 2 ** 15 - 1)))

        def low_halves(c, carry):
            rows = pl.ds(pl.multiple_of(c * csub, csub), csub)
            k32 = keys_ref[rows, :]
            lo = (k32 & 0xFFFF) + i16_min
            half_ref[rows, :] = jnp.where((k32 >> 16) == hi, lo, i16_min).astype(jnp.int16)
            return carry

        lax.fori_loop(0, nchunk, low_halves, 0)
        thr = lax.shift_left(hi, 16) | kth_half(need - above)
        thr_ref[...] = thr

        def rows_of(c):
            return c * csub + lax.broadcasted_iota(I32, (csub, 1), 0)

        def count_where(pred):
            def blk(c, acc):
                kb = keys_ref[pl.ds(pl.multiple_of(c * csub, csub), csub), :]
                hit = jnp.where(pred(kb, rows_of(c)), 1.0, 0.0)
                return acc + jnp.sum(hit.reshape(csub // 8, 8, tq), axis=0)

            acc = lax.fori_loop(0, nchunk, blk, jnp.zeros((8, tq), F32))
            return jnp.sum(acc, axis=0, keepdims=True).astype(I32)

        excess = count_where(lambda kb, _: kb >= thr) - need

        @pl.when(jnp.max(excess.astype(F32)) > 0.0)
        def _():
            keep = count_where(lambda kb, _: kb == thr) - excess
            nbits = keys_ref.shape[0].bit_length()

            def bit_step(it, bound):
                cand = bound | lax.shift_left(jnp.int32(1), nbits - 1 - it)
                cnt = count_where(lambda kb, rows: jnp.where(rows < cand, kb, thr - 1) == thr)
                return jnp.where(cnt <= keep, cand, bound)

            bound = lax.fori_loop(0, nbits, bit_step, jnp.zeros((1, tq), I32))

            def demote(c, carry):
                rows = pl.ds(pl.multiple_of(c * csub, csub), csub)
                kb = keys_ref[rows, :]
                late_tie = jnp.where(rows_of(c) >= bound, kb, thr - 1) == thr
                keys_ref[rows, :] = jnp.where(late_tie, thr - 1, kb)
                return carry

            lax.fori_loop(0, nchunk, demote, 0)

        m_ref[...] = jnp.full(m_ref.shape, NEG, F32)
        l_ref[...] = jnp.zeros(l_ref.shape, F32)
        acc_ref[...] = jnp.zeros(acc_ref.shape, F32)

    thr = thr_ref[...]
    for c in range(tk // sub):
        rows = pl.ds(pl.multiple_of(j * tk + c * sub, sub), sub)
        madd_ref[c * sub:(c + 1) * sub, :] = jnp.where(keys_ref[rows, :] >= thr, 0.0, NEG)

    def logits(h, near):
        cs = slice(h * D_HEAD, (h + 1) * D_HEAD)
        qh = qt_ref[cs, :]
        mx = jnp.full((8, tq), NEG, F32)
        for c in range(tk // sub):
            rs = slice(c * sub, (c + 1) * sub)
            s = jnp.dot(k_ref[rs, cs], qh, preferred_element_type=F32)
            extra = bias_ref[h, rs, :] + madd_ref[rs, :] if near else madd_ref[rs, :]
            s = s + extra
            s_ref[h % s_ref.shape[0], rs, :] = s
            mx = jnp.maximum(mx, jnp.max(s.reshape(sub // 8, 8, tq), axis=0))
        return jnp.max(mx, axis=0, keepdims=True)

    def attend(near):
        def weighted_values(h, alpha):
            cs = slice(h * D_HEAD, (h + 1) * D_HEAD)
            vt_ones = jnp.concatenate([vt_ref[cs, :], jnp.ones((16, tk), BF16)], axis=0)
            pv = jnp.dot(vt_ones, p_ref[h % 2], preferred_element_type=F32)
            l_ref[h:h + 1, :] = alpha * l_ref[h:h + 1, :] + pv[D_HEAD:D_HEAD + 1, :]
            acc_ref[cs, :] = alpha * acc_ref[cs, :] + pv[:D_HEAD, :]

        ahead = s_ref.shape[0] - 1
        smaxes = [logits(h, near) for h in range(ahead)]
        alpha_prev = None
        for h in range(H_SA):
            m_prev = m_ref[h:h + 1, :]
            m_new = jnp.maximum(m_prev, smaxes[h])
            if h + ahead < H_SA:
                smaxes.append(logits(h + ahead, near))
            alpha = jnp.exp2(m_prev - m_new)
            for c in range(tk // sub):
                rs = slice(c * sub, (c + 1) * sub)
                p_ref[h % 2, rs, :] = jnp.exp2(
                    s_ref[h % (ahead + 1), rs, :] - m_new).astype(BF16)
            m_ref[h:h + 1, :] = m_new
            if h > 0:
                weighted_values(h - 1, alpha_prev)
            alpha_prev = alpha
        weighted_values(H_SA - 1, alpha_prev)

    pl.when(fartab[p] == 0)(functools.partial(attend, True))
    pl.when(fartab[p] == 1)(functools.partial(attend, False))

    @pl.when(ltab[p] == 1)
    def _():
        for h in range(H_SA):
            cs = slice(h * D_HEAD, (h + 1) * D_HEAD)
            o_ref[cs, :] = (acc_ref[cs, :] / l_ref[h:h + 1, :]).astype(o_ref.dtype)


def _dsa_attention(qit, wt, qt, ki_all, k_all, vt_all, rel_table, *, tq, tk, past_len):
    B, _, L = qt.shape
    Lkp = k_all.shape[1]
    assert Lkp % tk == 0 and L % tq == 0
    nq = L // tq
    it, jt, ft, lt_, nt, nbt, fart = [], [], [], [], [], [], []
    near_ds = []
    for i in range(nq):
        qpos0 = past_len + i * tq
        lim_max = ((qpos0 + tq - 1) // CHUNK + 1) * CHUNK
        jmax = (lim_max - 1) // tk
        for j in range(jmax + 1):
            d = qpos0 - j * tk
            if (tk - 1) - d >= _NEAR_REL:
                if d not in near_ds:
                    near_ds.append(d)
                nt.append(near_ds.index(d))
                fart.append(0)
            else:
                nt.append(nt[-1] if nt else 0)
                fart.append(1)
            it.append(i); jt.append(j); ft.append(int(j == 0)); lt_.append(int(j == jmax))
            nbt.append(jmax + 1)
    bias = _bias_tiles(near_ds, rel_table, tq=tq, tk=tk)
    tabs = [jnp.asarray(np.asarray(t, np.int32)) for t in (it, jt, ft, lt_, nt, nbt, fart)]
    grid_spec = pltpu.PrefetchScalarGridSpec(
        num_scalar_prefetch=7,
        grid=(B, len(it)),
        in_specs=[
            pl.BlockSpec((None, H_IDX * D_IDX, tq), lambda b, p, it, jt, *_: (b, 0, it[p])),
            pl.BlockSpec((None, H_IDX, tq), lambda b, p, it, jt, *_: (b, 0, it[p])),
            pl.BlockSpec((None, W_SA, tq), lambda b, p, it, jt, *_: (b, 0, it[p])),
            pl.BlockSpec((None, Lkp, D_IDX), lambda b, p, it, jt, *_: (b, 0, 0),
                         pipeline_mode=pl.Buffered(1)),
            pl.BlockSpec((None, tk, W_SA), lambda b, p, it, jt, *_: (b, jt[p], 0)),
            pl.BlockSpec((None, W_SA, tk), lambda b, p, it, jt, *_: (b, 0, jt[p])),
            pl.BlockSpec((None, H_SA, tk, tq),
                         lambda b, p, it, jt, ft, lt, nt, *_: (nt[p], 0, 0, 0)),
        ],
        out_specs=pl.BlockSpec((None, W_SA, tq), lambda b, p, it, jt, *_: (b, 0, it[p])),
        scratch_shapes=[
            pltpu.VMEM((Lkp, tq), I32),
            pltpu.VMEM((Lkp, tq), jnp.int16),
            pltpu.VMEM((1, tq), I32),
            pltpu.VMEM((H_SA, tq), F32),
            pltpu.VMEM((H_SA, tq), F32),
            pltpu.VMEM((W_SA, tq), F32),
            pltpu.VMEM((tk, tq), F32),
            pltpu.VMEM((3, tk, tq), F32),
            pltpu.VMEM((2, tk, tq), BF16),
        ],
    )
    return pl.pallas_call(
        functools.partial(_dsa_kernel, tq=tq, tk=tk, past_len=past_len,
                          sub=min(tk, 128), csub=min(tk, 512)),
        grid_spec=grid_spec,
        out_shape=jax.ShapeDtypeStruct((B, W_SA, L), BF16),
        compiler_params=_cparams(("arbitrary", "arbitrary")),
        name="dsa_attn",
    )(*tabs, qit, wt, qt, ki_all, k_all, vt_all, bias)


def _merge_kernel(osb_ref, osa_ref, g_ref, x_ref, gt_ref, wsb_ref, wsa_ref, wo_ref, o_ref):
    bt, lt, D = x_ref.shape
    tm = bt * lt
    a = jnp.dot(osb_ref[...].reshape(tm, W_SB), wsb_ref[...], preferred_element_type=F32)
    c = jnp.dot(osa_ref[...].reshape(tm, W_SA), wsa_ref[...], preferred_element_type=F32)
    g = g_ref[...].reshape(tm, 2 * D)
    merged = g[:, :D] * a + g[:, D:] * c
    y = jnp.dot(merged.astype(BF16), wo_ref[...], preferred_element_type=F32)
    o_ref[...] = x_ref[...] + gt_ref[...] * y.reshape(bt, lt, D)


def _merge(osb, osa, g, x, gt, wsb, wsa, wo, *, bt, lt):
    B, L, D = x.shape
    row = lambda w: pl.BlockSpec((bt, lt, w), lambda b, t: (b, t, 0))
    full = lambda a: pl.BlockSpec(a.shape, lambda b, t: (0, 0), pipeline_mode=pl.Buffered(1))
    return pl.pallas_call(
        _merge_kernel,
        grid=(B // bt, L // lt),
        in_specs=[row(W_SB), row(W_SA), row(2 * D), row(D),
                  pl.BlockSpec((bt, 1, D), lambda b, t: (b, 0, 0)),
                  full(wsb), full(wsa), full(wo)],
        out_specs=row(D),
        out_shape=jax.ShapeDtypeStruct((B, L, D), F32),
        compiler_params=_cparams(("arbitrary", "arbitrary")),
        name="merge_out",
    )(osb, osa, g, x, gt, wsb, wsa, wo)


HALO = 8
FFN_ROWS = 32


def _ffn_kernel(x_ref, sc_ref, sh_ref, gt_ref, g_ref, gfin_ref, prev_ref, wg_ref, wv_ref,
                cw_ref, cb_ref, wd_ref, x2_ref, st_ref,
                h_ref, acc_ref, ext_ref, carry_ref, uv_ref, act_ref, *, final):
    bt, lt, D = x_ref.shape
    tm = bt * lt
    fc = wg_ref.shape[1]
    t = pl.program_id(1)
    f = pl.program_id(2)

    @pl.when(f == 0)
    def _():
        _norm_mod_store(x_ref, g_ref, sc_ref, sh_ref, h_ref)
        acc_ref[...] = jnp.zeros(acc_ref.shape, F32)

    h = h_ref[...]

    @pl.when(t == 0)
    def _():
        ext_ref[:, HALO - 2:HALO, :] = prev_ref[...]

    @pl.when(t > 0)
    def _():
        ext_ref[:, HALO - 2:HALO, :] = carry_ref[f]

    ext_ref[:, HALO:HALO + lt, :] = jnp.dot(
        h, wg_ref[...], preferred_element_type=F32).reshape(bt, lt, fc)
    uv_ref[...] = jnp.dot(h, wv_ref[...], preferred_element_type=F32).reshape(bt, lt, fc)
    tail = ext_ref[:, HALO + lt - 2:HALO + lt, :]
    carry_ref[f] = tail
    st_ref[:, f] = tail
    cw = cw_ref[...]
    cb = cb_ref[...]
    for b in range(bt):
        for r0 in range(0, lt, FFN_ROWS):
            r1 = min(r0 + FFN_ROWS, lt)
            e = [ext_ref[b, HALO - 2 + j + r0:HALO - 2 + j + r1, :] for j in range(CONV_W)]
            conv = cb + cw[0:1, :] * e[0] + cw[1:2, :] * e[1] + cw[2:3, :] * e[2]
            act = conv * jax.nn.sigmoid(conv) * uv_ref[b, r0:r1, :]
            act_ref[b * lt + r0:b * lt + r1, :] = act.astype(BF16)
    acc_ref[...] += jnp.dot(act_ref[...], wd_ref[...], preferred_element_type=F32)

    @pl.when(f == pl.num_programs(2) - 1)
    def _():
        gfin = gfin_ref[...]
        for b in range(bt):
            gt = gt_ref[b]
            for r0 in range(0, lt, NORM_ROWS):
                r1 = min(r0 + NORM_ROWS, lt)
                x2 = x_ref[b, r0:r1, :] + gt * acc_ref[b * lt + r0:b * lt + r1, :]
                if final:
                    ms = jnp.mean(x2 * x2, axis=-1, keepdims=True)
                    x2 = x2 * lax.rsqrt(ms + EPS) * gfin
                x2_ref[b, r0:r1, :] = x2


def _ffn(x, sc, sh, gt, g, gfin, prev, w_up, cw, cb, w_down, *, bt, lt, fc, final):
    B, L, D = x.shape
    F = w_down.shape[0]
    nf = F // fc
    row = pl.BlockSpec((bt, lt, D), lambda b, t, f: (b, t, 0))
    mod = pl.BlockSpec((bt, 1, D), lambda b, t, f: (b, 0, 0))
    vec = pl.BlockSpec((1, D), lambda b, t, f: (0, 0))
    return pl.pallas_call(
        functools.partial(_ffn_kernel, final=final),
        grid=(B // bt, L // lt, nf),
        in_specs=[row, mod, mod, mod, vec, vec,
                  pl.BlockSpec((bt, CONV_W - 1, fc), lambda b, t, f: (b, 0, f)),
                  pl.BlockSpec((D, fc), lambda b, t, f: (0, f)),
                  pl.BlockSpec((D, fc), lambda b, t, f: (0, nf + f)),
                  pl.BlockSpec((CONV_W, fc), lambda b, t, f: (0, f)),
                  pl.BlockSpec((1, fc), lambda b, t, f: (0, f)),
                  pl.BlockSpec((fc, D), lambda b, t, f: (f, 0))],
        out_specs=[row, pl.BlockSpec((bt, nf, CONV_W - 1, fc), lambda b, t, f: (b, 0, 0, 0))],
        out_shape=[jax.ShapeDtypeStruct((B, L, D), F32),
                   jax.ShapeDtypeStruct((B, nf, CONV_W - 1, fc), F32)],
        scratch_shapes=[pltpu.VMEM((bt * lt, D), BF16),
                        pltpu.VMEM((bt * lt, D), F32),
                        pltpu.VMEM((bt, HALO + lt, fc), F32),
                        pltpu.VMEM((nf, bt, CONV_W - 1, fc), F32),
                        pltpu.VMEM((bt, lt, fc), F32),
                        pltpu.VMEM((bt * lt, fc), BF16)],
        compiler_params=_cparams(("arbitrary", "arbitrary", "arbitrary")),
        name="conv_ffn",
    )(x, sc, sh, gt, g, gfin, prev, w_up, w_up, cw, cb, w_down)


CACHE_BLOCK = 256


def _cache_kernel(c_ref, n_ref, o_ref, *, feature_major):
    blk = CACHE_BLOCK
    P = c_ref.shape[0]
    lkp = o_ref.shape[1] if feature_major else o_ref.shape[0]

    def put(r, v):
        rows = slice(r * blk, (r + 1) * blk)
        if feature_major:
            o_ref[:, rows] = v.T.astype(o_ref.dtype)
        else:
            o_ref[rows, :] = v.astype(o_ref.dtype)

    for r in range(P // blk):
        c = c_ref[r * blk:(r + 1) * blk]
        put(r, pltpu.einshape("thd->t(hd)", c) if c.ndim == 3 else c)
    new = n_ref[...].astype(F32)
    put(P // blk, jnp.concatenate(
        [new, jnp.zeros((blk - new.shape[0], new.shape[1]), F32)], axis=0))
    for r in range(P // blk + 1, lkp // blk):
        put(r, jnp.zeros((blk, new.shape[1]), F32))


def _cache_keys(cache, new, lkp, feature_major):
    B, P = cache.shape[:2]
    L, W = new.shape[1:]
    blk = CACHE_BLOCK
    assert P % blk == 0 and lkp % blk == 0 and L <= blk and lkp >= P + L
    cblock = (None,) + cache.shape[1:]
    cmap = (lambda b: (b, 0, 0, 0)) if cache.ndim == 4 else (lambda b: (b, 0, 0))
    oshape = (W, lkp) if feature_major else (lkp, W)
    return pl.pallas_call(
        functools.partial(_cache_kernel, feature_major=feature_major),
        grid=(B,),
        in_specs=[pl.BlockSpec(cblock, cmap),
                  pl.BlockSpec((None, L, W), lambda b: (b, 0, 0))],
        out_specs=pl.BlockSpec((None,) + oshape, lambda b: (b, 0, 0)),
        out_shape=jax.ShapeDtypeStruct((B,) + oshape, BF16),
        compiler_params=_cparams(("arbitrary",)),
        name="cache_keys",
    )(cache, new)


def _pad_cols(w, mult):
    n = w.shape[1]
    npad = -(-n // mult) * mult
    return jnp.pad(w, ((0, 0), (0, npad - n)))


def _tiles(B, L, rows):
    if L >= rows:
        assert L % rows == 0
        return 1, rows
    assert rows % L == 0 and B % (rows // L) == 0
    return rows // L, L


def _layer(x, mod, past, wts, *, last, g_final):
    (g_mix, w_in_sb, w_in_sa, w_in_ix, w_gate, w_br_sb, w_br_sa, w_out, rel_table, g_ffn, w_up,
     conv_w, conv_b, w_down) = wts
    B, L, D = x.shape
    sh1, sc1, gt1, sh2, sc2, gt2 = (m[:, None, :] for m in jnp.split(mod, 6, axis=-1))
    past_len = 0 if past is None else past[0].shape[1]

    wn = W_SB
    bt, lt = _tiles(B, L, 512)
    fm = bt == 1
    qlay = "feature" if fm else "row"
    call = functools.partial(_proj, x, sc1, sh1, g_mix, bt=bt, lt=lt)
    q_sb, k_sb, kb_sb, v_sb, vb_sb = call(
        w_in_sb, [_Out(0, 0, wn, BF16), _Out(1, 0, wn, F32, "heads"), _Out(1, 0, wn, BF16),
                  _Out(2, 0, wn, F32, "heads"), _Out(2, 0, wn, BF16)], tn=wn, name="proj_sb")
    q_sa, k_sa, kb_sa, v_sa, vb_sa = call(
        w_in_sa, [_Out(0, 0, wn, BF16, qlay, scale=ATT_SCALE * LOG2E),
                  _Out(1, 0, wn, F32, "heads"), _Out(1, 0, wn, BF16),
                  _Out(2, 0, wn, F32, "heads"), _Out(2, 0, wn, BF16, qlay)],
        tn=wn, name="proj_sa")
    w_ix_out = (_Out(1, 0, LANE, F32, "feature", (D_IDX, D_IDX + H_IDX)) if fm
                else _Out(1, D_IDX, D_IDX + H_IDX, F32))
    q_ix, k_ix, kb_ix, w_ix = call(
        w_in_ix, [_Out(0, 0, wn, BF16, qlay), _Out(1, 0, D_IDX, F32), _Out(1, 0, D_IDX, BF16),
                  w_ix_out], tn=wn, name="proj_ix")
    bt, lt = _tiles(B, L, 1024)
    gate, = _proj(x, sc1, sh1, g_mix, w_gate, [_Out(None, 0, 512, F32)], bt=bt, lt=lt, tn=512,
                  sigmoid=True, name="proj_gate")
    if not fm:
        q_sa, q_ix, w_ix = (jnp.swapaxes(a, 1, 2) for a in (q_sa, q_ix, w_ix))

    tq = min(256, L)
    tk_sb = 256
    tk_sa = 512 if L >= 512 else 256
    l_keys = past_len + L

    def keys_of(new, old, tk, feature_major=False):
        if old is None:
            assert l_keys % tk == 0
            return new
        assert not fm
        return _cache_keys(old, new, -(-l_keys // tk) * tk, feature_major)

    old = (None,) * 5 if past is None else past[:5]
    o_sb = _sb_attention(q_sb, keys_of(kb_sb, old[0], tk_sb), keys_of(vb_sb, old[1], tk_sb),
                         tq=tq, tk=tk_sb, past_len=past_len)
    assert fm or past is not None
    vt = keys_of(vb_sa, old[3], tk_sa, True)
    o_sa_t = _dsa_attention(q_ix, w_ix, q_sa, keys_of(kb_ix, old[4], tk_sa),
                            keys_of(kb_sa, old[2], tk_sa), vt, rel_table,
                            tq=tq, tk=tk_sa, past_len=past_len)
    o_sa = jnp.swapaxes(o_sa_t, 1, 2)

    bt, lt = _tiles(B, L, 256)
    x1 = _merge(o_sb, o_sa, gate, x, gt1, w_br_sb, w_br_sa, w_out, bt=bt, lt=lt)

    prev = jnp.zeros((B, CONV_W - 1, w_down.shape[0]), F32) if past is None else past[5]
    bt, lt = _tiles(B, L, 512)
    x2, conv_state = _ffn(x1, sc2, sh2, gt2, g_ffn, g_final, prev, w_up, conv_w, conv_b,
                          w_down, bt=bt, lt=lt, fc=512, final=last)
    conv_state = jnp.swapaxes(conv_state, 1, 2).reshape(B, CONV_W - 1, -1)
    return x2, (k_sb, v_sb, k_sa, v_sa, k_ix, conv_state)


def kernel(x_prompt, x_sample, cache_sb_k, cache_sb_v, cache_sa_k, cache_sa_v, cache_idx_k,
           state_ffn_conv, c_prompt, c_sample, w_ada, b_ada, g_mix, w_in, w_gate, w_br_sb,
           w_br_sa, w_out, rel_table, g_ffn, w_up, conv_w, conv_b, w_down, g_final):
    depth = w_ada.shape[0]
    nbp = c_prompt.shape[0]
    nbs = c_sample.shape[0]
    rows = -(-(nbp + nbs) // 8) * 8
    c_all = jnp.concatenate([c_prompt, c_sample,
                             jnp.zeros((rows - nbp - nbs, c_prompt.shape[1]), F32)], axis=0)
    xp, xs = x_prompt, x_sample
    new_p, new_s = [], []
    for l in range(depth):
        mod = _adaln(c_all, w_ada[l], b_ada[l][None, :])
        wb = w_in[l].astype(BF16)
        wts = (g_mix[l][None, :], wb[:, :COL_Q_SA], wb[:, COL_Q_SA:COL_Q_IX],
               _pad_cols(wb[:, COL_Q_IX:], 2 * W_SB), w_gate[l].astype(BF16),
               w_br_sb[l].astype(BF16), w_br_sa[l].astype(BF16), w_out[l].astype(BF16),
               rel_table, g_ffn[l][None, :], w_up[l].astype(BF16), conv_w[l],
               conv_b[l][None, :], w_down[l].astype(BF16))
        gfin = g_final[None, :]
        past = (cache_sb_k[l], cache_sb_v[l], cache_sa_k[l], cache_sa_v[l], cache_idx_k[l],
                state_ffn_conv[l])
        xp, sp = _layer(xp, mod[:nbp], None, wts, last=l == depth - 1, g_final=gfin)
        xs, ss = _layer(xs, mod[nbp:nbp + nbs], past, wts, last=l == depth - 1, g_final=gfin)
        new_p.append(sp)
        new_s.append(ss)
    stack = lambda states, n: jnp.stack([s[n] for s in states])
    return ((xp, xs) + tuple(stack(new_p, n) for n in range(6))
            + tuple(stack(new_s, n) for n in range(6)))
```

```python
import functools

import numpy as np
import jax
import jax.numpy as jnp
from jax import lax
from jax.experimental import pallas as pl
from jax.experimental.pallas import tpu as pltpu

F32 = jnp.float32
BF16 = jnp.bfloat16
I32 = jnp.int32

CHUNK = 64
D_HEAD = 128
H_SB = 8
H_SA = 8
W_SB = H_SB * D_HEAD
W_SA = H_SA * D_HEAD
H_IDX = 16
D_IDX = 64
TOPK_MAX = 256
N_BUCKETS = 32
REL_MAX_DIST = 1024
CONV_W = 3
EPS = 1e-6

COL_Q_SB = 0
COL_K_SB = W_SB
COL_V_SB = 2 * W_SB
COL_Q_SA = 3 * W_SB
COL_K_SA = 3 * W_SB + W_SA
COL_V_SA = 3 * W_SB + 2 * W_SA
COL_Q_IX = 3 * W_SB + 3 * W_SA
COL_K_IX = COL_Q_IX + H_IDX * D_IDX
COL_W_IX = COL_K_IX + D_IDX
IN_COLS = COL_W_IX + H_IDX

LANE = 128
VMEM_LIMIT = 56 * 1024 * 1024

ATT_SCALE = D_HEAD ** -0.5
LOG2E = 1.4426950408889634
IDX_SCALE = (D_IDX ** -0.5) * (H_IDX ** -0.5)
NEG = -1e30
INT_MIN = -2 ** 31
SB_DEAD = -104.0


def _cparams(sem):
    return pltpu.CompilerParams(dimension_semantics=sem, vmem_limit_bytes=VMEM_LIMIT)


def _adaln_kernel(c_ref, w_ref, b_ref, o_ref):
    c = c_ref[...]
    a = c * jax.nn.sigmoid(c)
    o_ref[...] = jnp.dot(a.astype(BF16), w_ref[...].astype(BF16),
                         preferred_element_type=F32) + b_ref[...]


def _adaln(c, w, b):
    R, D = c.shape
    N = w.shape[1]
    tn = 1024
    return pl.pallas_call(
        _adaln_kernel,
        grid=(N // tn,),
        in_specs=[pl.BlockSpec((R, D), lambda n: (0, 0)),
                  pl.BlockSpec((D, tn), lambda n: (0, n)),
                  pl.BlockSpec((1, tn), lambda n: (0, n))],
        out_specs=pl.BlockSpec((R, tn), lambda n: (0, n)),
        out_shape=jax.ShapeDtypeStruct((R, N), F32),
        compiler_params=_cparams(("arbitrary",)),
        name="adaln",
    )(c, w, b)


NORM_ROWS = 16


def _norm_mod_store(x_ref, g_ref, sc_ref, sh_ref, h_ref):
    bt, lt, _ = x_ref.shape
    g = g_ref[...]
    for b in range(bt):
        sc1 = 1.0 + sc_ref[b]
        sh = sh_ref[b]
        for r0 in range(0, lt, NORM_ROWS):
            r1 = min(r0 + NORM_ROWS, lt)
            x = x_ref[b, r0:r1, :]
            ms = jnp.mean(x * x, axis=-1, keepdims=True)
            y = x * lax.rsqrt(ms + EPS) * g
            h_ref[b * lt + r0:b * lt + r1, :] = (y * sc1 + sh).astype(h_ref.dtype)


class _Out(tuple):
    __slots__ = ()

    def __new__(cls, tile, lo, hi, dtype, layout="row", rows=None, scale=None):
        return tuple.__new__(cls, (tile, lo, hi, rows or (0, hi - lo), layout, dtype, scale))


def _proj_kernel(x_ref, sc_ref, sh_ref, g_ref, w_ref, *rest, plan, sigmoid):
    out_refs, h_ref = rest[:-1], rest[-1]
    bt, lt, D = x_ref.shape
    n = pl.program_id(2)

    @pl.when(n == 0)
    def _():
        _norm_mod_store(x_ref, g_ref, sc_ref, sh_ref, h_ref)

    r = jnp.dot(h_ref[...], w_ref[...], preferred_element_type=F32)
    if sigmoid:
        r = jnp.concatenate([jax.nn.sigmoid(r[r0:r0 + NORM_ROWS]) for r0 in
                             range(0, bt * lt, NORM_ROWS)], axis=0)
    for (tile, lo, hi, rows, layout, _, scale), o_ref in zip(plan, out_refs):
        def emit(o_ref=o_ref, lo=lo, hi=hi, rows=rows, layout=layout, scale=scale):
            v = r[:, lo:hi]
            if scale is not None:
                v = v * scale
            if layout == "feature":
                o_ref[...] = v.T[rows[0]:rows[1], :].astype(o_ref.dtype)
            elif layout == "heads":
                nh = (hi - lo) // D_HEAD
                v = pltpu.einshape("t(hd)->thd", v, h=nh)
                o_ref[...] = v.reshape(bt, lt, nh, D_HEAD).astype(o_ref.dtype)
            else:
                o_ref[...] = v.reshape(bt, lt, hi - lo).astype(o_ref.dtype)

        if tile is None:
            emit()
        else:
            pl.when(n == tile)(emit)


def _proj(x, sc, sh, g, w, plan, *, bt, lt, tn, sigmoid=False, name):
    B, L, D = x.shape
    N = w.shape[1]
    out_specs, out_shape = [], []
    for tile, lo, hi, rows, layout, dtype, _ in plan:
        if tile is None:
            out_specs.append(pl.BlockSpec((bt, lt, tn), lambda b, t, n: (b, t, n)))
            out_shape.append(jax.ShapeDtypeStruct((B, L, N), dtype))
        elif layout == "heads":
            nh = (hi - lo) // D_HEAD
            out_specs.append(pl.BlockSpec((bt, lt, nh, D_HEAD), lambda b, t, n: (b, t, 0, 0)))
            out_shape.append(jax.ShapeDtypeStruct((B, L, nh, D_HEAD), dtype))
        elif layout == "feature":
            assert bt == 1
            width = rows[1] - rows[0]
            out_specs.append(pl.BlockSpec((None, width, lt), lambda b, t, n: (b, 0, t)))
            out_shape.append(jax.ShapeDtypeStruct((B, width, L), dtype))
        else:
            out_specs.append(pl.BlockSpec((bt, lt, hi - lo), lambda b, t, n: (b, t, 0)))
            out_shape.append(jax.ShapeDtypeStruct((B, L, hi - lo), dtype))
    return pl.pallas_call(
        functools.partial(_proj_kernel, plan=tuple(plan), sigmoid=sigmoid),
        grid=(B // bt, L // lt, N // tn),
        in_specs=[pl.BlockSpec((bt, lt, D), lambda b, t, n: (b, t, 0)),
                  pl.BlockSpec((bt, 1, D), lambda b, t, n: (b, 0, 0)),
                  pl.BlockSpec((bt, 1, D), lambda b, t, n: (b, 0, 0)),
                  pl.BlockSpec((1, D), lambda b, t, n: (0, 0)),
                  pl.BlockSpec((D, tn), lambda b, t, n: (0, n))],
        out_specs=out_specs,
        out_shape=out_shape,
        scratch_shapes=[pltpu.VMEM((bt * lt, D), BF16)],
        compiler_params=_cparams(("arbitrary", "arbitrary", "arbitrary")),
        name=name,
    )(x, sc, sh, g, w)


def _sb_kernel(q_ref, k_ref, v_ref, u_ref, o_ref, *, tq, tk, past_len, hp):
    i = pl.program_id(2)
    qpos0 = past_len + i * tq
    qpos = qpos0 + lax.broadcasted_iota(I32, (tq, 1), 0)
    j0 = (qpos0 + tq - 2) // tk
    u = u_ref[...]

    def cond(carry):
        j, go, _, _ = carry
        return jnp.logical_and(j >= 0, go)

    def body(carry):
        j, _, cums, accs = carry
        off = pl.multiple_of(j * tk, tk)
        kpos = j * tk + lax.broadcasted_iota(I32, (1, tk), 1)
        mask = kpos < qpos
        new_cums, new_accs = [], []
        for h in range(hp):
            cs = slice(h * D_HEAD, (h + 1) * D_HEAD)
            z = lax.dot_general(q_ref[:, cs], k_ref[pl.ds(off, tk), cs], _NT,
                                preferred_element_type=F32) * (ATT_SCALE * LOG2E)
            t = jnp.log2(1.0 + jnp.exp2(-jnp.abs(z)))
            lk = jnp.where(mask, -(jnp.maximum(z, 0.0) + t), 0.0)
            hi = lk.astype(BF16)
            lo = (lk - hi.astype(F32)).astype(BF16)
            after = cums[h] + (jnp.dot(hi, u, preferred_element_type=F32)
                               + jnp.dot(lo, u, preferred_element_type=F32))
            lsz = jnp.minimum(z, 0.0) - t
            a = jnp.where(mask, jnp.exp2(lsz + after), 0.0)
            new_accs.append(accs[h] + jnp.dot(a.astype(BF16), v_ref[pl.ds(off, tk), cs],
                                              preferred_element_type=F32))
            new_cums.append(cums[h] + jnp.sum(lk, axis=1, keepdims=True))
        top = new_cums[0]
        for c in new_cums[1:]:
            top = jnp.maximum(top, c)
        return j - 1, jnp.max(top) > SB_DEAD * LOG2E, tuple(new_cums), tuple(new_accs)

    init = (j0, jnp.bool_(True), (jnp.zeros((tq, 1), F32),) * hp,
            (jnp.zeros((tq, D_HEAD), F32),) * hp)
    _, _, _, accs = lax.while_loop(cond, body, init)
    for h in range(hp):
        o_ref[:, h * D_HEAD:(h + 1) * D_HEAD] = accs[h].astype(o_ref.dtype)


def _sb_attention(q, k_all, v_all, *, tq, tk, past_len, hp=4):
    B, L, _ = q.shape
    Lkp = k_all.shape[1]
    r = np.arange(tk)
    u = jnp.asarray((r[:, None] > r[None, :]).astype(np.float32), BF16)
    wp = hp * D_HEAD
    kv_mode = dict(pipeline_mode=pl.Buffered(1)) if L // tq > 1 else {}
    return pl.pallas_call(
        functools.partial(_sb_kernel, tq=tq, tk=tk, past_len=past_len, hp=hp),
        grid=(B, H_SB // hp, L // tq),
        in_specs=[pl.BlockSpec((None, tq, wp), lambda b, h, i: (b, i, h)),
                  pl.BlockSpec((None, Lkp, wp), lambda b, h, i: (b, 0, h), **kv_mode),
                  pl.BlockSpec((None, Lkp, wp), lambda b, h, i: (b, 0, h), **kv_mode),
                  pl.BlockSpec((tk, tk), lambda b, h, i: (0, 0))],
        out_specs=pl.BlockSpec((None, tq, wp), lambda b, h, i: (b, i, h)),
        out_shape=jax.ShapeDtypeStruct((B, L, W_SB), BF16),
        compiler_params=_cparams(("arbitrary", "arbitrary", "arbitrary")),
        name="sb_attn",
    )(q, k_all, v_all, u)


def _bucket_edges():
    nb = N_BUCKETS // 2
    max_exact = nb // 2

    def bucket(rel):
        n = abs(rel)
        if n < max_exact:
            v = n
        else:
            v = max_exact + int(np.log(np.float32(n) / max_exact)
                                / np.log(REL_MAX_DIST / max_exact) * (nb - max_exact))
            v = min(v, nb - 1)
        return (nb if rel > 0 else 0) + v

    lo = -4 * REL_MAX_DIST
    assert bucket(lo) == nb - 1
    edges = []
    prev = bucket(lo)
    for rel in range(lo + 1, CHUNK):
        bk = bucket(rel)
        if bk != prev:
            edges.append((rel, prev))
            prev = bk
    edges.append((CHUNK, prev))
    return edges


_EDGES = _bucket_edges()
_FAR_BUCKET = _EDGES[0][1]
_NEAR_REL = _EDGES[0][0]


def _bias_kernel(d_ref, rel_ref, o_ref):
    tk, tq = o_ref.shape
    h = pl.program_id(1)
    rel = (lax.broadcasted_iota(I32, (tk, tq), 0) - lax.broadcasted_iota(I32, (tk, tq), 1)
           - d_ref[pl.program_id(0)])
    tile = jnp.full((tk, tq), rel_ref[_EDGES[-1][1], h], F32)
    for edge, bk in reversed(_EDGES[:-1]):
        tile = jnp.where(rel < edge, rel_ref[bk, h], tile)
    o_ref[...] = (tile - rel_ref[_FAR_BUCKET, h]) * LOG2E


def _bias_tiles(ds, rel_table, *, tq, tk):
    return pl.pallas_call(
        _bias_kernel,
        grid=(len(ds), H_SA),
        in_specs=[pl.BlockSpec(memory_space=pltpu.SMEM), pl.BlockSpec(memory_space=pltpu.SMEM)],
        out_specs=pl.BlockSpec((None, None, tk, tq), lambda n, h: (n, h, 0, 0)),
        out_shape=jax.ShapeDtypeStruct((len(ds), H_SA, tk, tq), F32),
        compiler_params=_cparams(("arbitrary", "arbitrary")),
        name="bias_tiles",
    )(jnp.asarray(np.asarray(ds, np.int32)), rel_table)


_NT = (((1,), (1,)), ((), ()))


def _dsa_kernel(itab, jtab, ftab, ltab, ntab, nbtab, fartab,
                qit_ref, wt_ref, qt_ref, ki_ref, k_ref, vt_ref, bias_ref, o_ref,
                keys_ref, half_ref, thr_ref, m_ref, l_ref, acc_ref, madd_ref, s_ref, p_ref,
                *, tq, tk, past_len, sub, csub):
    p = pl.program_id(1)
    i = itab[p]
    j = jtab[p]
    qpos = past_len + i * tq + lax.broadcasted_iota(I32, (1, tq), 1)
    lim = (qpos // CHUNK + 1) * CHUNK

    @pl.when(ftab[p] == 1)
    def _():
        wt = wt_ref[...] * IDX_SCALE

        def score_blk(c, carry):
            off = pl.multiple_of(c * sub, sub)
            kib = ki_ref[pl.ds(off, sub), :]
            s = jnp.zeros((sub, tq), F32)
            for h in range(H_IDX):
                sh = jnp.dot(kib, qit_ref[h * D_IDX:(h + 1) * D_IDX, :],
                             preferred_element_type=F32)
                s = s + wt[h:h + 1, :] * jnp.maximum(sh, 0.0)
            bits = pltpu.bitcast(s, I32)
            key = jnp.where(bits < 0, bits ^ jnp.int32(0x7FFFFFFF), bits)
            kpos = c * sub + lax.broadcasted_iota(I32, (sub, 1), 0)
            key = jnp.where(kpos < lim, key, jnp.int32(INT_MIN))
            keys_ref[pl.ds(off, sub), :] = key
            half_ref[pl.ds(off, sub), :] = (key >> 16).astype(jnp.int16)
            return carry

        per = tk // sub

        def score_grp(g, carry):
            for u in range(per):
                score_blk(g * per + u, carry)
            return carry

        lax.fori_loop(0, nbtab[p], score_grp, 0)

        need = jnp.minimum(TOPK_MAX, lim)
        nchunk = nbtab[p] * (tk // csub)
        i16_min = -2 ** 15

        def count_ge(cand):
            c16 = jnp.broadcast_to(cand, (16, tq)).astype(jnp.int16)

            def cnt_blk(c, acc):
                kb = half_ref[pl.ds(pl.multiple_of(c * csub, csub), csub), :]
                parts = [jnp.where(kb[16 * r:16 * (r + 1), :] >= c16, jnp.int16(1), jnp.int16(0))
                         for r in range(csub // 16)]
                while len(parts) > 1:
                    parts = [a + b for a, b in zip(parts[::2], parts[1::2])]
                return acc + parts[0]

            acc = lax.fori_loop(0, nchunk, cnt_blk, jnp.zeros((16, tq), jnp.int16))
            return jnp.sum(acc.astype(I32), axis=0, keepdims=True)

        def kth_half(want):
            def bit_step(it, carry):
                tpre, reach = carry
                bitv = lax.shift_left(jnp.int32(1), 15 - it)
                cnt = count_ge((tpre | bitv) + i16_min)
                hit = cnt >= want
                return jnp.where(hit, tpre | bitv, tpre), jnp.where(hit, cnt, reach)

            init = (jnp.zeros((1, tq), I32), jnp.full((1, tq), -1, I32))
            return lax.fori_loop(0, 16, bit_step, init)

        hi = kth_half(need)[0] + i16_min
        above = jnp.where(hi == 2 ** 15 - 1, 0, count_ge(jnp.minimum(hi + 1, 2 ** 15 - 1)))

        def low_halves(c, bucket):
            rows = pl.ds(pl.multiple_of(c * csub, csub), csub)
            k32 = keys_ref[rows, :]
            lo = (k32 & 0xFFFF) + i16_min
            same = (k32 >> 16) == hi
            half_ref[rows, :] = jnp.where(same, lo, i16_min).astype(jnp.int16)
            return bucket + jnp.sum(jnp.where(same, 1.0, 0.0).reshape(csub // 8, 8, tq), axis=0)

        bucket = lax.fori_loop(0, nchunk, low_halves, jnp.zeros((8, tq), F32))
        bucket = jnp.sum(bucket, axis=0, keepdims=True).astype(I32)
        lo_half, reach = kth_half(need - above)
        thr = lax.shift_left(hi, 16) | lo_half
        thr_ref[...] = thr
        kept = above + jnp.where(lo_half > 0, reach, bucket)

        def rows_of(c):
            return c * csub + lax.broadcasted_iota(I32, (csub, 1), 0)

        def count_where(pred):
            def blk(c, acc):
                kb = keys_ref[pl.ds(pl.multiple_of(c * csub, csub), csub), :]
                hit = jnp.where(pred(kb, rows_of(c)), 1.0, 0.0)
                return acc + jnp.sum(hit.reshape(csub // 8, 8, tq), axis=0)

            acc = lax.fori_loop(0, nchunk, blk, jnp.zeros((8, tq), F32))
            return jnp.sum(acc, axis=0, keepdims=True).astype(I32)

        excess = kept - need

        @pl.when(jnp.max(excess.astype(F32)) > 0.0)
        def _():
            keep = count_where(lambda kb, _: kb == thr) - excess
            nbits = keys_ref.shape[0].bit_length()

            def bit_step(it, bound):
                cand = bound | lax.shift_left(jnp.int32(1), nbits - 1 - it)
                cnt = count_where(lambda kb, rows: jnp.where(rows < cand, kb, thr - 1) == thr)
                return jnp.where(cnt <= keep, cand, bound)

            bound = lax.fori_loop(0, nbits, bit_step, jnp.zeros((1, tq), I32))

            def demote(c, carry):
                rows = pl.ds(pl.multiple_of(c * csub, csub), csub)
                kb = keys_ref[rows, :]
                late_tie = jnp.where(rows_of(c) >= bound, kb, thr - 1) == thr
                keys_ref[rows, :] = jnp.where(late_tie, thr - 1, kb)
                return carry

            lax.fori_loop(0, nchunk, demote, 0)

        m_ref[...] = jnp.full(m_ref.shape, NEG, F32)
        l_ref[...] = jnp.zeros(l_ref.shape, F32)
        acc_ref[...] = jnp.zeros(acc_ref.shape, F32)

    thr = thr_ref[...]
    for c in range(tk // sub):
        rows = pl.ds(pl.multiple_of(j * tk + c * sub, sub), sub)
        madd_ref[c * sub:(c + 1) * sub, :] = jnp.where(keys_ref[rows, :] >= thr, 0.0, NEG)

    def logits(h, near):
        cs = slice(h * D_HEAD, (h + 1) * D_HEAD)
        qh = qt_ref[cs, :]
        mx = jnp.full((8, tq), NEG, F32)
        for c in range(tk // sub):
            rs = slice(c * sub, (c + 1) * sub)
            s = jnp.dot(k_ref[rs, cs], qh, preferred_element_type=F32)
            extra = bias_ref[h, rs, :] + madd_ref[rs, :] if near else madd_ref[rs, :]
            s = s + extra
            s_ref[h % s_ref.shape[0], rs, :] = s
            mx = jnp.maximum(mx, jnp.max(s.reshape(sub // 8, 8, tq), axis=0))
        return jnp.max(mx, axis=0, keepdims=True)

    def attend(near):
        def weighted_values(h, alpha):
            cs = slice(h * D_HEAD, (h + 1) * D_HEAD)
            vt_ones = jnp.concatenate([vt_ref[cs, :], jnp.ones((16, tk), BF16)], axis=0)
            pv = jnp.dot(vt_ones, p_ref[h % 2], preferred_element_type=F32)
            l_ref[h:h + 1, :] = alpha * l_ref[h:h + 1, :] + pv[D_HEAD:D_HEAD + 1, :]
            acc_ref[cs, :] = alpha * acc_ref[cs, :] + pv[:D_HEAD, :]

        ahead = s_ref.shape[0] - 1
        smaxes = [logits(h, near) for h in range(ahead)]
        alpha_prev = None
        for h in range(H_SA):
            m_prev = m_ref[h:h + 1, :]
            m_new = jnp.maximum(m_prev, smaxes[h])
            if h + ahead < H_SA:
                smaxes.append(logits(h + ahead, near))
            alpha = jnp.exp2(m_prev - m_new)
            for c in range(tk // sub):
                rs = slice(c * sub, (c + 1) * sub)
                p_ref[h % 2, rs, :] = jnp.exp2(
                    s_ref[h % (ahead + 1), rs, :] - m_new).astype(BF16)
            m_ref[h:h + 1, :] = m_new
            if h > 0:
                weighted_values(h - 1, alpha_prev)
            alpha_prev = alpha
        weighted_values(H_SA - 1, alpha_prev)

    pl.when(fartab[p] == 0)(functools.partial(attend, True))
    pl.when(fartab[p] == 1)(functools.partial(attend, False))

    @pl.when(ltab[p] == 1)
    def _():
        for h in range(H_SA):
            cs = slice(h * D_HEAD, (h + 1) * D_HEAD)
            o_ref[cs, :] = (acc_ref[cs, :] / l_ref[h:h + 1, :]).astype(o_ref.dtype)


def _dsa_attention(qit, wt, qt, ki_all, k_all, vt_all, rel_table, *, tq, tk, past_len):
    B, _, L = qt.shape
    Lkp = k_all.shape[1]
    assert Lkp % tk == 0 and L % tq == 0
    nq = L // tq
    it, jt, ft, lt_, nt, nbt, fart = [], [], [], [], [], [], []
    near_ds = []
    for i in range(nq):
        qpos0 = past_len + i * tq
        lim_max = ((qpos0 + tq - 1) // CHUNK + 1) * CHUNK
        jmax = (lim_max - 1) // tk
        for j in range(jmax + 1):
            d = qpos0 - j * tk
            if (tk - 1) - d >= _NEAR_REL:
                if d not in near_ds:
                    near_ds.append(d)
                nt.append(near_ds.index(d))
                fart.append(0)
            else:
                nt.append(nt[-1] if nt else 0)
                fart.append(1)
            it.append(i); jt.append(j); ft.append(int(j == 0)); lt_.append(int(j == jmax))
            nbt.append(jmax + 1)
    bias = _bias_tiles(near_ds, rel_table, tq=tq, tk=tk)
    tabs = [jnp.asarray(np.asarray(t, np.int32)) for t in (it, jt, ft, lt_, nt, nbt, fart)]
    grid_spec = pltpu.PrefetchScalarGridSpec(
        num_scalar_prefetch=7,
        grid=(B, len(it)),
        in_specs=[
            pl.BlockSpec((None, H_IDX * D_IDX, tq), lambda b, p, it, jt, *_: (b, 0, it[p])),
            pl.BlockSpec((None, H_IDX, tq), lambda b, p, it, jt, *_: (b, 0, it[p])),
            pl.BlockSpec((None, W_SA, tq), lambda b, p, it, jt, *_: (b, 0, it[p])),
            pl.BlockSpec((None, Lkp, D_IDX), lambda b, p, it, jt, *_: (b, 0, 0),
                         pipeline_mode=pl.Buffered(1)),
            pl.BlockSpec((None, tk, W_SA), lambda b, p, it, jt, *_: (b, jt[p], 0)),
            pl.BlockSpec((None, W_SA, tk), lambda b, p, it, jt, *_: (b, 0, jt[p])),
            pl.BlockSpec((None, H_SA, tk, tq),
                         lambda b, p, it, jt, ft, lt, nt, *_: (nt[p], 0, 0, 0)),
        ],
        out_specs=pl.BlockSpec((None, W_SA, tq), lambda b, p, it, jt, *_: (b, 0, it[p])),
        scratch_shapes=[
            pltpu.VMEM((Lkp, tq), I32),
            pltpu.VMEM((Lkp, tq), jnp.int16),
            pltpu.VMEM((1, tq), I32),
            pltpu.VMEM((H_SA, tq), F32),
            pltpu.VMEM((H_SA, tq), F32),
            pltpu.VMEM((W_SA, tq), F32),
            pltpu.VMEM((tk, tq), F32),
            pltpu.VMEM((3, tk, tq), F32),
            pltpu.VMEM((2, tk, tq), BF16),
        ],
    )
    return pl.pallas_call(
        functools.partial(_dsa_kernel, tq=tq, tk=tk, past_len=past_len,
                          sub=min(tk, 128), csub=min(tk, 512)),
        grid_spec=grid_spec,
        out_shape=jax.ShapeDtypeStruct((B, W_SA, L), BF16),
        compiler_params=_cparams(("arbitrary", "arbitrary")),
        name="dsa_attn",
    )(*tabs, qit, wt, qt, ki_all, k_all, vt_all, bias)


def _merge_kernel(osb_ref, osa_ref, g_ref, x_ref, gt_ref, wsb_ref, wsa_ref, wo_ref, o_ref):
    bt, lt, D = x_ref.shape
    tm = bt * lt
    a = jnp.dot(osb_ref[...].reshape(tm, W_SB), wsb_ref[...], preferred_element_type=F32)
    c = jnp.dot(osa_ref[...].reshape(tm, W_SA), wsa_ref[...], preferred_element_type=F32)
    g = g_ref[...].reshape(tm, 2 * D)
    merged = g[:, :D] * a + g[:, D:] * c
    y = jnp.dot(merged.astype(BF16), wo_ref[...], preferred_element_type=F32)
    o_ref[...] = x_ref[...] + gt_ref[...] * y.reshape(bt, lt, D)


def _merge(osb, osa, g, x, gt, wsb, wsa, wo, *, bt, lt):
    B, L, D = x.shape
    row = lambda w: pl.BlockSpec((bt, lt, w), lambda b, t: (b, t, 0))
    full = lambda a: pl.BlockSpec(a.shape, lambda b, t: (0, 0), pipeline_mode=pl.Buffered(1))
    return pl.pallas_call(
        _merge_kernel,
        grid=(B // bt, L // lt),
        in_specs=[row(W_SB), row(W_SA), row(2 * D), row(D),
                  pl.BlockSpec((bt, 1, D), lambda b, t: (b, 0, 0)),
                  full(wsb), full(wsa), full(wo)],
        out_specs=row(D),
        out_shape=jax.ShapeDtypeStruct((B, L, D), F32),
        compiler_params=_cparams(("arbitrary", "arbitrary")),
        name="merge_out",
    )(osb, osa, g, x, gt, wsb, wsa, wo)


HALO = 8
FFN_ROWS = 32


def _ffn_kernel(x_ref, sc_ref, sh_ref, gt_ref, g_ref, gfin_ref, prev_ref, wg_ref, wv_ref,
                cw_ref, cb_ref, wd_ref, x2_ref, st_ref,
                h_ref, acc_ref, ext_ref, carry_ref, uv_ref, act_ref, *, final):
    bt, lt, D = x_ref.shape
    tm = bt * lt
    fc = wg_ref.shape[1]
    t = pl.program_id(1)
    f = pl.program_id(2)

    @pl.when(f == 0)
    def _():
        _norm_mod_store(x_ref, g_ref, sc_ref, sh_ref, h_ref)
        acc_ref[...] = jnp.zeros(acc_ref.shape, F32)

    h = h_ref[...]

    @pl.when(t == 0)
    def _():
        ext_ref[:, HALO - 2:HALO, :] = prev_ref[...]

    @pl.when(t > 0)
    def _():
        ext_ref[:, HALO - 2:HALO, :] = carry_ref[f]

    ext_ref[:, HALO:HALO + lt, :] = jnp.dot(
        h, wg_ref[...], preferred_element_type=F32).reshape(bt, lt, fc)
    uv_ref[...] = jnp.dot(h, wv_ref[...], preferred_element_type=F32).reshape(bt, lt, fc)
    tail = ext_ref[:, HALO + lt - 2:HALO + lt, :]
    carry_ref[f] = tail
    st_ref[:, f] = tail
    cw = cw_ref[...]
    cb = cb_ref[...]
    for b in range(bt):
        for r0 in range(0, lt, FFN_ROWS):
            r1 = min(r0 + FFN_ROWS, lt)
            e = [ext_ref[b, HALO - 2 + j + r0:HALO - 2 + j + r1, :] for j in range(CONV_W)]
            conv = cb + cw[0:1, :] * e[0] + cw[1:2, :] * e[1] + cw[2:3, :] * e[2]
            act = conv * jax.nn.sigmoid(conv) * uv_ref[b, r0:r1, :]
            act_ref[b * lt + r0:b * lt + r1, :] = act.astype(BF16)
    acc_ref[...] += jnp.dot(act_ref[...], wd_ref[...], preferred_element_type=F32)

    @pl.when(f == pl.num_programs(2) - 1)
    def _():
        gfin = gfin_ref[...]
        for b in range(bt):
            gt = gt_ref[b]
            for r0 in range(0, lt, NORM_ROWS):
                r1 = min(r0 + NORM_ROWS, lt)
                x2 = x_ref[b, r0:r1, :] + gt * acc_ref[b * lt + r0:b * lt + r1, :]
                if final:
                    ms = jnp.mean(x2 * x2, axis=-1, keepdims=True)
                    x2 = x2 * lax.rsqrt(ms + EPS) * gfin
                x2_ref[b, r0:r1, :] = x2


def _ffn(x, sc, sh, gt, g, gfin, prev, w_up, cw, cb, w_down, *, bt, lt, fc, final):
    B, L, D = x.shape
    F = w_down.shape[0]
    nf = F // fc
    row = pl.BlockSpec((bt, lt, D), lambda b, t, f: (b, t, 0))
    mod = pl.BlockSpec((bt, 1, D), lambda b, t, f: (b, 0, 0))
    vec = pl.BlockSpec((1, D), lambda b, t, f: (0, 0))
    return pl.pallas_call(
        functools.partial(_ffn_kernel, final=final),
        grid=(B // bt, L // lt, nf),
        in_specs=[row, mod, mod, mod, vec, vec,
                  pl.BlockSpec((bt, CONV_W - 1, fc), lambda b, t, f: (b, 0, f)),
                  pl.BlockSpec((D, fc), lambda b, t, f: (0, f)),
                  pl.BlockSpec((D, fc), lambda b, t, f: (0, nf + f)),
                  pl.BlockSpec((CONV_W, fc), lambda b, t, f: (0, f)),
                  pl.BlockSpec((1, fc), lambda b, t, f: (0, f)),
                  pl.BlockSpec((fc, D), lambda b, t, f: (f, 0))],
        out_specs=[row, pl.BlockSpec((bt, nf, CONV_W - 1, fc), lambda b, t, f: (b, 0, 0, 0))],
        out_shape=[jax.ShapeDtypeStruct((B, L, D), F32),
                   jax.ShapeDtypeStruct((B, nf, CONV_W - 1, fc), F32)],
        scratch_shapes=[pltpu.VMEM((bt * lt, D), BF16),
                        pltpu.VMEM((bt * lt, D), F32),
                        pltpu.VMEM((bt, HALO + lt, fc), F32),
                        pltpu.VMEM((nf, bt, CONV_W - 1, fc), F32),
                        pltpu.VMEM((bt, lt, fc), F32),
                        pltpu.VMEM((bt * lt, fc), BF16)],
        compiler_params=_cparams(("arbitrary", "arbitrary", "arbitrary")),
        name="conv_ffn",
    )(x, sc, sh, gt, g, gfin, prev, w_up, w_up, cw, cb, w_down)


CACHE_BLOCK = 256


def _cache_kernel(c_ref, n_ref, o_ref, *, feature_major):
    blk = CACHE_BLOCK
    P = c_ref.shape[0]
    lkp = o_ref.shape[1] if feature_major else o_ref.shape[0]

    def put(r, v):
        rows = slice(r * blk, (r + 1) * blk)
        if feature_major:
            o_ref[:, rows] = v.T.astype(o_ref.dtype)
        else:
            o_ref[rows, :] = v.astype(o_ref.dtype)

    for r in range(P // blk):
        c = c_ref[r * blk:(r + 1) * blk]
        put(r, pltpu.einshape("thd->t(hd)", c) if c.ndim == 3 else c)
    new = n_ref[...].astype(F32)
    put(P // blk, jnp.concatenate(
        [new, jnp.zeros((blk - new.shape[0], new.shape[1]), F32)], axis=0))
    for r in range(P // blk + 1, lkp // blk):
        put(r, jnp.zeros((blk, new.shape[1]), F32))


def _cache_keys(cache, new, lkp, feature_major):
    B, P = cache.shape[:2]
    L, W = new.shape[1:]
    blk = CACHE_BLOCK
    assert P % blk == 0 and lkp % blk == 0 and L <= blk and lkp >= P + L
    cblock = (None,) + cache.shape[1:]
    cmap = (lambda b: (b, 0, 0, 0)) if cache.ndim == 4 else (lambda b: (b, 0, 0))
    oshape = (W, lkp) if feature_major else (lkp, W)
    return pl.pallas_call(
        functools.partial(_cache_kernel, feature_major=feature_major),
        grid=(B,),
        in_specs=[pl.BlockSpec(cblock, cmap),
                  pl.BlockSpec((None, L, W), lambda b: (b, 0, 0))],
        out_specs=pl.BlockSpec((None,) + oshape, lambda b: (b, 0, 0)),
        out_shape=jax.ShapeDtypeStruct((B,) + oshape, BF16),
        compiler_params=_cparams(("arbitrary",)),
        name="cache_keys",
    )(cache, new)


def _pad_cols(w, mult):
    n = w.shape[1]
    npad = -(-n // mult) * mult
    return jnp.pad(w, ((0, 0), (0, npad - n)))


def _tiles(B, L, rows):
    if L >= rows:
        assert L % rows == 0
        return 1, rows
    assert rows % L == 0 and B % (rows // L) == 0
    return rows // L, L


def _layer(x, mod, past, wts, *, last, g_final):
    (g_mix, w_in_sb, w_in_sa, w_in_ix, w_gate, w_br_sb, w_br_sa, w_out, rel_table, g_ffn, w_up,
     conv_w, conv_b, w_down) = wts
    B, L, D = x.shape
    sh1, sc1, gt1, sh2, sc2, gt2 = (m[:, None, :] for m in jnp.split(mod, 6, axis=-1))
    past_len = 0 if past is None else past[0].shape[1]

    wn = W_SB
    bt, lt = _tiles(B, L, 512)
    fm = bt == 1
    qlay = "feature" if fm else "row"
    call = functools.partial(_proj, x, sc1, sh1, g_mix, bt=bt, lt=lt)
    q_sb, k_sb, kb_sb, v_sb, vb_sb = call(
        w_in_sb, [_Out(0, 0, wn, BF16), _Out(1, 0, wn, F32, "heads"), _Out(1, 0, wn, BF16),
                  _Out(2, 0, wn, F32, "heads"), _Out(2, 0, wn, BF16)], tn=wn, name="proj_sb")
    q_sa, k_sa, kb_sa, v_sa, vb_sa = call(
        w_in_sa, [_Out(0, 0, wn, BF16, qlay, scale=ATT_SCALE * LOG2E),
                  _Out(1, 0, wn, F32, "heads"), _Out(1, 0, wn, BF16),
                  _Out(2, 0, wn, F32, "heads"), _Out(2, 0, wn, BF16, qlay)],
        tn=wn, name="proj_sa")
    w_ix_out = (_Out(1, 0, LANE, F32, "feature", (D_IDX, D_IDX + H_IDX)) if fm
                else _Out(1, D_IDX, D_IDX + H_IDX, F32))
    q_ix, k_ix, kb_ix, w_ix = call(
        w_in_ix, [_Out(0, 0, wn, BF16, qlay), _Out(1, 0, D_IDX, F32), _Out(1, 0, D_IDX, BF16),
                  w_ix_out], tn=wn, name="proj_ix")
    bt, lt = _tiles(B, L, 1024)
    gate, = _proj(x, sc1, sh1, g_mix, w_gate, [_Out(None, 0, 512, F32)], bt=bt, lt=lt, tn=512,
                  sigmoid=True, name="proj_gate")
    if not fm:
        q_sa, q_ix, w_ix = (jnp.swapaxes(a, 1, 2) for a in (q_sa, q_ix, w_ix))

    tq = min(256, L)
    tk_sb = 256
    tk_sa = 512 if L >= 512 else 256
    l_keys = past_len + L

    def keys_of(new, old, tk, feature_major=False):
        if old is None:
            assert l_keys % tk == 0
            return new
        assert not fm
        return _cache_keys(old, new, -(-l_keys // tk) * tk, feature_major)

    old = (None,) * 5 if past is None else past[:5]
    o_sb = _sb_attention(q_sb, keys_of(kb_sb, old[0], tk_sb), keys_of(vb_sb, old[1], tk_sb),
                         tq=tq, tk=tk_sb, past_len=past_len)
    assert fm or past is not None
    vt = keys_of(vb_sa, old[3], tk_sa, True)
    o_sa_t = _dsa_attention(q_ix, w_ix, q_sa, keys_of(kb_ix, old[4], tk_sa),
                            keys_of(kb_sa, old[2], tk_sa), vt, rel_table,
                            tq=tq, tk=tk_sa, past_len=past_len)
    o_sa = jnp.swapaxes(o_sa_t, 1, 2)

    bt, lt = _tiles(B, L, 256)
    x1 = _merge(o_sb, o_sa, gate, x, gt1, w_br_sb, w_br_sa, w_out, bt=bt, lt=lt)

    prev = jnp.zeros((B, CONV_W - 1, w_down.shape[0]), F32) if past is None else past[5]
    bt, lt = _tiles(B, L, 512)
    x2, conv_state = _ffn(x1, sc2, sh2, gt2, g_ffn, g_final, prev, w_up, conv_w, conv_b,
                          w_down, bt=bt, lt=lt, fc=512, final=last)
    conv_state = jnp.swapaxes(conv_state, 1, 2).reshape(B, CONV_W - 1, -1)
    return x2, (k_sb, v_sb, k_sa, v_sa, k_ix, conv_state)


def kernel(x_prompt, x_sample, cache_sb_k, cache_sb_v, cache_sa_k, cache_sa_v, cache_idx_k,
           state_ffn_conv, c_prompt, c_sample, w_ada, b_ada, g_mix, w_in, w_gate, w_br_sb,
           w_br_sa, w_out, rel_table, g_ffn, w_up, conv_w, conv_b, w_down, g_final):
    depth = w_ada.shape[0]
    nbp = c_prompt.shape[0]
    nbs = c_sample.shape[0]
    rows = -(-(nbp + nbs) // 8) * 8
    c_all = jnp.concatenate([c_prompt, c_sample,
                             jnp.zeros((rows - nbp - nbs, c_prompt.shape[1]), F32)], axis=0)
    xp, xs = x_prompt, x_sample
    new_p, new_s = [], []
    for l in range(depth):
        mod = _adaln(c_all, w_ada[l], b_ada[l][None, :])
        wb = w_in[l].astype(BF16)
        wts = (g_mix[l][None, :], wb[:, :COL_Q_SA], wb[:, COL_Q_SA:COL_Q_IX],
               _pad_cols(wb[:, COL_Q_IX:], 2 * W_SB), w_gate[l].astype(BF16),
               w_br_sb[l].astype(BF16), w_br_sa[l].astype(BF16), w_out[l].astype(BF16),
               rel_table, g_ffn[l][None, :], w_up[l].astype(BF16), conv_w[l],
               conv_b[l][None, :], w_down[l].astype(BF16))
        gfin = g_final[None, :]
        past = (cache_sb_k[l], cache_sb_v[l], cache_sa_k[l], cache_sa_v[l], cache_idx_k[l],
                state_ffn_conv[l])
        xp, sp = _layer(xp, mod[:nbp], None, wts, last=l == depth - 1, g_final=gfin)
        xs, ss = _layer(xs, mod[nbp:nbp + nbs], past, wts, last=l == depth - 1, g_final=gfin)
        new_p.append(sp)
        new_s.append(ss)
    stack = lambda states, n: jnp.stack([s[n] for s in states])
    return ((xp, xs) + tuple(stack(new_p, n) for n in range(6))
            + tuple(stack(new_s, n) for n in range(6)))
```

```python
import functools

import numpy as np
import jax
import jax.numpy as jnp
from jax import lax
from jax.experimental import pallas as pl
from jax.experimental.pallas import tpu as pltpu

F32 = jnp.float32
BF16 = jnp.bfloat16
I32 = jnp.int32

CHUNK = 64
D_HEAD = 128
H_SB = 8
H_SA = 8
W_SB = H_SB * D_HEAD
W_SA = H_SA * D_HEAD
H_IDX = 16
D_IDX = 64
TOPK_MAX = 256
N_BUCKETS = 32
REL_MAX_DIST = 1024
CONV_W = 3
EPS = 1e-6

COL_Q_SB = 0
COL_K_SB = W_SB
COL_V_SB = 2 * W_SB
COL_Q_SA = 3 * W_SB
COL_K_SA = 3 * W_SB + W_SA
COL_V_SA = 3 * W_SB + 2 * W_SA
COL_Q_IX = 3 * W_SB + 3 * W_SA
COL_K_IX = COL_Q_IX + H_IDX * D_IDX
COL_W_IX = COL_K_IX + D_IDX
IN_COLS = COL_W_IX + H_IDX

LANE = 128
VMEM_LIMIT = 56 * 1024 * 1024

ATT_SCALE = D_HEAD ** -0.5
LOG2E = 1.4426950408889634
IDX_SCALE = (D_IDX ** -0.5) * (H_IDX ** -0.5)
NEG = -1e30
INT_MIN = -2 ** 31
SB_DEAD = -104.0


def _cparams(sem):
    return pltpu.CompilerParams(dimension_semantics=sem, vmem_limit_bytes=VMEM_LIMIT)


def _adaln_kernel(c_ref, w_ref, b_ref, o_ref):
    c = c_ref[...]
    a = c * jax.nn.sigmoid(c)
    o_ref[...] = jnp.dot(a.astype(BF16), w_ref[...].astype(BF16),
                         preferred_element_type=F32) + b_ref[...]


def _adaln(c, w, b):
    R, D = c.shape
    N = w.shape[1]
    tn = 1024
    return pl.pallas_call(
        _adaln_kernel,
        grid=(N // tn,),
        in_specs=[pl.BlockSpec((R, D), lambda n: (0, 0)),
                  pl.BlockSpec((D, tn), lambda n: (0, n)),
                  pl.BlockSpec((1, tn), lambda n: (0, n))],
        out_specs=pl.BlockSpec((R, tn), lambda n: (0, n)),
        out_shape=jax.ShapeDtypeStruct((R, N), F32),
        compiler_params=_cparams(("arbitrary",)),
        name="adaln",
    )(c, w, b)


NORM_ROWS = 16


def _norm_mod_store(x_ref, g_ref, sc_ref, sh_ref, h_ref):
    bt, lt, _ = x_ref.shape
    g = g_ref[...]
    for b in range(bt):
        sc1 = 1.0 + sc_ref[b]
        sh = sh_ref[b]
        for r0 in range(0, lt, NORM_ROWS):
            r1 = min(r0 + NORM_ROWS, lt)
            x = x_ref[b, r0:r1, :]
            ms = jnp.mean(x * x, axis=-1, keepdims=True)
            y = x * lax.rsqrt(ms + EPS) * g
            h_ref[b * lt + r0:b * lt + r1, :] = (y * sc1 + sh).astype(h_ref.dtype)


class _Out(tuple):
    __slots__ = ()

    def __new__(cls, tile, lo, hi, dtype, layout="row", rows=None, scale=None):
        return tuple.__new__(cls, (tile, lo, hi, rows or (0, hi - lo), layout, dtype, scale))


def _proj_kernel(x_ref, sc_ref, sh_ref, g_ref, w_ref, *rest, plan, sigmoid):
    out_refs, h_ref = rest[:-1], rest[-1]
    bt, lt, D = x_ref.shape
    n = pl.program_id(2)

    @pl.when(n == 0)
    def _():
        _norm_mod_store(x_ref, g_ref, sc_ref, sh_ref, h_ref)

    r = jnp.dot(h_ref[...], w_ref[...], preferred_element_type=F32)
    if sigmoid:
        r = jnp.concatenate([jax.nn.sigmoid(r[r0:r0 + NORM_ROWS]) for r0 in
                             range(0, bt * lt, NORM_ROWS)], axis=0)
    for (tile, lo, hi, rows, layout, _, scale), o_ref in zip(plan, out_refs):
        def emit(o_ref=o_ref, lo=lo, hi=hi, rows=rows, layout=layout, scale=scale):
            v = r[:, lo:hi]
            if scale is not None:
                v = v * scale
            if layout == "feature":
                o_ref[...] = v.T[rows[0]:rows[1], :].astype(o_ref.dtype)
            elif layout == "heads":
                nh = (hi - lo) // D_HEAD
                v = pltpu.einshape("t(hd)->thd", v, h=nh)
                o_ref[...] = v.reshape(bt, lt, nh, D_HEAD).astype(o_ref.dtype)
            else:
                o_ref[...] = v.reshape(bt, lt, hi - lo).astype(o_ref.dtype)

        if tile is None:
            emit()
        else:
            pl.when(n == tile)(emit)


def _proj(x, sc, sh, g, w, plan, *, bt, lt, tn, sigmoid=False, name):
    B, L, D = x.shape
    N = w.shape[1]
    out_specs, out_shape = [], []
    for tile, lo, hi, rows, layout, dtype, _ in plan:
        if tile is None:
            out_specs.append(pl.BlockSpec((bt, lt, tn), lambda b, t, n: (b, t, n)))
            out_shape.append(jax.ShapeDtypeStruct((B, L, N), dtype))
        elif layout == "heads":
            nh = (hi - lo) // D_HEAD
            out_specs.append(pl.BlockSpec((bt, lt, nh, D_HEAD), lambda b, t, n: (b, t, 0, 0)))
            out_shape.append(jax.ShapeDtypeStruct((B, L, nh, D_HEAD), dtype))
        elif layout == "feature":
            assert bt == 1
            width = rows[1] - rows[0]
            out_specs.append(pl.BlockSpec((None, width, lt), lambda b, t, n: (b, 0, t)))
            out_shape.append(jax.ShapeDtypeStruct((B, width, L), dtype))
        else:
            out_specs.append(pl.BlockSpec((bt, lt, hi - lo), lambda b, t, n: (b, t, 0)))
            out_shape.append(jax.ShapeDtypeStruct((B, L, hi - lo), dtype))
    return pl.pallas_call(
        functools.partial(_proj_kernel, plan=tuple(plan), sigmoid=sigmoid),
        grid=(B // bt, L // lt, N // tn),
        in_specs=[pl.BlockSpec((bt, lt, D), lambda b, t, n: (b, t, 0)),
                  pl.BlockSpec((bt, 1, D), lambda b, t, n: (b, 0, 0)),
                  pl.BlockSpec((bt, 1, D), lambda b, t, n: (b, 0, 0)),
                  pl.BlockSpec((1, D), lambda b, t, n: (0, 0)),
                  pl.BlockSpec((D, tn), lambda b, t, n: (0, n))],
        out_specs=out_specs,
        out_shape=out_shape,
        scratch_shapes=[pltpu.VMEM((bt * lt, D), BF16)],
        compiler_params=_cparams(("arbitrary", "arbitrary", "arbitrary")),
        name=name,
    )(x, sc, sh, g, w)


def _sb_kernel(q_ref, k_ref, v_ref, u_ref, o_ref, *, tq, tk, past_len, hp):
    i = pl.program_id(2)
    qpos0 = past_len + i * tq
    qpos = qpos0 + lax.broadcasted_iota(I32, (tq, 1), 0)
    j0 = (qpos0 + tq - 2) // tk
    u = u_ref[...]

    def cond(carry):
        j, go, _, _ = carry
        return jnp.logical_and(j >= 0, go)

    def body(carry):
        j, _, cums, accs = carry
        off = pl.multiple_of(j * tk, tk)
        kpos = j * tk + lax.broadcasted_iota(I32, (1, tk), 1)
        mask = kpos < qpos
        new_cums, new_accs = [], []
        for h in range(hp):
            cs = slice(h * D_HEAD, (h + 1) * D_HEAD)
            z = lax.dot_general(q_ref[:, cs], k_ref[pl.ds(off, tk), cs], _NT,
                                preferred_element_type=F32) * (ATT_SCALE * LOG2E)
            t = jnp.log2(1.0 + jnp.exp2(-jnp.abs(z)))
            lk = jnp.where(mask, -(jnp.maximum(z, 0.0) + t), 0.0)
            hi = lk.astype(BF16)
            lo = (lk - hi.astype(F32)).astype(BF16)
            after = cums[h] + (jnp.dot(hi, u, preferred_element_type=F32)
                               + jnp.dot(lo, u, preferred_element_type=F32))
            lsz = jnp.minimum(z, 0.0) - t
            a = jnp.where(mask, jnp.exp2(lsz + after), 0.0)
            new_accs.append(accs[h] + jnp.dot(a.astype(BF16), v_ref[pl.ds(off, tk), cs],
                                              preferred_element_type=F32))
            new_cums.append(cums[h] + jnp.sum(lk, axis=1, keepdims=True))
        top = new_cums[0]
        for c in new_cums[1:]:
            top = jnp.maximum(top, c)
        return j - 1, jnp.max(top) > SB_DEAD * LOG2E, tuple(new_cums), tuple(new_accs)

    init = (j0, jnp.bool_(True), (jnp.zeros((tq, 1), F32),) * hp,
            (jnp.zeros((tq, D_HEAD), F32),) * hp)
    _, _, _, accs = lax.while_loop(cond, body, init)
    for h in range(hp):
        o_ref[:, h * D_HEAD:(h + 1) * D_HEAD] = accs[h].astype(o_ref.dtype)


def _sb_attention(q, k_all, v_all, *, tq, tk, past_len, hp=4):
    B, L, _ = q.shape
    Lkp = k_all.shape[1]
    r = np.arange(tk)
    u = jnp.asarray((r[:, None] > r[None, :]).astype(np.float32), BF16)
    wp = hp * D_HEAD
    kv_mode = dict(pipeline_mode=pl.Buffered(1)) if L // tq > 1 else {}
    return pl.pallas_call(
        functools.partial(_sb_kernel, tq=tq, tk=tk, past_len=past_len, hp=hp),
        grid=(B, H_SB // hp, L // tq),
        in_specs=[pl.BlockSpec((None, tq, wp), lambda b, h, i: (b, i, h)),
                  pl.BlockSpec((None, Lkp, wp), lambda b, h, i: (b, 0, h), **kv_mode),
                  pl.BlockSpec((None, Lkp, wp), lambda b, h, i: (b, 0, h), **kv_mode),
                  pl.BlockSpec((tk, tk), lambda b, h, i: (0, 0))],
        out_specs=pl.BlockSpec((None, tq, wp), lambda b, h, i: (b, i, h)),
        out_shape=jax.ShapeDtypeStruct((B, L, W_SB), BF16),
        compiler_params=_cparams(("arbitrary", "arbitrary", "arbitrary")),
        name="sb_attn",
    )(q, k_all, v_all, u)


def _bucket_edges():
    nb = N_BUCKETS // 2
    max_exact = nb // 2

    def bucket(rel):
        n = abs(rel)
        if n < max_exact:
            v = n
        else:
            v = max_exact + int(np.log(np.float32(n) / max_exact)
                                / np.log(REL_MAX_DIST / max_exact) * (nb - max_exact))
            v = min(v, nb - 1)
        return (nb if rel > 0 else 0) + v

    lo = -4 * REL_MAX_DIST
    assert bucket(lo) == nb - 1
    edges = []
    prev = bucket(lo)
    for rel in range(lo + 1, CHUNK):
        bk = bucket(rel)
        if bk != prev:
            edges.append((rel, prev))
            prev = bk
    edges.append((CHUNK, prev))
    return edges


_EDGES = _bucket_edges()
_FAR_BUCKET = _EDGES[0][1]
_NEAR_REL = _EDGES[0][0]


def _bias_kernel(d_ref, rel_ref, o_ref):
    tk, tq = o_ref.shape
    h = pl.program_id(1)
    rel = (lax.broadcasted_iota(I32, (tk, tq), 0) - lax.broadcasted_iota(I32, (tk, tq), 1)
           - d_ref[pl.program_id(0)])
    tile = jnp.full((tk, tq), rel_ref[_EDGES[-1][1], h], F32)
    for edge, bk in reversed(_EDGES[:-1]):
        tile = jnp.where(rel < edge, rel_ref[bk, h], tile)
    o_ref[...] = (tile - rel_ref[_FAR_BUCKET, h]) * LOG2E


def _bias_tiles(ds, rel_table, *, tq, tk):
    return pl.pallas_call(
        _bias_kernel,
        grid=(len(ds), H_SA),
        in_specs=[pl.BlockSpec(memory_space=pltpu.SMEM), pl.BlockSpec(memory_space=pltpu.SMEM)],
        out_specs=pl.BlockSpec((None, None, tk, tq), lambda n, h: (n, h, 0, 0)),
        out_shape=jax.ShapeDtypeStruct((len(ds), H_SA, tk, tq), F32),
        compiler_params=_cparams(("arbitrary", "arbitrary")),
        name="bias_tiles",
    )(jnp.asarray(np.asarray(ds, np.int32)), rel_table)


_NT = (((1,), (1,)), ((), ()))


def _dsa_kernel(itab, jtab, ftab, ltab, ntab, nbtab, fartab,
                qit_ref, wt_ref, qt_ref, ki_ref, k_ref, vt_ref, bias_ref, o_ref,
                keys_ref, half_ref, thr_ref, m_ref, l_ref, acc_ref, madd_ref, s_ref, p_ref,
                *, tq, tk, past_len, sub, csub):
    p = pl.program_id(1)
    i = itab[p]
    j = jtab[p]
    qpos = past_len + i * tq + lax.broadcasted_iota(I32, (1, tq), 1)
    lim = (qpos // CHUNK + 1) * CHUNK

    @pl.when(ftab[p] == 1)
    def _():
        wt = wt_ref[...] * IDX_SCALE

        def score_blk(c, carry):
            off = pl.multiple_of(c * sub, sub)
            kib = ki_ref[pl.ds(off, sub), :]
            s = jnp.zeros((sub, tq), F32)
            for h in range(H_IDX):
                sh = jnp.dot(kib, qit_ref[h * D_IDX:(h + 1) * D_IDX, :],
                             preferred_element_type=F32)
                s = s + wt[h:h + 1, :] * jnp.maximum(sh, 0.0)
            bits = pltpu.bitcast(s, I32)
            key = jnp.where(bits < 0, bits ^ jnp.int32(0x7FFFFFFF), bits)
            kpos = c * sub + lax.broadcasted_iota(I32, (sub, 1), 0)
            key = jnp.where(kpos < lim, key, jnp.int32(INT_MIN))
            keys_ref[pl.ds(off, sub), :] = key
            half_ref[pl.ds(off, sub), :] = (key >> 16).astype(jnp.int16)
            return carry

        per = tk // sub

        def score_grp(g, carry):
            for u in range(per):
                score_blk(g * per + u, carry)
            return carry

        lax.fori_loop(0, nbtab[p], score_grp, 0)

        need = jnp.minimum(TOPK_MAX, lim)
        nchunk = nbtab[p] * (tk // csub)
        i16_min = -2 ** 15

        def count_ge(cand):
            c16 = jnp.broadcast_to(cand, (16, tq)).astype(jnp.int16)

            def cnt_blk(c, acc):
                kb = half_ref[pl.ds(pl.multiple_of(c * csub, csub), csub), :]
                parts = [jnp.where(kb[16 * r:16 * (r + 1), :] >= c16, jnp.int16(1), jnp.int16(0))
                         for r in range(csub // 16)]
                while len(parts) > 1:
                    parts = [a + b for a, b in zip(parts[::2], parts[1::2])]
                return acc + parts[0]

            acc = lax.fori_loop(0, nchunk, cnt_blk, jnp.zeros((16, tq), jnp.int16))
            return jnp.sum(acc.astype(I32), axis=0, keepdims=True)

        def kth_half(want):
            def bit_step(it, carry):
                tpre, reach = carry
                bitv = lax.shift_left(jnp.int32(1), 15 - it)
                cnt = count_ge((tpre | bitv) + i16_min)
                hit = cnt >= want
                return jnp.where(hit, tpre | bitv, tpre), jnp.where(hit, cnt, reach)

            init = (jnp.zeros((1, tq), I32), jnp.full((1, tq), -1, I32))
            return lax.fori_loop(0, 16, bit_step, init)

        hi = kth_half(need)[0] + i16_min
        above = jnp.where(hi == 2 ** 15 - 1, 0, count_ge(jnp.minimum(hi + 1, 2 ** 15 - 1)))

        def low_halves(c, bucket):
            rows = pl.ds(pl.multiple_of(c * csub, csub), csub)
            k32 = keys_ref[rows, :]
            lo = (k32 & 0xFFFF) + i16_min
            same = (k32 >> 16) == hi
            half_ref[rows, :] = jnp.where(same, lo, i16_min).astype(jnp.int16)
            return bucket + jnp.sum(jnp.where(same, 1.0, 0.0).reshape(csub // 8, 8, tq), axis=0)

        bucket = lax.fori_loop(0, nchunk, low_halves, jnp.zeros((8, tq), F32))
        bucket = jnp.sum(bucket, axis=0, keepdims=True).astype(I32)
        lo_half, reach = kth_half(need - above)
        thr = lax.shift_left(hi, 16) | lo_half
        thr_ref[...] = thr
        kept = above + jnp.where(lo_half > 0, reach, bucket)

        def rows_of(c):
            return c * csub + lax.broadcasted_iota(I32, (csub, 1), 0)

        def count_where(pred):
            def blk(c, acc):
                kb = keys_ref[pl.ds(pl.multiple_of(c * csub, csub), csub), :]
                hit = jnp.where(pred(kb, rows_of(c)), 1.0, 0.0)
                return acc + jnp.sum(hit.reshape(csub // 8, 8, tq), axis=0)

            acc = lax.fori_loop(0, nchunk, blk, jnp.zeros((8, tq), F32))
            return jnp.sum(acc, axis=0, keepdims=True).astype(I32)

        excess = kept - need

        @pl.when(jnp.max(excess.astype(F32)) > 0.0)
        def _():
            keep = count_where(lambda kb, _: kb == thr) - excess
            nbits = keys_ref.shape[0].bit_length()

            def bit_step(it, bound):
                cand = bound | lax.shift_left(jnp.int32(1), nbits - 1 - it)
                cnt = count_where(lambda kb, rows: jnp.where(rows < cand, kb, thr - 1) == thr)
                return jnp.where(cnt <= keep, cand, bound)

            bound = lax.fori_loop(0, nbits, bit_step, jnp.zeros((1, tq), I32))

            def demote(c, carry):
                rows = pl.ds(pl.multiple_of(c * csub, csub), csub)
                kb = keys_ref[rows, :]
                late_tie = jnp.where(rows_of(c) >= bound, kb, thr - 1) == thr
                keys_ref[rows, :] = jnp.where(late_tie, thr - 1, kb)
                return carry

            lax.fori_loop(0, nchunk, demote, 0)

        m_ref[...] = jnp.full(m_ref.shape, NEG, F32)
        l_ref[...] = jnp.zeros(l_ref.shape, F32)
        acc_ref[...] = jnp.zeros(acc_ref.shape, F32)

    thr = thr_ref[...]
    for c in range(tk // sub):
        rows = pl.ds(pl.multiple_of(j * tk + c * sub, sub), sub)
        madd_ref[c * sub:(c + 1) * sub, :] = jnp.where(keys_ref[rows, :] >= thr, 0.0, NEG)

    def logits(h, near):
        cs = slice(h * D_HEAD, (h + 1) * D_HEAD)
        qh = qt_ref[cs, :]
        mx = jnp.full((8, tq), NEG, F32)
        for c in range(tk // sub):
            rs = slice(c * sub, (c + 1) * sub)
            s = jnp.dot(k_ref[rs, cs], qh, preferred_element_type=F32)
            extra = bias_ref[h, rs, :] + madd_ref[rs, :] if near else madd_ref[rs, :]
            s = s + extra
            s_ref[h % s_ref.shape[0], rs, :] = s
            mx = jnp.maximum(mx, jnp.max(s.reshape(sub // 8, 8, tq), axis=0))
        return jnp.max(mx, axis=0, keepdims=True)

    def attend(near):
        def weighted_values(h, alpha):
            cs = slice(h * D_HEAD, (h + 1) * D_HEAD)
            vt_ones = jnp.concatenate([vt_ref[cs, :], jnp.ones((16, tk), BF16)], axis=0)
            pv = jnp.dot(vt_ones, p_ref[h % 2], preferred_element_type=F32)
            l_ref[h:h + 1, :] = alpha * l_ref[h:h + 1, :] + pv[D_HEAD:D_HEAD + 1, :]
            acc_ref[cs, :] = alpha * acc_ref[cs, :] + pv[:D_HEAD, :]

        ahead = s_ref.shape[0] - 1
        smaxes = [logits(h, near) for h in range(ahead)]
        alpha_prev = None
        for h in range(H_SA):
            m_prev = m_ref[h:h + 1, :]
            m_new = jnp.maximum(m_prev, smaxes[h])
            if h + ahead < H_SA:
                smaxes.append(logits(h + ahead, near))
            alpha = jnp.exp2(m_prev - m_new)
            for c in range(tk // sub):
                rs = slice(c * sub, (c + 1) * sub)
                p_ref[h % 2, rs, :] = jnp.exp2(
                    s_ref[h % (ahead + 1), rs, :] - m_new).astype(BF16)
            m_ref[h:h + 1, :] = m_new
            if h > 0:
                weighted_values(h - 1, alpha_prev)
            alpha_prev = alpha
        weighted_values(H_SA - 1, alpha_prev)

    pl.when(fartab[p] == 0)(functools.partial(attend, True))
    pl.when(fartab[p] == 1)(functools.partial(attend, False))

    @pl.when(ltab[p] == 1)
    def _():
        for h in range(H_SA):
            cs = slice(h * D_HEAD, (h + 1) * D_HEAD)
            o_ref[cs, :] = (acc_ref[cs, :] / l_ref[h:h + 1, :]).astype(o_ref.dtype)


def _dsa_attention(qit, wt, qt, ki_all, k_all, vt_all, rel_table, *, tq, tk, past_len):
    B, _, L = qt.shape
    Lkp = k_all.shape[1]
    assert Lkp % tk == 0 and L % tq == 0
    nq = L // tq
    it, jt, ft, lt_, nt, nbt, fart = [], [], [], [], [], [], []
    near_ds = []
    for i in range(nq):
        qpos0 = past_len + i * tq
        lim_max = ((qpos0 + tq - 1) // CHUNK + 1) * CHUNK
        jmax = (lim_max - 1) // tk
        for j in range(jmax + 1):
            d = qpos0 - j * tk
            if (tk - 1) - d >= _NEAR_REL:
                if d not in near_ds:
                    near_ds.append(d)
                nt.append(near_ds.index(d))
                fart.append(0)
            else:
                nt.append(nt[-1] if nt else 0)
                fart.append(1)
            it.append(i); jt.append(j); ft.append(int(j == 0)); lt_.append(int(j == jmax))
            nbt.append(jmax + 1)
    bias = _bias_tiles(near_ds, rel_table, tq=tq, tk=tk)
    tabs = [jnp.asarray(np.asarray(t, np.int32)) for t in (it, jt, ft, lt_, nt, nbt, fart)]
    grid_spec = pltpu.PrefetchScalarGridSpec(
        num_scalar_prefetch=7,
        grid=(B, len(it)),
        in_specs=[
            pl.BlockSpec((None, H_IDX * D_IDX, tq), lambda b, p, it, jt, *_: (b, 0, it[p])),
            pl.BlockSpec((None, H_IDX, tq), lambda b, p, it, jt, *_: (b, 0, it[p])),
            pl.BlockSpec((None, W_SA, tq), lambda b, p, it, jt, *_: (b, 0, it[p])),
            pl.BlockSpec((None, Lkp, D_IDX), lambda b, p, it, jt, *_: (b, 0, 0),
                         pipeline_mode=pl.Buffered(1)),
            pl.BlockSpec((None, tk, W_SA), lambda b, p, it, jt, *_: (b, jt[p], 0)),
            pl.BlockSpec((None, W_SA, tk), lambda b, p, it, jt, *_: (b, 0, jt[p])),
            pl.BlockSpec((None, H_SA, tk, tq),
                         lambda b, p, it, jt, ft, lt, nt, *_: (nt[p], 0, 0, 0)),
        ],
        out_specs=pl.BlockSpec((None, W_SA, tq), lambda b, p, it, jt, *_: (b, 0, it[p])),
        scratch_shapes=[
            pltpu.VMEM((Lkp, tq), I32),
            pltpu.VMEM((Lkp, tq), jnp.int16),
            pltpu.VMEM((1, tq), I32),
            pltpu.VMEM((H_SA, tq), F32),
            pltpu.VMEM((H_SA, tq), F32),
            pltpu.VMEM((W_SA, tq), F32),
            pltpu.VMEM((tk, tq), F32),
            pltpu.VMEM((3, tk, tq), F32),
            pltpu.VMEM((2, tk, tq), BF16),
        ],
    )
    return pl.pallas_call(
        functools.partial(_dsa_kernel, tq=tq, tk=tk, past_len=past_len,
                          sub=min(tk, 128), csub=min(tk, 512)),
        grid_spec=grid_spec,
        out_shape=jax.ShapeDtypeStruct((B, W_SA, L), BF16),
        compiler_params=_cparams(("arbitrary", "arbitrary")),
        name="dsa_attn",
    )(*tabs, qit, wt, qt, ki_all, k_all, vt_all, bias)


def _merge_kernel(osb_ref, osa_ref, x_ref, sc_ref, sh_ref, gt_ref, g_ref, wg_ref, wsb_ref, wsa_ref,
                  wo_ref, o_ref, h_ref, m_ref):
    bt, lt, D = x_ref.shape
    tm = bt * lt
    _norm_mod_store(x_ref, g_ref, sc_ref, sh_ref, h_ref)
    gate = jnp.dot(h_ref[...], wg_ref[...], preferred_element_type=F32)
    a = jnp.dot(osb_ref[...].reshape(tm, W_SB), wsb_ref[...], preferred_element_type=F32)
    c = jnp.dot(osa_ref[...].reshape(tm, W_SA), wsa_ref[...], preferred_element_type=F32)
    for r0 in range(0, tm, NORM_ROWS):
        rs = slice(r0, r0 + NORM_ROWS)
        g = jax.nn.sigmoid(gate[rs])
        m_ref[rs, :] = (g[:, :D] * a[rs] + g[:, D:] * c[rs]).astype(BF16)
    y = jnp.dot(m_ref[...], wo_ref[...], preferred_element_type=F32)
    o_ref[...] = x_ref[...] + gt_ref[...] * y.reshape(bt, lt, D)


def _merge(osb, osa, x, sc, sh, gt, g, wg, wsb, wsa, wo, *, bt, lt):
    B, L, D = x.shape
    row = lambda w: pl.BlockSpec((bt, lt, w), lambda b, t: (b, t, 0))
    mod = pl.BlockSpec((bt, 1, D), lambda b, t: (b, 0, 0))
    full = lambda a: pl.BlockSpec(a.shape, lambda b, t: (0, 0), pipeline_mode=pl.Buffered(1))
    return pl.pallas_call(
        _merge_kernel,
        grid=(B // bt, L // lt),
        in_specs=[row(W_SB), row(W_SA), row(D), mod, mod, mod, full(g),
                  full(wg), full(wsb), full(wsa), full(wo)],
        out_specs=row(D),
        out_shape=jax.ShapeDtypeStruct((B, L, D), F32),
        scratch_shapes=[pltpu.VMEM((bt * lt, D), BF16), pltpu.VMEM((bt * lt, D), BF16)],
        compiler_params=_cparams(("arbitrary", "arbitrary")),
        name="merge_out",
    )(osb, osa, x, sc, sh, gt, g, wg, wsb, wsa, wo)


HALO = 8
FFN_ROWS = 32


def _ffn_kernel(x_ref, sc_ref, sh_ref, gt_ref, g_ref, gfin_ref, prev_ref, wg_ref, wv_ref,
                cw_ref, cb_ref, wd_ref, x2_ref, st_ref,
                h_ref, acc_ref, ext_ref, carry_ref, uv_ref, act_ref, *, final):
    bt, lt, D = x_ref.shape
    tm = bt * lt
    fc = wg_ref.shape[1]
    t = pl.program_id(1)
    f = pl.program_id(2)

    @pl.when(f == 0)
    def _():
        _norm_mod_store(x_ref, g_ref, sc_ref, sh_ref, h_ref)
        acc_ref[...] = jnp.zeros(acc_ref.shape, F32)

    h = h_ref[...]

    @pl.when(t == 0)
    def _():
        ext_ref[:, HALO - 2:HALO, :] = prev_ref[...]

    @pl.when(t > 0)
    def _():
        ext_ref[:, HALO - 2:HALO, :] = carry_ref[f]

    ext_ref[:, HALO:HALO + lt, :] = jnp.dot(
        h, wg_ref[...], preferred_element_type=F32).reshape(bt, lt, fc)
    uv_ref[...] = jnp.dot(h, wv_ref[...], preferred_element_type=F32).reshape(bt, lt, fc)
    tail = ext_ref[:, HALO + lt - 2:HALO + lt, :]
    carry_ref[f] = tail
    st_ref[:, f] = tail
    cw = cw_ref[...]
    cb = cb_ref[...]
    for b in range(bt):
        for r0 in range(0, lt, FFN_ROWS):
            r1 = min(r0 + FFN_ROWS, lt)
            e = [ext_ref[b, HALO - 2 + j + r0:HALO - 2 + j + r1, :] for j in range(CONV_W)]
            conv = cb + cw[0:1, :] * e[0] + cw[1:2, :] * e[1] + cw[2:3, :] * e[2]
            act = conv * jax.nn.sigmoid(conv) * uv_ref[b, r0:r1, :]
            act_ref[b * lt + r0:b * lt + r1, :] = act.astype(BF16)
    acc_ref[...] += jnp.dot(act_ref[...], wd_ref[...], preferred_element_type=F32)

    @pl.when(f == pl.num_programs(2) - 1)
    def _():
        gfin = gfin_ref[...]
        for b in range(bt):
            gt = gt_ref[b]
            for r0 in range(0, lt, NORM_ROWS):
                r1 = min(r0 + NORM_ROWS, lt)
                x2 = x_ref[b, r0:r1, :] + gt * acc_ref[b * lt + r0:b * lt + r1, :]
                if final:
                    ms = jnp.mean(x2 * x2, axis=-1, keepdims=True)
                    x2 = x2 * lax.rsqrt(ms + EPS) * gfin
                x2_ref[b, r0:r1, :] = x2


def _ffn(x, sc, sh, gt, g, gfin, prev, w_up, cw, cb, w_down, *, bt, lt, fc, final):
    B, L, D = x.shape
    F = w_down.shape[0]
    nf = F // fc
    row = pl.BlockSpec((bt, lt, D), lambda b, t, f: (b, t, 0))
    mod = pl.BlockSpec((bt, 1, D), lambda b, t, f: (b, 0, 0))
    vec = pl.BlockSpec((1, D), lambda b, t, f: (0, 0))
    return pl.pallas_call(
        functools.partial(_ffn_kernel, final=final),
        grid=(B // bt, L // lt, nf),
        in_specs=[row, mod, mod, mod, vec, vec,
                  pl.BlockSpec((bt, CONV_W - 1, fc), lambda b, t, f: (b, 0, f)),
                  pl.BlockSpec((D, fc), lambda b, t, f: (0, f)),
                  pl.BlockSpec((D, fc), lambda b, t, f: (0, nf + f)),
                  pl.BlockSpec((CONV_W, fc), lambda b, t, f: (0, f)),
                  pl.BlockSpec((1, fc), lambda b, t, f: (0, f)),
                  pl.BlockSpec((fc, D), lambda b, t, f: (f, 0))],
        out_specs=[row, pl.BlockSpec((bt, nf, CONV_W - 1, fc), lambda b, t, f: (b, 0, 0, 0))],
        out_shape=[jax.ShapeDtypeStruct((B, L, D), F32),
                   jax.ShapeDtypeStruct((B, nf, CONV_W - 1, fc), F32)],
        scratch_shapes=[pltpu.VMEM((bt * lt, D), BF16),
                        pltpu.VMEM((bt * lt, D), F32),
                        pltpu.VMEM((bt, HALO + lt, fc), F32),
                        pltpu.VMEM((nf, bt, CONV_W - 1, fc), F32),
                        pltpu.VMEM((bt, lt, fc), F32),
                        pltpu.VMEM((bt * lt, fc), BF16)],
        compiler_params=_cparams(("arbitrary", "arbitrary", "arbitrary")),
        name="conv_ffn",
    )(x, sc, sh, gt, g, gfin, prev, w_up, w_up, cw, cb, w_down)


CACHE_BLOCK = 256


def _cache_kernel(c_ref, n_ref, o_ref, *, feature_major):
    blk = CACHE_BLOCK
    P = c_ref.shape[0]
    lkp = o_ref.shape[1] if feature_major else o_ref.shape[0]

    def put(r, v):
        rows = slice(r * blk, (r + 1) * blk)
        if feature_major:
            o_ref[:, rows] = v.T.astype(o_ref.dtype)
        else:
            o_ref[rows, :] = v.astype(o_ref.dtype)

    for r in range(P // blk):
        c = c_ref[r * blk:(r + 1) * blk]
        put(r, pltpu.einshape("thd->t(hd)", c) if c.ndim == 3 else c)
    new = n_ref[...].astype(F32)
    put(P // blk, jnp.concatenate(
        [new, jnp.zeros((blk - new.shape[0], new.shape[1]), F32)], axis=0))
    for r in range(P // blk + 1, lkp // blk):
        put(r, jnp.zeros((blk, new.shape[1]), F32))


def _cache_keys(cache, new, lkp, feature_major):
    B, P = cache.shape[:2]
    L, W = new.shape[1:]
    blk = CACHE_BLOCK
    assert P % blk == 0 and lkp % blk == 0 and L <= blk and lkp >= P + L
    cblock = (None,) + cache.shape[1:]
    cmap = (lambda b: (b, 0, 0, 0)) if cache.ndim == 4 else (lambda b: (b, 0, 0))
    oshape = (W, lkp) if feature_major else (lkp, W)
    return pl.pallas_call(
        functools.partial(_cache_kernel, feature_major=feature_major),
        grid=(B,),
        in_specs=[pl.BlockSpec(cblock, cmap),
                  pl.BlockSpec((None, L, W), lambda b: (b, 0, 0))],
        out_specs=pl.BlockSpec((None,) + oshape, lambda b: (b, 0, 0)),
        out_shape=jax.ShapeDtypeStruct((B,) + oshape, BF16),
        compiler_params=_cparams(("arbitrary",)),
        name="cache_keys",
    )(cache, new)


def _pad_cols(w, mult):
    n = w.shape[1]
    npad = -(-n // mult) * mult
    return jnp.pad(w, ((0, 0), (0, npad - n)))


def _tiles(B, L, rows):
    if L >= rows:
        assert L % rows == 0
        return 1, rows
    assert rows % L == 0 and B % (rows // L) == 0
    return rows // L, L


def _layer(x, mod, past, wts, *, last, g_final):
    (g_mix, w_in_sb, w_in_sa, w_in_ix, w_gate, w_br_sb, w_br_sa, w_out, rel_table, g_ffn, w_up,
     conv_w, conv_b, w_down) = wts
    B, L, D = x.shape
    sh1, sc1, gt1, sh2, sc2, gt2 = (m[:, None, :] for m in jnp.split(mod, 6, axis=-1))
    past_len = 0 if past is None else past[0].shape[1]

    wn = W_SB
    bt, lt = _tiles(B, L, 512)
    fm = bt == 1
    qlay = "feature" if fm else "row"
    call = functools.partial(_proj, x, sc1, sh1, g_mix, bt=bt, lt=lt)
    q_sb, k_sb, kb_sb, v_sb, vb_sb = call(
        w_in_sb, [_Out(0, 0, wn, BF16), _Out(1, 0, wn, F32, "heads"), _Out(1, 0, wn, BF16),
                  _Out(2, 0, wn, F32, "heads"), _Out(2, 0, wn, BF16)], tn=wn, name="proj_sb")
    q_sa, k_sa, kb_sa, v_sa, vb_sa = call(
        w_in_sa, [_Out(0, 0, wn, BF16, qlay, scale=ATT_SCALE * LOG2E),
                  _Out(1, 0, wn, F32, "heads"), _Out(1, 0, wn, BF16),
                  _Out(2, 0, wn, F32, "heads"), _Out(2, 0, wn, BF16, qlay)],
        tn=wn, name="proj_sa")
    w_ix_out = (_Out(1, 0, LANE, F32, "feature", (D_IDX, D_IDX + H_IDX)) if fm
                else _Out(1, D_IDX, D_IDX + H_IDX, F32))
    q_ix, k_ix, kb_ix, w_ix = call(
        w_in_ix, [_Out(0, 0, wn, BF16, qlay), _Out(1, 0, D_IDX, F32), _Out(1, 0, D_IDX, BF16),
                  w_ix_out], tn=wn, name="proj_ix")
    if not fm:
        q_sa, q_ix, w_ix = (jnp.swapaxes(a, 1, 2) for a in (q_sa, q_ix, w_ix))

    tq = min(256, L)
    tk_sb = 256
    tk_sa = 512 if L >= 512 else 256
    l_keys = past_len + L

    def keys_of(new, old, tk, feature_major=False):
        if old is None:
            assert l_keys % tk == 0
            return new
        assert not fm
        return _cache_keys(old, new, -(-l_keys // tk) * tk, feature_major)

    old = (None,) * 5 if past is None else past[:5]
    o_sb = _sb_attention(q_sb, keys_of(kb_sb, old[0], tk_sb), keys_of(vb_sb, old[1], tk_sb),
                         tq=tq, tk=tk_sb, past_len=past_len)
    assert fm or past is not None
    vt = keys_of(vb_sa, old[3], tk_sa, True)
    o_sa_t = _dsa_attention(q_ix, w_ix, q_sa, keys_of(kb_ix, old[4], tk_sa),
                            keys_of(kb_sa, old[2], tk_sa), vt, rel_table,
                            tq=tq, tk=tk_sa, past_len=past_len)
    o_sa = jnp.swapaxes(o_sa_t, 1, 2)

    bt, lt = _tiles(B, L, 256)
    x1 = _merge(o_sb, o_sa, x, sc1, sh1, gt1, g_mix, w_gate, w_br_sb, w_br_sa, w_out,
                bt=bt, lt=lt)

    prev = jnp.zeros((B, CONV_W - 1, w_down.shape[0]), F32) if past is None else past[5]
    bt, lt = _tiles(B, L, 512)
    x2, conv_state = _ffn(x1, sc2, sh2, gt2, g_ffn, g_final, prev, w_up, conv_w, conv_b,
                          w_down, bt=bt, lt=lt, fc=512, final=last)
    conv_state = jnp.swapaxes(conv_state, 1, 2).reshape(B, CONV_W - 1, -1)
    return x2, (k_sb, v_sb, k_sa, v_sa, k_ix, conv_state)


def kernel(x_prompt, x_sample, cache_sb_k, cache_sb_v, cache_sa_k, cache_sa_v, cache_idx_k,
           state_ffn_conv, c_prompt, c_sample, w_ada, b_ada, g_mix, w_in, w_gate, w_br_sb,
           w_br_sa, w_out, rel_table, g_ffn, w_up, conv_w, conv_b, w_down, g_final):
    depth = w_ada.shape[0]
    nbp = c_prompt.shape[0]
    nbs = c_sample.shape[0]
    rows = -(-(nbp + nbs) // 8) * 8
    c_all = jnp.concatenate([c_prompt, c_sample,
                             jnp.zeros((rows - nbp - nbs, c_prompt.shape[1]), F32)], axis=0)
    xp, xs = x_prompt, x_sample
    new_p, new_s = [], []
    for l in range(depth):
        mod = _adaln(c_all, w_ada[l], b_ada[l][None, :])
        wb = w_in[l].astype(BF16)
        wts = (g_mix[l][None, :], wb[:, :COL_Q_SA], wb[:, COL_Q_SA:COL_Q_IX],
               _pad_cols(wb[:, COL_Q_IX:], 2 * W_SB), w_gate[l].astype(BF16),
               w_br_sb[l].astype(BF16), w_br_sa[l].astype(BF16), w_out[l].astype(BF16),
               rel_table, g_ffn[l][None, :], w_up[l].astype(BF16), conv_w[l],
               conv_b[l][None, :], w_down[l].astype(BF16))
        gfin = g_final[None, :]
        past = (cache_sb_k[l], cache_sb_v[l], cache_sa_k[l], cache_sa_v[l], cache_idx_k[l],
                state_ffn_conv[l])
        xp, sp = _layer(xp, mod[:nbp], None, wts, last=l == depth - 1, g_final=gfin)
        xs, ss = _layer(xs, mod[nbp:nbp + nbs], past, wts, last=l == depth - 1, g_final=gfin)
        new_p.append(sp)
        new_s.append(ss)
    stack = lambda states, n: jnp.stack([s[n] for s in states])
    return ((xp, xs) + tuple(stack(new_p, n) for n in range(6))
            + tuple(stack(new_s, n) for n in range(6)))
```
